```python
import jax
import jax.numpy as jnp
from jax import lax
import numpy as np


D_MODEL = 1024
BATCH = 8
SEQ = 4096
DEPTH = 2

GRID_W = 64
CTX_LEN = 256
N_MOD = 6
LN_EPS = 1e-6
DEEPNORM_ALPHA = (2 * DEPTH) ** 0.25
DEEPNORM_BETA = (8 * DEPTH) ** -0.25

MLA_HEADS = 8
MLA_Q_RANK = 384
MLA_KV_RANK = 256
MLA_NOPE = 64
MLA_ROPE = 32
MLA_V = 64
MLA_SCALE = (MLA_NOPE + MLA_ROPE) ** -0.5
ROPE_BASE = 10000.0
ATTN_BLOCK = 128

GLA_HEADS = 4
GLA_DK = 64
GLA_DV = 128
GLA_GATE_RANK = 16
GLA_TAU = 16.0
GLA_CHUNK = 64

EV_WIDTHS = (MLA_Q_RANK, MLA_KV_RANK, MLA_ROPE,
             GLA_HEADS * GLA_DK, GLA_HEADS * GLA_DK, GLA_HEADS * GLA_DV, GLA_HEADS * GLA_DV,
             GLA_GATE_RANK, GLA_GATE_RANK)
EV_SPLITS = tuple(sum(EV_WIDTHS[:i + 1]) for i in range(len(EV_WIDTHS) - 1))
EV_IN = sum(EV_WIDTHS)
MIX_OUT = MLA_HEADS * MLA_V + GLA_HEADS * GLA_DV

RG_WIDTH = 1280
RG_BLOCKS = 10
RG_BW = RG_WIDTH // RG_BLOCKS
RG_CONV = 4
RG_CONV_LEFT = 2
RG_C = 8.0

PEER_HEADS = 8
PEER_NK = 128
PEER_N = PEER_NK * PEER_NK
PEER_TOPK = 16
PEER_QDIM = 256
PEER_HALF = PEER_QDIM // 2
PEER_BLOCK = 128

kernel_name = 'hybrid_mla_gla_rglru_peer_diffusion_block'


def layer_norm(x, g, b):
    xf = x.astype(jnp.float32)
    mu = jnp.mean(xf, axis=-1, keepdims=True)
    var = jnp.mean(jnp.square(xf - mu), axis=-1, keepdims=True)
    return ((xf - mu) * lax.rsqrt(var + LN_EPS) * g + b).astype(x.dtype)


def rms_norm(x, g):
    xf = x.astype(jnp.float32)
    y = xf * lax.rsqrt(jnp.mean(jnp.square(xf), axis=-1, keepdims=True) + LN_EPS)
    return (y * g).astype(x.dtype)


def modulate(h, shift, scale):
    return h * (1.0 + scale) + shift


def tflip(z):
    return jnp.flip(z, axis=1)


def axial_rope(seq_len):
    rows = seq_len // GRID_W
    row = jnp.repeat(jnp.arange(rows, dtype=jnp.float32), GRID_W)
    col = jnp.tile(jnp.arange(GRID_W, dtype=jnp.float32), rows)
    n_freq = MLA_ROPE // 4
    inv = jnp.power(ROPE_BASE, -jnp.arange(n_freq, dtype=jnp.float32) / n_freq)
    ang = jnp.concatenate([row[:, None] * inv, col[:, None] * inv], axis=-1)
    return jnp.cos(ang), jnp.sin(ang)


def apply_rope(x, cos, sin):
    half = x.shape[-1] // 2
    x1, x2 = x[..., :half], x[..., half:]
    return jnp.concatenate([x1 * cos - x2 * sin, x1 * sin + x2 * cos], axis=-1).astype(x.dtype)


def mla_queries(c_q, q_norm, w_uq, cos, sin):
    q = (rms_norm(c_q, q_norm) @ w_uq).reshape(c_q.shape[:-1] + (MLA_HEADS, MLA_NOPE + MLA_ROPE))
    q_nope, q_pe = q[..., :MLA_NOPE], q[..., MLA_NOPE:]
    if cos is not None:
        q_pe = apply_rope(q_pe, cos[:, None, :], sin[:, None, :])
    return jnp.concatenate([q_nope, q_pe], axis=-1)


def mla_keys_values(c_kv, k_pe, kv_norm, w_ukv, cos, sin):
    kv = (rms_norm(c_kv, kv_norm) @ w_ukv).reshape(c_kv.shape[:-1] + (MLA_HEADS, MLA_NOPE + MLA_V))
    k_nope, v = kv[..., :MLA_NOPE], kv[..., MLA_NOPE:]
    if cos is not None:
        k_pe = apply_rope(k_pe, cos, sin)
    k_pe = jnp.broadcast_to(k_pe[..., None, :], k_nope.shape[:-1] + (MLA_ROPE,))
    return jnp.concatenate([k_nope, k_pe], axis=-1), v


def softmax_attend(q, k, v):
    s = jnp.einsum('bqhd,bkhd->bhqk', q, k).astype(jnp.float32) * MLA_SCALE
    p = jax.nn.softmax(s, axis=-1).astype(v.dtype)
    return jnp.einsum('bhqk,bkhd->bqhd', p, v)


def blocked_attend(q, k, v):
    b, s, h, d = q.shape
    nb = s // ATTN_BLOCK
    qb = q.reshape(b, nb, ATTN_BLOCK, h, d).transpose(1, 0, 2, 3, 4)
    ob = lax.map(lambda qi: softmax_attend(qi, k, v), qb)
    return ob.transpose(1, 0, 2, 3, 4).reshape(b, s, h, v.shape[-1])


def gla_project(gq, gk, gv, gaf, gab, wa2, ba):
    lead = gq.shape[:-1]
    q = gq.reshape(lead + (GLA_HEADS, GLA_DK)) * (GLA_DK ** -0.5)
    k = gk.reshape(lead + (GLA_HEADS, GLA_DK))
    v = gv.reshape(lead + (GLA_HEADS, GLA_DV))

    def log_decay(ga, w, bias):
        z = (ga @ w + bias).astype(jnp.float32)
        return (jax.nn.log_sigmoid(z) / GLA_TAU).reshape(lead + (GLA_HEADS, GLA_DK))

    return q, k, v, log_decay(gaf, wa2[0], ba[0]), log_decay(gab, wa2[1], ba[1])


def gla_chunked(q, k, v, log_a, s0):
    b, t, h, dk = q.shape
    dv = v.shape[-1]
    n = t // GLA_CHUNK

    def chunks(z):
        return z.astype(jnp.float32).reshape(b, n, GLA_CHUNK, h, z.shape[-1]).transpose(1, 0, 3, 2, 4)

    lower = jnp.tril(jnp.ones((GLA_CHUNK, GLA_CHUNK), dtype=bool))[:, :, None]

    def step(state, inp):
        qc, kc, vc, ac = inp
        cum = jnp.cumsum(ac, axis=2)
        diff = cum[:, :, :, None, :] - cum[:, :, None, :, :]
        decay = jnp.exp(jnp.where(lower, diff, -jnp.inf))
        scores = jnp.einsum('bhid,bhjd,bhijd->bhij', qc, kc, decay)
        out = scores @ vc + jnp.einsum('bhid,bhde->bhie', qc * jnp.exp(cum), state)
        last = cum[:, :, -1:, :]
        state = (jnp.exp(last[:, :, 0, :])[..., None] * state
                 + jnp.einsum('bhjd,bhje->bhde', kc * jnp.exp(last - cum), vc))
        return state, out

    s_fin, o = lax.scan(step, s0, (chunks(q), chunks(k), chunks(v), chunks(log_a)))
    o = o.transpose(1, 0, 3, 2, 4).reshape(b, t, h, dv).astype(v.dtype)
    return o, s_fin


def gla_final_state(k, v, log_a):
    cum = jnp.cumsum(log_a, axis=1)
    w = jnp.exp(cum[:, -1:] - cum)
    return jnp.einsum('bthd,bthe->bhde', k.astype(jnp.float32) * w, v.astype(jnp.float32))


def gla_output(o, g, norm):
    lead = o.shape[:-2]
    y = rms_norm(o, norm.reshape(GLA_HEADS, GLA_DV)).reshape(lead + (GLA_HEADS * GLA_DV,))
    return y * jax.nn.silu(g)


def mla_gla_mixer(h_ctx, h_lat, cos, sin, w_in, q_norm, kv_norm, w_uq, w_ukv,
                  gla_wa2, gla_ba, gla_norm, w_out, with_ctx_out):
    b = h_lat.shape[0]
    pc = jnp.split(h_ctx @ w_in, EV_SPLITS, axis=-1)
    px = jnp.split(h_lat @ w_in, EV_SPLITS, axis=-1)
    k_c, v_c = mla_keys_values(pc[1], pc[2], kv_norm, w_ukv, None, None)
    k_x, v_x = mla_keys_values(px[1], px[2], kv_norm, w_ukv, cos, sin)
    q_x = mla_queries(px[0], q_norm, w_uq, cos, sin)
    att_x = blocked_attend(q_x, jnp.concatenate([k_c, k_x], axis=1), jnp.concatenate([v_c, v_x], axis=1))
    gq_c, gk_c, gv_c, laf_c, lab_c = gla_project(pc[3], pc[4], pc[5], pc[7], pc[8], gla_wa2, gla_ba)
    gq_x, gk_x, gv_x, laf_x, lab_x = gla_project(px[3], px[4], px[5], px[7], px[8], gla_wa2, gla_ba)
    zero = jnp.zeros((b, GLA_HEADS, GLA_DK, GLA_DV), jnp.float32)
    if with_ctx_out:
        o_cf, s_f = gla_chunked(gq_c, gk_c, gv_c, laf_c, zero)
        o_cb, s_b = gla_chunked(tflip(gq_c), tflip(gk_c), tflip(gv_c), tflip(lab_c), zero)
    else:
        s_f = gla_final_state(gk_c, gv_c, laf_c)
        s_b = gla_final_state(tflip(gk_c), tflip(gv_c), tflip(lab_c))
    o_xf, _ = gla_chunked(gq_x, gk_x, gv_x, laf_x, s_f)
    o_xb, _ = gla_chunked(tflip(gq_x), tflip(gk_x), tflip(gv_x), tflip(lab_x), s_b)
    gla_x = gla_output(o_xf + tflip(o_xb), px[6], gla_norm)
    out_x = jnp.concatenate([att_x.reshape(att_x.shape[:2] + (MLA_HEADS * MLA_V,)), gla_x], axis=-1) @ w_out
    out_c = None
    if with_ctx_out:
        q_c = mla_queries(pc[0], q_norm, w_uq, None, None)
        att_c = softmax_attend(q_c, k_c, v_c)
        gla_c = gla_output(o_cf + tflip(o_cb), pc[6], gla_norm)
        out_c = jnp.concatenate([att_c.reshape(att_c.shape[:2] + (MLA_HEADS * MLA_V,)), gla_c], axis=-1) @ w_out
    return out_c, out_x


def dwconv_centred(z, w, bias):
    t = z.shape[1]
    zp = jnp.pad(z, ((0, 0), (RG_CONV_LEFT, RG_CONV - 1 - RG_CONV_LEFT), (0, 0)))
    y = bias + zp[:, 0:t] * w[0]
    for tap in range(1, RG_CONV):
        y = y + zp[:, tap:tap + t] * w[tap]
    return y


def rg_gates(z, gw, gb, lam):
    zf = z.astype(jnp.float32)
    zb = zf.reshape(z.shape[:-1] + (RG_BLOCKS, RG_BW))

    def gate(w, bias):
        return jax.nn.sigmoid(jnp.einsum('btnc,ncd->btnd', zb, w.astype(jnp.float32)).reshape(zf.shape) + bias)

    r = gate(gw[0], gb[0])
    i = gate(gw[1], gb[1])
    log_a = -RG_C * jax.nn.softplus(-lam.astype(jnp.float32)) * r
    x_in = jnp.sqrt(-jnp.expm1(2.0 * log_a)) * (i * zf)
    return log_a, x_in


def linear_scan(log_a, x_in, h0, reverse):
    a = jnp.exp(log_a)
    if h0 is not None:
        edge = -1 if reverse else 0
        x_in = x_in.at[:, edge].add(a[:, edge] * h0)

    def combine(left, right):
        a_l, b_l = left
        a_r, b_r = right
        return a_l * a_r, a_r * b_l + b_r

    _, h = lax.associative_scan(combine, (a, x_in), reverse=reverse, axis=1)
    return h, (h[:, 0] if reverse else h[:, -1])


def rglru_mixer(h_ctx, h_lat, w_in, conv_w, conv_b, gate_w, gate_b, lam, w_out, with_ctx_out):
    proj_x = h_lat @ w_in
    branch_x, rnn_x = proj_x[..., :RG_WIDTH], proj_x[..., RG_WIDTH:]
    rnn_c = h_ctx @ w_in[:, RG_WIDTH:]
    rnn_x = dwconv_centred(rnn_x, conv_w, conv_b)
    rnn_c = dwconv_centred(rnn_c, conv_w, conv_b)
    ys_x = []
    ys_c = []
    for d, reverse in enumerate((False, True)):
        la_c, in_c = rg_gates(rnn_c, gate_w[d], gate_b[d], lam[d])
        la_x, in_x = rg_gates(rnn_x, gate_w[d], gate_b[d], lam[d])
        y_c, h_ctx_final = linear_scan(la_c, in_c, None, reverse)
        y_x, _ = linear_scan(la_x, in_x, h_ctx_final, reverse)
        ys_c.append(y_c)
        ys_x.append(y_x)
    out_x = (jax.nn.gelu(branch_x, approximate=False) * (ys_x[0] + ys_x[1]).astype(h_lat.dtype)) @ w_out
    out_c = None
    if with_ctx_out:
        branch_c = h_ctx @ w_in[:, :RG_WIDTH]
        out_c = (jax.nn.gelu(branch_c, approximate=False) * (ys_c[0] + ys_c[1]).astype(h_ctx.dtype)) @ w_out
    return out_c, out_x


def peer_ffn(h, wq, k1, k2, u, v):
    b, t, d = h.shape
    n_tok = b * t
    ht = h.reshape(n_tok, d)
    q = (ht @ wq).reshape(n_tok, PEER_HEADS, PEER_QDIM)
    s1 = jnp.einsum('thd,kd->thk', q[..., :PEER_HALF], k1).astype(jnp.float32)
    s2 = jnp.einsum('thd,kd->thk', q[..., PEER_HALF:], k2).astype(jnp.float32)
    t1, i1 = lax.top_k(s1, PEER_TOPK)
    t2, i2 = lax.top_k(s2, PEER_TOPK)
    cand_s = (t1[..., :, None] + t2[..., None, :]).reshape(n_tok, PEER_HEADS, PEER_TOPK * PEER_TOPK)
    cand_i = (i1[..., :, None] * PEER_NK + i2[..., None, :]).reshape(n_tok, PEER_HEADS, PEER_TOPK * PEER_TOPK)
    top_s, pos = lax.top_k(cand_s, PEER_TOPK)
    idx = jnp.take_along_axis(cand_i, pos, axis=-1)
    g = jax.nn.softmax(top_s, axis=-1).astype(h.dtype)
    nb = n_tok // PEER_BLOCK

    def block(args):
        hb, ib, gb = args
        ue = jnp.take(u, ib, axis=0)
        ve = jnp.take(v, ib, axis=0)
        act = jax.nn.gelu(jnp.einsum('td,thkd->thk', hb, ue), approximate=False)
        return jnp.einsum('thk,thkd->td', gb * act, ve)

    out = lax.map(block, (ht.reshape(nb, PEER_BLOCK, d),
                          idx.reshape(nb, PEER_BLOCK, PEER_HEADS, PEER_TOPK),
                          g.reshape(nb, PEER_BLOCK, PEER_HEADS, PEER_TOPK)))
    return out.reshape(b, t, d)


def setup_inputs(seed: int = 0) -> dict:
    key = jax.random.key(seed)
    ks = jax.random.split(key, 32)
    f32 = jnp.float32
    D = D_MODEL
    NE = (DEPTH + 1) // 2
    NO = DEPTH // 2

    def nrm(k, shape, scale):
        return jax.random.normal(k, shape, f32) * scale

    a_c = jax.random.uniform(ks[24], (NO, 2, RG_WIDTH), f32, minval=0.9, maxval=0.999)
    sig = jnp.power(a_c, 1.0 / RG_C)
    od_lambda = jnp.log(sig) - jnp.log1p(-sig)
    return {
        'x': nrm(ks[0], (BATCH, SEQ, D), 1.0),
        'c': nrm(ks[1], (BATCH, D), 1.0),
        'ctx': nrm(ks[2], (BATCH, CTX_LEN, D), 1.0),
        'c_ctx': nrm(ks[3], (D,), 1.0),
        'mod_w': nrm(ks[4], (DEPTH, D, N_MOD * D), D ** -0.5),
        'mod_b': nrm(ks[5], (DEPTH, N_MOD * D), 0.02),
        'ln_g': 1.0 + nrm(ks[6], (DEPTH, 2, D), 0.02),
        'ln_b': nrm(ks[7], (DEPTH, 2, D), 0.02),
        'ev_w_in': nrm(ks[8], (NE, D, EV_IN), D ** -0.5),
        'ev_q_norm': 1.0 + nrm(ks[9], (NE, MLA_Q_RANK), 0.02),
        'ev_kv_norm': 1.0 + nrm(ks[10], (NE, MLA_KV_RANK), 0.02),
        'ev_w_uq': nrm(ks[11], (NE, MLA_Q_RANK, MLA_HEADS * (MLA_NOPE + MLA_ROPE)), MLA_Q_RANK ** -0.5),
        'ev_w_ukv': nrm(ks[12], (NE, MLA_KV_RANK, MLA_HEADS * (MLA_NOPE + MLA_V)), MLA_KV_RANK ** -0.5),
        'ev_gla_wa2': nrm(ks[13], (NE, 2, GLA_GATE_RANK, GLA_HEADS * GLA_DK), GLA_GATE_RANK ** -0.5),
        'ev_gla_ba': nrm(ks[14], (NE, 2, GLA_HEADS * GLA_DK), 0.1),
        'ev_gla_norm': 1.0 + nrm(ks[15], (NE, GLA_HEADS * GLA_DV), 0.02),
        'ev_w_out': nrm(ks[16], (NE, MIX_OUT, D), (MIX_OUT ** -0.5) * DEEPNORM_BETA),
        'od_w_in': nrm(ks[17], (NO, D, 2 * RG_WIDTH), D ** -0.5),
        'od_conv_w': nrm(ks[18], (NO, RG_CONV, RG_WIDTH), RG_CONV ** -0.5),
        'od_conv_b': nrm(ks[19], (NO, RG_WIDTH), 0.02),
        'od_gate_w': nrm(ks[20], (NO, 2, 2, RG_BLOCKS, RG_BW, RG_BW), RG_BW ** -0.5),
        'od_gate_b': nrm(ks[21], (NO, 2, 2, RG_WIDTH), 0.02),
        'od_lambda': od_lambda,
        'od_w_out': nrm(ks[22], (NO, RG_WIDTH, D), (RG_WIDTH ** -0.5) * DEEPNORM_BETA),
        'peer_wq': nrm(ks[23], (DEPTH, D, PEER_HEADS * PEER_QDIM), D ** -0.5),
        'peer_k1': nrm(ks[25], (DEPTH, PEER_NK, PEER_HALF), PEER_HALF ** -0.5),
        'peer_k2': nrm(ks[26], (DEPTH, PEER_NK, PEER_HALF), PEER_HALF ** -0.5),
        'peer_u': nrm(ks[27], (DEPTH, PEER_N, D), D ** -0.5),
        'peer_v': nrm(ks[28], (DEPTH, PEER_N, D), DEEPNORM_BETA),
    }


def reference(x, c, ctx, c_ctx, mod_w, mod_b, ln_g, ln_b,
              ev_w_in, ev_q_norm, ev_kv_norm, ev_w_uq, ev_w_ukv, ev_gla_wa2, ev_gla_ba, ev_gla_norm, ev_w_out,
              od_w_in, od_conv_w, od_conv_b, od_gate_w, od_gate_b, od_lambda, od_w_out,
              peer_wq, peer_k1, peer_k2, peer_u, peer_v):
    seq_len = x.shape[1]
    cos, sin = axial_rope(seq_len)
    silu_c = jax.nn.silu(c)
    silu_cc = jax.nn.silu(c_ctx)
    h_lat, h_ctx = x, ctx
    for layer in range(DEPTH):
        last = layer == DEPTH - 1
        j = layer // 2
        mx = jnp.split((silu_c @ mod_w[layer] + mod_b[layer])[:, None, :], N_MOD, axis=-1)
        mc = jnp.split((silu_cc @ mod_w[layer] + mod_b[layer])[None, None, :], N_MOD, axis=-1)
        in_x = modulate(h_lat, mx[0], mx[1])
        in_c = modulate(h_ctx, mc[0], mc[1])
        if layer % 2 == 0:
            mix_c, mix_x = mla_gla_mixer(in_c, in_x, cos, sin, ev_w_in[j], ev_q_norm[j], ev_kv_norm[j],
                                         ev_w_uq[j], ev_w_ukv[j], ev_gla_wa2[j], ev_gla_ba[j],
                                         ev_gla_norm[j], ev_w_out[j], not last)
        else:
            mix_c, mix_x = rglru_mixer(in_c, in_x, od_w_in[j], od_conv_w[j], od_conv_b[j], od_gate_w[j],
                                       od_gate_b[j], od_lambda[j], od_w_out[j], not last)
        h_lat = layer_norm(DEEPNORM_ALPHA * h_lat + mx[2] * mix_x, ln_g[layer, 0], ln_b[layer, 0])
        f_x = peer_ffn(modulate(h_lat, mx[3], mx[4]), peer_wq[layer], peer_k1[layer], peer_k2[layer],
                       peer_u[layer], peer_v[layer])
        h_lat = layer_norm(DEEPNORM_ALPHA * h_lat + mx[5] * f_x, ln_g[layer, 1], ln_b[layer, 1])
        if not last:
            h_ctx = layer_norm(DEEPNORM_ALPHA * h_ctx + mc[2] * mix_c, ln_g[layer, 0], ln_b[layer, 0])
            f_c = peer_ffn(modulate(h_ctx, mc[3], mc[4]), peer_wq[layer], peer_k1[layer], peer_k2[layer],
                           peer_u[layer], peer_v[layer])
            h_ctx = layer_norm(DEEPNORM_ALPHA * h_ctx + mc[5] * f_c, ln_g[layer, 1], ln_b[layer, 1])
    return h_lat
```

```python
import functools
import math

import numpy as np
import jax
import jax.numpy as jnp
from jax import lax
from jax.experimental import pallas as pl
from jax.experimental.pallas import tpu as pltpu

F32 = jnp.float32
BF16 = jnp.bfloat16
HI = lax.Precision.HIGHEST

N_MOD = 6
LN_EPS = 1e-6
GRID_W = 64
MLA_HEADS = 8
MLA_Q_RANK = 384
MLA_KV_RANK = 256
MLA_NOPE = 64
MLA_ROPE = 32
MLA_V = 64
MLA_SCALE = (MLA_NOPE + MLA_ROPE) ** -0.5
ROPE_BASE = 10000.0
GLA_HEADS = 4
GLA_DK = 64
GLA_DV = 128
GLA_GATE_RANK = 16
GLA_TAU = 16.0
RG_WIDTH = 1280
RG_BLOCKS = 10
RG_BW = RG_WIDTH // RG_BLOCKS
RG_CONV = 4
RG_CONV_LEFT = 2
RG_C = 8.0
PEER_HEADS = 8
PEER_NK = 128
PEER_TOPK = 16
PEER_QDIM = 256
PEER_HALF = PEER_QDIM // 2

LANES = 128
SUBLANES = 8
TOK_TILE = 256
GLA_CHUNK = 64
GLA_SUB = 16
PEER_TOK = 8
VMEM_LIMIT = 56 * 1024 * 1024

HEAD_PAD = LANES
PEER_PAIRS = PEER_HEADS * PEER_TOPK


def _cp(*sem):
    return pltpu.CompilerParams(dimension_semantics=sem, vmem_limit_bytes=VMEM_LIMIT)


def _gelu(x):
    return 0.5 * x * (1.0 + lax.erf(x * (2.0 ** -0.5)))


def _ln(y, g, b):
    mu = jnp.mean(y, axis=-1, keepdims=True)
    d = y - mu
    var = jnp.mean(d * d, axis=-1, keepdims=True)
    return d * lax.rsqrt(var + LN_EPS) * g + b


def _mod_index(nct, off):
    def index(b, i):
        return (2 * b + jnp.where(i + off >= nct, 1, 0), 0, 0)
    return index


def _mod_kernel(c_ref, w_ref, b_ref, o_ref):
    x = c_ref[...]
    s = x * jax.nn.sigmoid(x)
    o_ref[...] = jnp.dot(s, w_ref[...], precision=HI, preferred_element_type=F32) + b_ref[...]


def _mod_vectors(cc, w, b):
    rows, d = cc.shape
    n = w.shape[1] // d
    return pl.pallas_call(
        _mod_kernel,
        grid=(n,),
        in_specs=[pl.BlockSpec((rows, d), lambda j: (0, 0)),
                  pl.BlockSpec((d, d), lambda j: (0, j)),
                  pl.BlockSpec((1, d), lambda j: (0, j))],
        out_specs=pl.BlockSpec((rows, d), lambda j: (0, j)),
        out_shape=jax.ShapeDtypeStruct((rows, w.shape[1]), F32),
        compiler_params=_cp("arbitrary"),
        name="mod_vectors",
    )(cc, w, b)


def _rope_block(x, c, s1, s2):
    return x * c + pltpu.roll(x, LANES - MLA_ROPE // 2, axis=1) * s1 + pltpu.roll(x, MLA_ROPE // 2, axis=1) * s2


def _ev_proj_kernel(h_ref, mod_ref, win_ref, qn_ref, kvn_ref, wuq_ref, wukv_ref, wa2_ref, ba_ref,
                    rc_ref, rs1_ref, rs2_ref,
                    q_ref, k_ref, v_ref, gq_ref, gk_ref, gv_ref, gg_ref, laf_ref, lab_ref):
    x = h_ref[0]
    mod = mod_ref[0]
    xin = (x * (1.0 + mod[1:2]) + mod[0:1]).astype(BF16)
    p = jnp.dot(xin, win_ref[...], preferred_element_type=F32)
    o = 0
    cq = p[:, o:o + MLA_Q_RANK]; o += MLA_Q_RANK
    ckv = p[:, o:o + MLA_KV_RANK]; o += MLA_KV_RANK
    kpe = p[:, o:o + HEAD_PAD]; o += HEAD_PAD
    gdk = GLA_HEADS * GLA_DK
    gdv = GLA_HEADS * GLA_DV
    gq_ref[0] = p[:, o:o + gdk] * (GLA_DK ** -0.5); o += gdk
    gk_ref[0] = p[:, o:o + gdk]; o += gdk
    gv_ref[0] = p[:, o:o + gdv]; o += gdv
    gg_ref[0] = p[:, o:o + gdv]; o += gdv
    ga = p[:, o:o + LANES]

    c, s1, s2 = rc_ref[...], rs1_ref[...], rs2_ref[...]
    nq = cq * lax.rsqrt(jnp.mean(cq * cq, axis=-1, keepdims=True) + LN_EPS) * qn_ref[...]
    q = jnp.dot(nq.astype(BF16), wuq_ref[...], preferred_element_type=F32)
    nkv = ckv * lax.rsqrt(jnp.mean(ckv * ckv, axis=-1, keepdims=True) + LN_EPS) * kvn_ref[...]
    kv = jnp.dot(nkv.astype(BF16), wukv_ref[...], preferred_element_type=F32)
    kpe_r = _rope_block(kpe, c, s1, s2)
    for h in range(MLA_HEADS):
        sl = slice(h * HEAD_PAD, (h + 1) * HEAD_PAD)
        q_ref[0, :, sl] = _rope_block(q[:, sl], c, s1, s2).astype(BF16)
        k_ref[0, :, sl] = (kv[:, sl] + kpe_r).astype(BF16)
    v_ref[0] = kv[:, MLA_HEADS * HEAD_PAD:].astype(BF16)

    z = jnp.dot(ga.astype(BF16), wa2_ref[...], preferred_element_type=F32) + ba_ref[...]
    la = jax.nn.log_sigmoid(z) * (1.0 / GLA_TAU)
    laf_ref[0] = la[:, :gdk]
    lab_ref[0] = la[:, gdk:]


def _ev_proj(h, modt, nct, w):
    b, l, d = h.shape
    nt = l // TOK_TILE
    tok = lambda width: pl.BlockSpec((1, TOK_TILE, width), lambda bi, i: (bi, i, 0))
    full = lambda a: pl.BlockSpec(a.shape, lambda bi, i: (0,) * a.ndim)
    rope = pl.BlockSpec((TOK_TILE, LANES), lambda bi, i: (i, 0))
    gdk = GLA_HEADS * GLA_DK
    gdv = GLA_HEADS * GLA_DV
    widths = [(MLA_HEADS * HEAD_PAD, BF16), (MLA_HEADS * HEAD_PAD, BF16), (MLA_HEADS * MLA_V, BF16),
              (gdk, F32), (gdk, F32), (gdv, F32), (gdv, F32), (gdk, F32), (gdk, F32)]
    return pl.pallas_call(
        _ev_proj_kernel,
        grid=(b, nt),
        in_specs=[tok(d), pl.BlockSpec((1, N_MOD, d), _mod_index(nct, 0)),
                  full(w["win"]), full(w["qn"]), full(w["kvn"]), full(w["wuq"]), full(w["wukv"]),
                  full(w["wa2"]), full(w["ba"]), rope, rope, rope],
        out_specs=[tok(wd) for wd, _ in widths],
        out_shape=[jax.ShapeDtypeStruct((b, l, wd), dt) for wd, dt in widths],
        compiler_params=_cp("parallel", "parallel"),
        name="ev_proj",
    )(h, modt, w["win"], w["qn"], w["kvn"], w["wuq"], w["wukv"], w["wa2"], w["ba"],
      w["rope_c"], w["rope_s1"], w["rope_s2"])


def _attn_kernel(q_ref, k_ref, v_ref, o_ref, *, nct, ctx_len):
    def attend(nk):
        lane = lax.broadcasted_iota(jnp.int32, (TOK_TILE, LANES), 1)
        for hp in range(MLA_HEADS // 2):
            vp = v_ref[0, 0:nk, hp * LANES:(hp + 1) * LANES]
            outs = []
            for hh in range(2):
                h = 2 * hp + hh
                qh = q_ref[0, :, h * HEAD_PAD:(h + 1) * HEAD_PAD]
                kh = k_ref[0, 0:nk, h * HEAD_PAD:(h + 1) * HEAD_PAD]
                s = lax.dot_general(qh, kh, (((1,), (1,)), ((), ())), preferred_element_type=F32)
                m = jnp.max(s, axis=-1, keepdims=True)
                p = jnp.exp((s - m) * MLA_SCALE)
                den = jnp.sum(p, axis=-1, keepdims=True)
                o = jnp.dot(p.astype(BF16), vp, preferred_element_type=F32)
                outs.append(o / den)
            o_ref[0, :, hp * LANES:(hp + 1) * LANES] = jnp.where(lane < MLA_V, outs[0], outs[1]).astype(BF16)

    is_ctx = pl.program_id(1) < nct

    @pl.when(is_ctx)
    def _():
        attend(ctx_len)

    @pl.when(jnp.logical_not(is_ctx))
    def _():
        attend(k_ref.shape[1])


def _attention(q, k, v, nct, ctx_len):
    b, l, _ = q.shape
    nt = l // TOK_TILE
    return pl.pallas_call(
        functools.partial(_attn_kernel, nct=nct, ctx_len=ctx_len),
        grid=(b, nt),
        in_specs=[pl.BlockSpec((1, TOK_TILE, q.shape[2]), lambda bi, i: (bi, i, 0)),
                  pl.BlockSpec((1, l, k.shape[2]), lambda bi, i: (bi, 0, 0)),
                  pl.BlockSpec((1, l, v.shape[2]), lambda bi, i: (bi, 0, 0))],
        out_specs=pl.BlockSpec((1, TOK_TILE, v.shape[2]), lambda bi, i: (bi, i, 0)),
        out_shape=jax.ShapeDtypeStruct((b, l, v.shape[2]), BF16),
        compiler_params=_cp("parallel", "arbitrary"),
        name="mla_attention",
    )(q, k, v)


def _bwd_tile(s, nct, nt):
    return jnp.where(s < nct, nct - 1 - s, nt - 1 - (s - nct))


def _gla_chunk(q, k, v, la, st, rev):
    c = GLA_CHUNK
    nsub = c // GLA_SUB
    row = lax.broadcasted_iota(jnp.int32, (c, c), 0)
    col = lax.broadcasted_iota(jnp.int32, (c, c), 1)
    tri = (col >= row) if rev else (col <= row)
    cum = jnp.dot(tri.astype(F32), la, precision=HI, preferred_element_type=F32)
    rid = lax.broadcasted_iota(jnp.int32, (c, LANES), 0)
    lane = lax.broadcasted_iota(jnp.int32, (c, LANES), 1)
    head0 = lane < GLA_DK

    qparts, kparts, eref = [], [], jnp.zeros((c, LANES), F32)
    for jb in range(nsub):
        r = jb * GLA_SUB if rev else jb * GLA_SUB + GLA_SUB - 1
        ej = cum[r:r + 1, :]
        valid = (rid <= jb * GLA_SUB + GLA_SUB - 1) if rev else (rid >= jb * GLA_SUB)
        qparts.append(jnp.where(valid, q * jnp.exp(jnp.where(valid, cum - ej, 0.0)), 0.0))
        inblk = (rid >= jb * GLA_SUB) & (rid < (jb + 1) * GLA_SUB)
        eref = jnp.where(inblk, ej, eref)
    kt = k * jnp.exp(eref - cum)
    for jb in range(nsub):
        inblk = (rid >= jb * GLA_SUB) & (rid < (jb + 1) * GLA_SUB)
        kparts.append(jnp.where(inblk, kt, 0.0))
    qbig = jnp.concatenate(qparts, axis=1)
    kbig = jnp.concatenate(kparts, axis=1).astype(BF16)
    h0big = lax.broadcasted_iota(jnp.int32, (c, nsub * LANES), 1) % LANES < GLA_DK
    qst = jnp.concatenate([jnp.where(h0big, qbig, 0.0), jnp.where(h0big, 0.0, qbig)], axis=0).astype(BF16)
    sc = lax.dot_general(qst, kbig, (((1,), (1,)), ((), ())), preferred_element_type=F32)
    ri = lax.broadcasted_iota(jnp.int32, (2 * c, c), 0) % c
    ci = lax.broadcasted_iota(jnp.int32, (2 * c, c), 1)
    sc = jnp.where((ci >= ri) if rev else (ci <= ri), sc, 0.0)
    vb = v.astype(BF16)
    intra = jnp.dot(sc.astype(BF16), vb, preferred_element_type=F32)
    qs = (q * jnp.exp(cum)).astype(BF16)
    inter = lax.dot_general(qs, st.astype(BF16), (((1,), (1,)), ((), ())), preferred_element_type=F32)
    out = jnp.concatenate([intra[0:c, 0:GLA_DV] + inter[:, 0:GLA_DV],
                           intra[c:2 * c, GLA_DV:2 * GLA_DV] + inter[:, GLA_DV:2 * GLA_DV]], axis=1)
    last = cum[0:1, :] if rev else cum[c - 1:c, :]
    kd = (k * jnp.exp(last - cum)).astype(BF16)
    upd = lax.dot_general(vb, kd, (((0,), (0,)), ((), ())), preferred_element_type=F32)
    srow = lax.broadcasted_iota(jnp.int32, (2 * GLA_DV, LANES), 0)
    slane = lax.broadcasted_iota(jnp.int32, (2 * GLA_DV, LANES), 1)
    diag = (srow < GLA_DV) == (slane < GLA_DK)
    st_new = jnp.where(diag, st * jnp.exp(last) + upd, 0.0)
    return out, st_new


def _gla_kernel(qf_ref, kf_ref, vf_ref, laf_ref, qb_ref, kb_ref, vb_ref, lab_ref, of_ref, ob_ref, stf_ref, stb_ref):
    @pl.when(pl.program_id(1) == 0)
    def _():
        stf_ref[...] = jnp.zeros_like(stf_ref)
        stb_ref[...] = jnp.zeros_like(stb_ref)

    nchunk = TOK_TILE // GLA_CHUNK
    npair = GLA_HEADS // 2

    def body(ci, carry):
        rf = pl.multiple_of(ci * GLA_CHUNK, GLA_CHUNK)
        rb = pl.multiple_of((nchunk - 1 - ci) * GLA_CHUNK, GLA_CHUNK)
        for pr in range(npair):
            ks = slice(pr * LANES, (pr + 1) * LANES)
            vs = slice(pr * 2 * GLA_DV, (pr + 1) * 2 * GLA_DV)
            o, st = _gla_chunk(qf_ref[0, pl.ds(rf, GLA_CHUNK), ks], kf_ref[0, pl.ds(rf, GLA_CHUNK), ks],
                               vf_ref[0, pl.ds(rf, GLA_CHUNK), vs], laf_ref[0, pl.ds(rf, GLA_CHUNK), ks],
                               stf_ref[pr], False)
            of_ref[0, pl.ds(rf, GLA_CHUNK), vs] = o
            stf_ref[pr] = st
            o, st = _gla_chunk(qb_ref[0, pl.ds(rb, GLA_CHUNK), ks], kb_ref[0, pl.ds(rb, GLA_CHUNK), ks],
                               vb_ref[0, pl.ds(rb, GLA_CHUNK), vs], lab_ref[0, pl.ds(rb, GLA_CHUNK), ks],
                               stb_ref[pr], True)
            ob_ref[0, pl.ds(rb, GLA_CHUNK), vs] = o
            stb_ref[pr] = st
        return carry

    lax.fori_loop(0, nchunk, body, 0)


def _gla(gq, gk, gv, laf, lab, nct):
    b, l, dk = gq.shape
    dv = gv.shape[2]
    nt = l // TOK_TILE
    fwd = lambda width: pl.BlockSpec((1, TOK_TILE, width), lambda bi, s: (bi, s, 0))
    bwd = lambda width: pl.BlockSpec((1, TOK_TILE, width), lambda bi, s: (bi, _bwd_tile(s, nct, nt), 0))
    npair = GLA_HEADS // 2
    return pl.pallas_call(
        _gla_kernel,
        grid=(b, nt),
        in_specs=[fwd(dk), fwd(dk), fwd(dv), fwd(dk), bwd(dk), bwd(dk), bwd(dv), bwd(dk)],
        out_specs=[fwd(dv), bwd(dv)],
        out_shape=[jax.ShapeDtypeStruct((b, l, dv), F32)] * 2,
        scratch_shapes=[pltpu.VMEM((npair, 2 * GLA_DV, LANES), F32)] * 2,
        compiler_params=_cp("parallel", "arbitrary"),
        name="gla_scan",
    )(gq, gk, gv, laf, gq, gk, gv, lab)


def _ev_out_kernel(att_ref, of_ref, ob_ref, gg_ref, gn_ref, wout_ref, h_ref, mod_ref, lng_ref, lnb_ref, o_ref, *, alpha):
    o = of_ref[0] + ob_ref[0]
    g = gg_ref[0]
    parts = []
    for hd in range(GLA_HEADS):
        sl = slice(hd * GLA_DV, (hd + 1) * GLA_DV)
        oh = o[:, sl]
        y = oh * lax.rsqrt(jnp.mean(oh * oh, axis=-1, keepdims=True) + LN_EPS) * gn_ref[:, sl]
        gh = g[:, sl]
        parts.append((y * (gh * jax.nn.sigmoid(gh))).astype(BF16))
    gla = jnp.concatenate(parts, axis=1)
    na = att_ref.shape[2]
    mix = (jnp.dot(att_ref[0], wout_ref[0:na, :], preferred_element_type=F32)
           + jnp.dot(gla, wout_ref[na:, :], preferred_element_type=F32))
    mod = mod_ref[0]
    o_ref[0] = _ln(alpha * h_ref[0] + mod[2:3] * mix, lng_ref[...], lnb_ref[...])


def _ev_out(att, o_f, o_b, gg, gn, wout, h, modt, lng, lnb, nct, alpha):
    b, l, d = h.shape
    nt = l // TOK_TILE
    tok = lambda a: pl.BlockSpec((1, TOK_TILE, a.shape[2]), lambda bi, i: (bi, i, 0))
    full = lambda a: pl.BlockSpec(a.shape, lambda bi, i: (0,) * a.ndim)
    return pl.pallas_call(
        functools.partial(_ev_out_kernel, alpha=alpha),
        grid=(b, nt),
        in_specs=[tok(att), tok(o_f), tok(o_b), tok(gg), full(gn), full(wout), tok(h),
                  pl.BlockSpec((1, N_MOD, d), _mod_index(nct, 0)), full(lng), full(lnb)],
        out_specs=tok(h),
        out_shape=jax.ShapeDtypeStruct((b, l, d), F32),
        compiler_params=_cp("parallel", "parallel"),
        name="ev_out",
    )(att, o_f, o_b, gg, gn, wout, h, modt, lng, lnb)


def _od_proj_kernel(h_ref, mod_ref, win_ref, br_ref, rnn_ref):
    mod = mod_ref[0]
    xin = (h_ref[0] * (1.0 + mod[1:2]) + mod[0:1]).astype(BF16)
    p = jnp.dot(xin, win_ref[...], preferred_element_type=F32)
    br_ref[0] = p[:, :RG_WIDTH]
    rnn_ref[0] = p[:, RG_WIDTH:]


def _od_proj(h, modt, nct, win):
    b, l, d = h.shape
    nt = l // TOK_TILE
    tok = lambda width: pl.BlockSpec((1, TOK_TILE, width), lambda bi, i: (bi, i, 0))
    return pl.pallas_call(
        _od_proj_kernel,
        grid=(b, nt),
        in_specs=[tok(d), pl.BlockSpec((1, N_MOD, d), _mod_index(nct, 0)),
                  pl.BlockSpec(win.shape, lambda bi, i: (0, 0))],
        out_specs=[tok(RG_WIDTH), tok(RG_WIDTH)],
        out_shape=[jax.ShapeDtypeStruct((b, l, RG_WIDTH), F32)] * 2,
        compiler_params=_cp("parallel", "parallel"),
        name="od_proj",
    )(h, modt, win)


def _rg_prepare(z_ref, pv_ref, nx_ref, prev_zero, next_zero, cw_ref, cb_ref, gw_ref, gb_ref, lam_ref, a_ref, x_ref):
    t = TOK_TILE
    z = z_ref[0]
    pv = jnp.where(prev_zero, 0.0, pv_ref[0])
    nx = jnp.where(next_zero, 0.0, nx_ref[0])
    ext = jnp.concatenate([pv, z, nx], axis=0)
    base = SUBLANES - RG_CONV_LEFT
    y = cb_ref[...] + ext[base:base + t] * cw_ref[0:1, :]
    for tap in range(1, RG_CONV):
        y = y + ext[base + tap:base + tap + t] * cw_ref[tap:tap + 1, :]
    for n in range(RG_BLOCKS):
        sl = slice(n * RG_BW, (n + 1) * RG_BW)
        yb = y[:, sl]
        ri = jnp.dot(yb.astype(BF16), gw_ref[n], preferred_element_type=F32)
        r = jax.nn.sigmoid(ri[:, :RG_BW] + gb_ref[0:1, sl])
        i = jax.nn.sigmoid(ri[:, RG_BW:] + gb_ref[1:2, sl])
        log_a = (-RG_C) * jax.nn.softplus(-lam_ref[:, sl]) * r
        th = jnp.tanh(log_a)
        a_ref[:, sl] = jnp.exp(log_a)
        x_ref[:, sl] = jnp.sqrt(-2.0 * th / (1.0 - th)) * (i * yb)


def _rglru_kernel(zf_ref, pf_ref, nf_ref, zb_ref, pb_ref, nb_ref, cw_ref, cb_ref,
                  gwf_ref, gbf_ref, lamf_ref, gwb_ref, gbb_ref, lamb_ref,
                  yf_ref, yb_ref, af_ref, xf_ref, ab_ref, xb_ref, hf_ref, hb_ref, *, nct, nt):
    s = pl.program_id(1)

    @pl.when(s == 0)
    def _():
        hf_ref[...] = jnp.zeros_like(hf_ref)
        hb_ref[...] = jnp.zeros_like(hb_ref)

    def edges(j):
        return (j == 0) | (j == nct), (j == nct - 1) | (j == nt - 1)

    pz, nz = edges(s)
    _rg_prepare(zf_ref, pf_ref, nf_ref, pz, nz, cw_ref, cb_ref, gwf_ref, gbf_ref, lamf_ref, af_ref, xf_ref)
    pz, nz = edges(_bwd_tile(s, nct, nt))
    _rg_prepare(zb_ref, pb_ref, nb_ref, pz, nz, cw_ref, cb_ref, gwb_ref, gbb_ref, lamb_ref, ab_ref, xb_ref)

    def body(kk, carry):
        hf, hb = carry
        tf = kk
        tb = TOK_TILE - 1 - kk
        hf = af_ref[pl.ds(tf, 1), :] * hf + xf_ref[pl.ds(tf, 1), :]
        yf_ref[0, pl.ds(tf, 1), :] = hf
        hb = ab_ref[pl.ds(tb, 1), :] * hb + xb_ref[pl.ds(tb, 1), :]
        yb_ref[0, pl.ds(tb, 1), :] = hb
        return hf, hb

    hf, hb = lax.fori_loop(0, TOK_TILE, body, (hf_ref[...], hb_ref[...]), unroll=8)
    hf_ref[...] = hf
    hb_ref[...] = hb


def _rglru(rnn, nct, w):
    b, l, wd = rnn.shape
    nt = l // TOK_TILE
    per = TOK_TILE // SUBLANES
    nh = l // SUBLANES
    bt = lambda s: _bwd_tile(s, nct, nt)
    tile = lambda f: pl.BlockSpec((1, TOK_TILE, wd), lambda bi, s: (bi, f(s), 0))
    prev = lambda f: pl.BlockSpec((1, SUBLANES, wd), lambda bi, s: (bi, jnp.maximum(f(s) * per - 1, 0), 0))
    nxt = lambda f: pl.BlockSpec((1, SUBLANES, wd), lambda bi, s: (bi, jnp.minimum((f(s) + 1) * per, nh - 1), 0))
    full = lambda a: pl.BlockSpec(a.shape, lambda bi, s: (0,) * a.ndim)
    ident = lambda s: s
    vm = lambda rows: pltpu.VMEM((rows, wd), F32)
    return pl.pallas_call(
        functools.partial(_rglru_kernel, nct=nct, nt=nt),
        grid=(b, nt),
        in_specs=[tile(ident), prev(ident), nxt(ident), tile(bt), prev(bt), nxt(bt),
                  full(w["cw"]), full(w["cb"]), full(w["gwf"]), full(w["gbf"]), full(w["lamf"]),
                  full(w["gwb"]), full(w["gbb"]), full(w["lamb"])],
        out_specs=[tile(ident), tile(bt)],
        out_shape=[jax.ShapeDtypeStruct((b, l, wd), F32)] * 2,
        scratch_shapes=[vm(TOK_TILE), vm(TOK_TILE), vm(TOK_TILE), vm(TOK_TILE), vm(1), vm(1)],
        compiler_params=_cp("parallel", "arbitrary"),
        name="rglru_scan",
    )(rnn, rnn, rnn, rnn, rnn, rnn, w["cw"], w["cb"], w["gwf"], w["gbf"], w["lamf"], w["gwb"], w["gbb"], w["lamb"])


def _od_out_kernel(br_ref, yf_ref, yb_ref, wout_ref, h_ref, mod_ref, lng_ref, lnb_ref, o_ref, *, alpha):
    m = (_gelu(br_ref[0]) * (yf_ref[0] + yb_ref[0])).astype(BF16)
    mix = jnp.dot(m, wout_ref[...], preferred_element_type=F32)
    mod = mod_ref[0]
    o_ref[0] = _ln(alpha * h_ref[0] + mod[2:3] * mix, lng_ref[...], lnb_ref[...])


def _od_out(br, yf, yb, wout, h, modt, lng, lnb, nct, off, alpha):
    b, l, d = h.shape
    nt = l // TOK_TILE - off
    tok = lambda a: pl.BlockSpec((1, TOK_TILE, a.shape[2]), lambda bi, i: (bi, i + off, 0))
    full = lambda a: pl.BlockSpec(a.shape, lambda bi, i: (0,) * a.ndim)
    return pl.pallas_call(
        functools.partial(_od_out_kernel, alpha=alpha),
        grid=(b, nt),
        in_specs=[tok(br), tok(yf), tok(yb), full(wout), tok(h),
                  pl.BlockSpec((1, N_MOD, d), _mod_index(nct, off)), full(lng), full(lnb)],
        out_specs=pl.BlockSpec((1, TOK_TILE, d), lambda bi, i: (bi, i, 0)),
        out_shape=jax.ShapeDtypeStruct((b, nt * TOK_TILE, d), F32),
        compiler_params=_cp("parallel", "parallel"),
        name="od_out",
    )(br, yf, yb, wout, h, modt, lng, lnb)


def _top16_rows(s, vals_ref, idx_ref, base):
    nk = s.shape[0]
    rid = lax.broadcasted_iota(jnp.int32, s.shape, 0).astype(F32)
    for r in range(PEER_TOPK):
        m = jnp.max(s, axis=0, keepdims=True)
        pos = jnp.min(jnp.where(s == m, rid, float(nk)), axis=0, keepdims=True)
        vals_ref[base + r:base + r + 1, :] = m
        idx_ref[base + r:base + r + 1, :] = pos
        s = jnp.where(rid == pos, -jnp.inf, s)


def _cand_blocks(t1, t2, combine):
    blocks = [combine(t1[0:1], t2[0:16])]
    for a in range(1, 8):
        blocks.append(combine(t1[a:a + 1], t2[0:8]))
    blocks.append(combine(t1[8:16], t2[0:1]))
    return jnp.concatenate(blocks, axis=0)


def _cand_flat_index(ntok):
    a16 = lax.broadcasted_iota(jnp.int32, (16, ntok), 0)
    a8 = lax.broadcasted_iota(jnp.int32, (8, ntok), 0)
    blocks = [a16]
    for a in range(1, 8):
        blocks.append(a * PEER_TOPK + a8)
    blocks.append((a8 + 8) * PEER_TOPK)
    return jnp.concatenate(blocks, axis=0).astype(F32)


def _peer_route_kernel(h_ref, mod_ref, wq_ref, k1_ref, k2_ref, idx_ref, gate_ref, t1_ref, i1_ref, t2_ref, i2_ref, ts_ref, ti_ref):
    mod = mod_ref[0]
    hm = h_ref[0] * (1.0 + mod[4:5]) + mod[3:4]
    q = jnp.dot(hm, wq_ref[...], precision=HI, preferred_element_type=F32)
    ntok = hm.shape[0]
    flat = _cand_flat_index(ntok)
    nc = flat.shape[0]
    nn = (((1,), (1,)), ((), ()))
    for h in range(PEER_HEADS):
        q1 = q[:, h * PEER_QDIM:h * PEER_QDIM + PEER_HALF]
        q2 = q[:, h * PEER_QDIM + PEER_HALF:(h + 1) * PEER_QDIM]
        s1 = lax.dot_general(k1_ref[...], q1, nn, precision=HI, preferred_element_type=F32)
        s2 = lax.dot_general(k2_ref[...], q2, nn, precision=HI, preferred_element_type=F32)
        _top16_rows(s1, t1_ref, i1_ref, 0)
        _top16_rows(s2, t2_ref, i2_ref, 0)
        cs = _cand_blocks(t1_ref[...], t2_ref[...], lambda x, y: x + y)
        ce = _cand_blocks(i1_ref[...], i2_ref[...], lambda x, y: x * PEER_NK + y)
        for r in range(PEER_TOPK):
            m = jnp.max(cs, axis=0, keepdims=True)
            pos = jnp.min(jnp.where(cs == m, flat, float(PEER_TOPK * PEER_TOPK)), axis=0, keepdims=True)
            hit = flat == pos
            row = h * PEER_TOPK + r
            ts_ref[row:row + 1, :] = m
            ti_ref[row:row + 1, :] = jnp.max(jnp.where(hit, ce, -1.0), axis=0, keepdims=True)
            cs = jnp.where(hit, -jnp.inf, cs)
        blk = slice(h * PEER_TOPK, (h + 1) * PEER_TOPK)
        top = ts_ref[blk, :]
        e = jnp.exp(top - top[0:1, :])
        ts_ref[blk, :] = e / jnp.sum(e, axis=0, keepdims=True)
    idx_ref[0] = ti_ref[...].T.astype(jnp.int32)
    gate_ref[0] = ts_ref[...].T


def _peer_route(h, modt, nct, off, wq, k1, k2):
    b, l, d = h.shape
    nt = l // TOK_TILE
    tok = lambda width: pl.BlockSpec((1, TOK_TILE, width), lambda bi, i: (bi, i, 0))
    full = lambda a: pl.BlockSpec(a.shape, lambda bi, i: (0,) * a.ndim)
    vmf = lambda rows: pltpu.VMEM((rows, TOK_TILE), F32)
    return pl.pallas_call(
        _peer_route_kernel,
        grid=(b, nt),
        in_specs=[tok(d), pl.BlockSpec((1, N_MOD, d), _mod_index(nct, off)), full(wq), full(k1), full(k2)],
        out_specs=[tok(PEER_PAIRS), tok(PEER_PAIRS)],
        out_shape=[jax.ShapeDtypeStruct((b, l, PEER_PAIRS), jnp.int32), jax.ShapeDtypeStruct((b, l, PEER_PAIRS), F32)],
        scratch_shapes=[vmf(PEER_TOPK), vmf(PEER_TOPK), vmf(PEER_TOPK), vmf(PEER_TOPK), vmf(PEER_PAIRS), vmf(PEER_PAIRS)],
        compiler_params=_cp("parallel", "parallel"),
        name="peer_route",
    )(h, modt, wq, k1, k2)


def _peer_expert_kernel(idx_ref, idxn_ref, h_ref, mod_ref, gate_ref, lng_ref, lnb_ref, tab_ref, o_ref, buf, sem,
                        *, alpha, nsteps):
    g = pl.program_id(0)
    slot = g % 2
    npair = PEER_TOK * PEER_PAIRS

    def issue(iref, dst_slot):
        def body(p, carry):
            e = iref[0, 0, p]
            pltpu.make_async_copy(tab_ref.at[e], buf.at[dst_slot, p], sem.at[dst_slot]).start()
            return carry
        lax.fori_loop(0, npair, body, 0, unroll=8)

    @pl.when(g == 0)
    def _():
        issue(idx_ref, 0)

    @pl.when(g + 1 < nsteps)
    def _():
        issue(idxn_ref, 1 - slot)

    pltpu.make_async_copy(buf.at[slot], buf.at[slot], sem.at[slot]).wait()

    mod = mod_ref[0]
    h = h_ref[...]
    hm = h * (1.0 + mod[4:5]) + mod[3:4]
    gates = gate_ref[...]
    nchunk = h.shape[1] // LANES
    ones = jnp.ones((LANES, LANES), F32)
    eye = (lax.broadcasted_iota(jnp.int32, (LANES, LANES), 0) == lax.broadcasted_iota(jnp.int32, (LANES, LANES), 1))
    rows = []
    for t in range(PEER_TOK):
        part = jnp.zeros((PEER_PAIRS, LANES), F32)
        for c in range(nchunk):
            w = buf[slot, pl.ds(t * PEER_PAIRS, PEER_PAIRS), c, :]
            uf = lax.bitcast_convert_type(w & jnp.uint32(0xFFFF0000), F32)
            part = part + uf * hm[t:t + 1, c * LANES:(c + 1) * LANES]
        act = jnp.dot(part, ones, precision=HI, preferred_element_type=F32)
        gdiag = jnp.where(eye, gates[t:t + 1, :], 0.0)
        coef = _gelu(act) * jnp.dot(gdiag, ones, precision=HI, preferred_element_type=F32)
        outs = []
        for c in range(nchunk):
            w = buf[slot, pl.ds(t * PEER_PAIRS, PEER_PAIRS), c, :]
            vf = lax.bitcast_convert_type(w << 16, F32)
            outs.append(jnp.sum(coef * vf, axis=0, keepdims=True))
        rows.append(jnp.concatenate(outs, axis=1))
    f = jnp.concatenate(rows, axis=0)
    o_ref[...] = _ln(alpha * h + mod[5:6] * f, lng_ref[...], lnb_ref[...])


def _peer_expert(h, modt, nct, off, idx, gate, table, lng, lnb, alpha):
    b, l, d = h.shape
    n = b * l
    nsteps = n // PEER_TOK
    per_b = l // PEER_TOK
    tiles_per = TOK_TILE // PEER_TOK
    h2 = h.reshape(n, d)
    idx3 = idx.reshape(nsteps, 1, PEER_TOK * PEER_PAIRS)
    gate2 = gate.reshape(n, PEER_PAIRS)
    smem = lambda f: pl.BlockSpec((1, 1, PEER_TOK * PEER_PAIRS), f, memory_space=pltpu.SMEM)
    mod_idx = _mod_index(nct, off)
    out = pl.pallas_call(
        functools.partial(_peer_expert_kernel, alpha=alpha, nsteps=nsteps),
        grid=(nsteps,),
        in_specs=[smem(lambda g: (g, 0, 0)), smem(lambda g: (jnp.minimum(g + 1, nsteps - 1), 0, 0)),
                  pl.BlockSpec((PEER_TOK, d), lambda g: (g, 0)),
                  pl.BlockSpec((1, N_MOD, d), lambda g: mod_idx(g // per_b, (g % per_b) // tiles_per)),
                  pl.BlockSpec((PEER_TOK, PEER_PAIRS), lambda g: (g, 0)),
                  pl.BlockSpec(lng.shape, lambda g: (0, 0)), pl.BlockSpec(lnb.shape, lambda g: (0, 0)),
                  pl.BlockSpec(memory_space=pl.ANY)],
        out_specs=pl.BlockSpec((PEER_TOK, d), lambda g: (g, 0)),
        out_shape=jax.ShapeDtypeStruct((n, d), F32),
        scratch_shapes=[pltpu.VMEM((2, PEER_TOK * PEER_PAIRS, d // LANES, LANES), jnp.uint32),
                        pltpu.SemaphoreType.DMA((2,))],
        compiler_params=_cp("arbitrary"),
        name="peer_expert",
    )(idx3, idx3, h2, modt, gate2, lng, lnb, table)
    return out.reshape(b, l, d)


def _peer_table(u, v):
    n, d = u.shape
    ub = lax.bitcast_convert_type(u.astype(BF16), jnp.uint16).astype(jnp.uint32)
    vb = lax.bitcast_convert_type(v.astype(BF16), jnp.uint16).astype(jnp.uint32)
    return ((ub << 16) | vb).reshape(n, d // LANES, LANES)


def _rope_tables(seq_len, ctx_len):
    rows = seq_len // GRID_W
    row = np.repeat(np.arange(rows, dtype=np.float32), GRID_W)
    col = np.tile(np.arange(GRID_W, dtype=np.float32), rows)
    n_freq = MLA_ROPE // 4
    inv = jnp.power(ROPE_BASE, -jnp.arange(n_freq, dtype=F32) / n_freq)
    ang = jnp.concatenate([row[:, None] * inv, col[:, None] * inv], axis=-1)
    ang = jnp.concatenate([jnp.zeros((ctx_len, MLA_ROPE // 2), F32), ang], axis=0)
    cos, sin = jnp.cos(ang), jnp.sin(ang)
    l = seq_len + ctx_len
    half = MLA_ROPE // 2
    c = jnp.ones((l, LANES), F32).at[:, MLA_NOPE:MLA_NOPE + half].set(cos).at[:, MLA_NOPE + half:MLA_NOPE + 2 * half].set(cos)
    s1 = jnp.zeros((l, LANES), F32).at[:, MLA_NOPE:MLA_NOPE + half].set(-sin)
    s2 = jnp.zeros((l, LANES), F32).at[:, MLA_NOPE + half:MLA_NOPE + 2 * half].set(sin)
    return c, s1, s2


def _ev_weights(w_in, q_norm, kv_norm, w_uq, w_ukv, wa2, ba, seq_len, ctx_len):
    d = w_in.shape[0]
    widths = (MLA_Q_RANK, MLA_KV_RANK, MLA_ROPE, GLA_HEADS * GLA_DK, GLA_HEADS * GLA_DK,
              GLA_HEADS * GLA_DV, GLA_HEADS * GLA_DV, GLA_GATE_RANK, GLA_GATE_RANK)
    offs = np.cumsum((0,) + widths)
    part = lambda i: w_in[:, offs[i]:offs[i + 1]]
    kpe = jnp.zeros((d, HEAD_PAD), F32).at[:, MLA_NOPE:MLA_NOPE + MLA_ROPE].set(part(2))
    gate = jnp.zeros((d, LANES), F32).at[:, :GLA_GATE_RANK].set(part(7)).at[:, GLA_GATE_RANK:2 * GLA_GATE_RANK].set(part(8))
    win = jnp.concatenate([part(0), part(1), kpe, part(3), part(4), part(5), part(6), gate], axis=1).astype(BF16)
    hq = MLA_NOPE + MLA_ROPE
    wuq = jnp.pad(w_uq.reshape(MLA_Q_RANK, MLA_HEADS, hq), ((0, 0), (0, 0), (0, HEAD_PAD - hq)))
    wuq = wuq.reshape(MLA_Q_RANK, MLA_HEADS * HEAD_PAD).astype(BF16)
    ukv = w_ukv.reshape(MLA_KV_RANK, MLA_HEADS, MLA_NOPE + MLA_V)
    wk = jnp.pad(ukv[:, :, :MLA_NOPE], ((0, 0), (0, 0), (0, HEAD_PAD - MLA_NOPE))).reshape(MLA_KV_RANK, MLA_HEADS * HEAD_PAD)
    wv = ukv[:, :, MLA_NOPE:].reshape(MLA_KV_RANK, MLA_HEADS * MLA_V)
    wukv = jnp.concatenate([wk, wv], axis=1).astype(BF16)
    gdk = GLA_HEADS * GLA_DK
    wa2p = jnp.zeros((LANES, 2 * gdk), F32).at[:GLA_GATE_RANK, :gdk].set(wa2[0])
    wa2p = wa2p.at[GLA_GATE_RANK:2 * GLA_GATE_RANK, gdk:].set(wa2[1]).astype(BF16)
    c, s1, s2 = _rope_tables(seq_len, ctx_len)
    return dict(win=win, qn=q_norm[None, :], kvn=kv_norm[None, :], wuq=wuq, wukv=wukv, wa2=wa2p,
                ba=ba.reshape(1, 2 * gdk), rope_c=c, rope_s1=s1, rope_s2=s2)


def _od_weights(conv_w, conv_b, gate_w, gate_b, lam):
    gw = lambda dd: jnp.concatenate([gate_w[dd, 0], gate_w[dd, 1]], axis=-1).astype(BF16)
    return dict(cw=conv_w, cb=conv_b[None, :], gwf=gw(0), gbf=gate_b[0], lamf=lam[0][None, :],
                gwb=gw(1), gbb=gate_b[1], lamb=lam[1][None, :])


def kernel(x, c, ctx, c_ctx, mod_w, mod_b, ln_g, ln_b,
           ev_w_in, ev_q_norm, ev_kv_norm, ev_w_uq, ev_w_ukv, ev_gla_wa2, ev_gla_ba, ev_gla_norm, ev_w_out,
           od_w_in, od_conv_w, od_conv_b, od_gate_w, od_gate_b, od_lambda, od_w_out,
           peer_wq, peer_k1, peer_k2, peer_u, peer_v):
    b, seq_len, d = x.shape
    ctx_len = ctx.shape[1]
    depth = mod_w.shape[0]
    assert ctx_len % TOK_TILE == 0 and seq_len % TOK_TILE == 0 and seq_len % GRID_W == 0
    alpha = (2 * depth) ** 0.25

    rows = -(-(b + 1) // SUBLANES) * SUBLANES
    cc = jnp.zeros((rows, d), F32).at[:b].set(c).at[b].set(c_ctx)

    h = jnp.concatenate([ctx, x], axis=1)
    nct = ctx_len // TOK_TILE
    off = 0
    for layer in range(depth):
        last = layer == depth - 1
        j = layer // 2
        mods = _mod_vectors(cc, mod_w[layer], mod_b[layer][None, :]).reshape(rows, N_MOD, d)
        modt = jnp.stack([jnp.broadcast_to(mods[b], (b, N_MOD, d)), mods[:b]], axis=1).reshape(2 * b, N_MOD, d)
        lng = ln_g[layer][:, None, :]
        lnb = ln_b[layer][:, None, :]
        assert off == 0
        if layer % 2 == 0:
            w = _ev_weights(ev_w_in[j], ev_q_norm[j], ev_kv_norm[j], ev_w_uq[j], ev_w_ukv[j],
                            ev_gla_wa2[j], ev_gla_ba[j], seq_len, ctx_len)
            q, k, v, gq, gk, gv, gg, laf, lab = _ev_proj(h, modt, nct, w)
            att = _attention(q, k, v, nct, ctx_len)
            o_f, o_b = _gla(gq, gk, gv, laf, lab, nct)
            assert not last, "the final layer drops the context rows only after an odd layer"
            h = _ev_out(att, o_f, o_b, gg, ev_gla_norm[j][None, :], ev_w_out[j].astype(BF16), h, modt,
                        lng[0], lnb[0], nct, alpha)
        else:
            w = _od_weights(od_conv_w[j], od_conv_b[j], od_gate_w[j], od_gate_b[j], od_lambda[j])
            br, rnn = _od_proj(h, modt, nct, od_w_in[j].astype(BF16))
            yf, yb = _rglru(rnn, nct, w)
            off = nct if last else 0
            h = _od_out(br, yf, yb, od_w_out[j].astype(BF16), h, modt, lng[0], lnb[0], nct, off, alpha)
        idx, gate = _peer_route(h, modt, nct, off, peer_wq[layer], peer_k1[layer], peer_k2[layer])
        table = _peer_table(peer_u[layer], peer_v[layer])
        h = _peer_expert(h, modt, nct, off, idx, gate, table, lng[1], lnb[1], alpha)
    return h if off else h[:, ctx_len:]
```

```python
import functools
import math

import numpy as np
import jax
import jax.numpy as jnp
from jax import lax
from jax.experimental import pallas as pl
from jax.experimental.pallas import tpu as pltpu

F32 = jnp.float32
BF16 = jnp.bfloat16
HI = lax.Precision.HIGHEST

N_MOD = 6
LN_EPS = 1e-6
GRID_W = 64
MLA_HEADS = 8
MLA_Q_RANK = 384
MLA_KV_RANK = 256
MLA_NOPE = 64
MLA_ROPE = 32
MLA_V = 64
MLA_SCALE = (MLA_NOPE + MLA_ROPE) ** -0.5
ROPE_BASE = 10000.0
GLA_HEADS = 4
GLA_DK = 64
GLA_DV = 128
GLA_GATE_RANK = 16
GLA_TAU = 16.0
RG_WIDTH = 1280
RG_BLOCKS = 10
RG_BW = RG_WIDTH // RG_BLOCKS
RG_CONV = 4
RG_CONV_LEFT = 2
RG_C = 8.0
PEER_HEADS = 8
PEER_NK = 128
PEER_TOPK = 16
PEER_QDIM = 256
PEER_HALF = PEER_QDIM // 2

LANES = 128
SUBLANES = 8
TOK_TILE = 256
GLA_CHUNK = 64
GLA_SUB = 16
PEER_TOK = 8
VMEM_LIMIT = 56 * 1024 * 1024

HEAD_PAD = LANES
PEER_PAIRS = PEER_HEADS * PEER_TOPK


def _cp(*sem):
    return pltpu.CompilerParams(dimension_semantics=sem, vmem_limit_bytes=VMEM_LIMIT)


def _gelu(x):
    return 0.5 * x * (1.0 + lax.erf(x * (2.0 ** -0.5)))


def _ln(y, g, b):
    mu = jnp.mean(y, axis=-1, keepdims=True)
    d = y - mu
    var = jnp.mean(d * d, axis=-1, keepdims=True)
    return d * lax.rsqrt(var + LN_EPS) * g + b


def _mod_index(nct, off):
    def index(b, i):
        return (2 * b + jnp.where(i + off >= nct, 1, 0), 0, 0)
    return index


def _mod_kernel(c_ref, w_ref, b_ref, o_ref):
    x = c_ref[...]
    s = x * jax.nn.sigmoid(x)
    o_ref[...] = jnp.dot(s, w_ref[...], precision=HI, preferred_element_type=F32) + b_ref[...]


def _mod_vectors(cc, w, b):
    rows, d = cc.shape
    n = w.shape[1] // d
    return pl.pallas_call(
        _mod_kernel,
        grid=(n,),
        in_specs=[pl.BlockSpec((rows, d), lambda j: (0, 0)),
                  pl.BlockSpec((d, d), lambda j: (0, j)),
                  pl.BlockSpec((1, d), lambda j: (0, j))],
        out_specs=pl.BlockSpec((rows, d), lambda j: (0, j)),
        out_shape=jax.ShapeDtypeStruct((rows, w.shape[1]), F32),
        compiler_params=_cp("arbitrary"),
        name="mod_vectors",
    )(cc, w, b)


def _rope_block(x, c, s1, s2):
    return x * c + pltpu.roll(x, LANES - MLA_ROPE // 2, axis=1) * s1 + pltpu.roll(x, MLA_ROPE // 2, axis=1) * s2


def _ev_proj_kernel(h_ref, mod_ref, win_ref, qn_ref, kvn_ref, wuq_ref, wukv_ref, wa2_ref, ba_ref,
                    rc_ref, rs1_ref, rs2_ref,
                    q_ref, k_ref, v_ref, gq_ref, gk_ref, gv_ref, gg_ref, laf_ref, lab_ref):
    x = h_ref[0]
    mod = mod_ref[0]
    xin = (x * (1.0 + mod[1:2]) + mod[0:1]).astype(BF16)
    p = jnp.dot(xin, win_ref[...], preferred_element_type=F32)
    o = 0
    cq = p[:, o:o + MLA_Q_RANK]; o += MLA_Q_RANK
    ckv = p[:, o:o + MLA_KV_RANK]; o += MLA_KV_RANK
    kpe = p[:, o:o + HEAD_PAD]; o += HEAD_PAD
    gdk = GLA_HEADS * GLA_DK
    gdv = GLA_HEADS * GLA_DV
    gq_ref[0] = p[:, o:o + gdk] * (GLA_DK ** -0.5); o += gdk
    gk_ref[0] = p[:, o:o + gdk]; o += gdk
    gv_ref[0] = p[:, o:o + gdv]; o += gdv
    gg_ref[0] = p[:, o:o + gdv]; o += gdv
    ga = p[:, o:o + LANES]

    c, s1, s2 = rc_ref[...], rs1_ref[...], rs2_ref[...]
    nq = cq * lax.rsqrt(jnp.mean(cq * cq, axis=-1, keepdims=True) + LN_EPS) * qn_ref[...]
    q = jnp.dot(nq.astype(BF16), wuq_ref[...], preferred_element_type=F32)
    nkv = ckv * lax.rsqrt(jnp.mean(ckv * ckv, axis=-1, keepdims=True) + LN_EPS) * kvn_ref[...]
    kv = jnp.dot(nkv.astype(BF16), wukv_ref[...], preferred_element_type=F32)
    kpe_r = _rope_block(kpe, c, s1, s2)
    for h in range(MLA_HEADS):
        sl = slice(h * HEAD_PAD, (h + 1) * HEAD_PAD)
        q_ref[0, :, sl] = _rope_block(q[:, sl], c, s1, s2).astype(BF16)
        k_ref[0, :, sl] = (kv[:, sl] + kpe_r).astype(BF16)
    v_ref[0] = kv[:, MLA_HEADS * HEAD_PAD:].astype(BF16)

    z = jnp.dot(ga.astype(BF16), wa2_ref[...], preferred_element_type=F32) + ba_ref[...]
    la = jax.nn.log_sigmoid(z) * (1.0 / GLA_TAU)
    laf_ref[0] = la[:, :gdk]
    lab_ref[0] = la[:, gdk:]


def _ev_proj(h, modt, nct, w):
    b, l, d = h.shape
    nt = l // TOK_TILE
    tok = lambda width: pl.BlockSpec((1, TOK_TILE, width), lambda bi, i: (bi, i, 0))
    full = lambda a: pl.BlockSpec(a.shape, lambda bi, i: (0,) * a.ndim)
    rope = pl.BlockSpec((TOK_TILE, LANES), lambda bi, i: (i, 0))
    gdk = GLA_HEADS * GLA_DK
    gdv = GLA_HEADS * GLA_DV
    widths = [(MLA_HEADS * HEAD_PAD, BF16), (MLA_HEADS * HEAD_PAD, BF16), (MLA_HEADS * MLA_V, BF16),
              (gdk, F32), (gdk, F32), (gdv, F32), (gdv, F32), (gdk, F32), (gdk, F32)]
    return pl.pallas_call(
        _ev_proj_kernel,
        grid=(b, nt),
        in_specs=[tok(d), pl.BlockSpec((1, N_MOD, d), _mod_index(nct, 0)),
                  full(w["win"]), full(w["qn"]), full(w["kvn"]), full(w["wuq"]), full(w["wukv"]),
                  full(w["wa2"]), full(w["ba"]), rope, rope, rope],
        out_specs=[tok(wd) for wd, _ in widths],
        out_shape=[jax.ShapeDtypeStruct((b, l, wd), dt) for wd, dt in widths],
        compiler_params=_cp("parallel", "parallel"),
        name="ev_proj",
    )(h, modt, w["win"], w["qn"], w["kvn"], w["wuq"], w["wukv"], w["wa2"], w["ba"],
      w["rope_c"], w["rope_s1"], w["rope_s2"])


def _attn_kernel(q_ref, k_ref, v_ref, o_ref, *, nct, ctx_len):
    def attend(nk):
        lane = lax.broadcasted_iota(jnp.int32, (TOK_TILE, LANES), 1)
        for hp in range(MLA_HEADS // 2):
            vp = v_ref[0, 0:nk, hp * LANES:(hp + 1) * LANES]
            outs = []
            for hh in range(2):
                h = 2 * hp + hh
                qh = q_ref[0, :, h * HEAD_PAD:(h + 1) * HEAD_PAD]
                kh = k_ref[0, 0:nk, h * HEAD_PAD:(h + 1) * HEAD_PAD]
                s = lax.dot_general(qh, kh, (((1,), (1,)), ((), ())), preferred_element_type=F32)
                m = jnp.max(s, axis=-1, keepdims=True)
                p = jnp.exp((s - m) * MLA_SCALE)
                den = jnp.sum(p, axis=-1, keepdims=True)
                o = jnp.dot(p.astype(BF16), vp, preferred_element_type=F32)
                outs.append(o / den)
            o_ref[0, :, hp * LANES:(hp + 1) * LANES] = jnp.where(lane < MLA_V, outs[0], outs[1]).astype(BF16)

    is_ctx = pl.program_id(1) < nct

    @pl.when(is_ctx)
    def _():
        attend(ctx_len)

    @pl.when(jnp.logical_not(is_ctx))
    def _():
        attend(k_ref.shape[1])


def _attention(q, k, v, nct, ctx_len):
    b, l, _ = q.shape
    nt = l // TOK_TILE
    return pl.pallas_call(
        functools.partial(_attn_kernel, nct=nct, ctx_len=ctx_len),
        grid=(b, nt),
        in_specs=[pl.BlockSpec((1, TOK_TILE, q.shape[2]), lambda bi, i: (bi, i, 0)),
                  pl.BlockSpec((1, l, k.shape[2]), lambda bi, i: (bi, 0, 0)),
                  pl.BlockSpec((1, l, v.shape[2]), lambda bi, i: (bi, 0, 0))],
        out_specs=pl.BlockSpec((1, TOK_TILE, v.shape[2]), lambda bi, i: (bi, i, 0)),
        out_shape=jax.ShapeDtypeStruct((b, l, v.shape[2]), BF16),
        compiler_params=_cp("parallel", "arbitrary"),
        name="mla_attention",
    )(q, k, v)


def _bwd_tile(s, nct, nt):
    return jnp.where(s < nct, nct - 1 - s, nt - 1 - (s - nct))


def _gla_chunk(q, k, v, la, st, rev):
    c = GLA_CHUNK
    nsub = c // GLA_SUB
    row = lax.broadcasted_iota(jnp.int32, (c, c), 0)
    col = lax.broadcasted_iota(jnp.int32, (c, c), 1)
    tri = (col >= row) if rev else (col <= row)
    cum = jnp.dot(tri.astype(F32), la, precision=HI, preferred_element_type=F32)
    rid = lax.broadcasted_iota(jnp.int32, (c, LANES), 0)
    lane = lax.broadcasted_iota(jnp.int32, (c, LANES), 1)
    head0 = lane < GLA_DK

    qparts, kparts, eref = [], [], jnp.zeros((c, LANES), F32)
    for jb in range(nsub):
        r = jb * GLA_SUB if rev else jb * GLA_SUB + GLA_SUB - 1
        ej = cum[r:r + 1, :]
        valid = (rid <= jb * GLA_SUB + GLA_SUB - 1) if rev else (rid >= jb * GLA_SUB)
        qparts.append(jnp.where(valid, q * jnp.exp(jnp.where(valid, cum - ej, 0.0)), 0.0))
        inblk = (rid >= jb * GLA_SUB) & (rid < (jb + 1) * GLA_SUB)
        eref = jnp.where(inblk, ej, eref)
    kt = k * jnp.exp(eref - cum)
    for jb in range(nsub):
        inblk = (rid >= jb * GLA_SUB) & (rid < (jb + 1) * GLA_SUB)
        kparts.append(jnp.where(inblk, kt, 0.0))
    qbig = jnp.concatenate(qparts, axis=1)
    kbig = jnp.concatenate(kparts, axis=1).astype(BF16)
    h0big = lax.broadcasted_iota(jnp.int32, (c, nsub * LANES), 1) % LANES < GLA_DK
    qst = jnp.concatenate([jnp.where(h0big, qbig, 0.0), jnp.where(h0big, 0.0, qbig)], axis=0).astype(BF16)
    sc = lax.dot_general(qst, kbig, (((1,), (1,)), ((), ())), preferred_element_type=F32)
    ri = lax.broadcasted_iota(jnp.int32, (2 * c, c), 0) % c
    ci = lax.broadcasted_iota(jnp.int32, (2 * c, c), 1)
    sc = jnp.where((ci >= ri) if rev else (ci <= ri), sc, 0.0)
    vb = v.astype(BF16)
    intra = jnp.dot(sc.astype(BF16), vb, preferred_element_type=F32)
    qs = (q * jnp.exp(cum)).astype(BF16)
    inter = lax.dot_general(qs, st.astype(BF16), (((1,), (1,)), ((), ())), preferred_element_type=F32)
    out = jnp.concatenate([intra[0:c, 0:GLA_DV] + inter[:, 0:GLA_DV],
                           intra[c:2 * c, GLA_DV:2 * GLA_DV] + inter[:, GLA_DV:2 * GLA_DV]], axis=1)
    last = cum[0:1, :] if rev else cum[c - 1:c, :]
    kd = (k * jnp.exp(last - cum)).astype(BF16)
    upd = lax.dot_general(vb, kd, (((0,), (0,)), ((), ())), preferred_element_type=F32)
    srow = lax.broadcasted_iota(jnp.int32, (2 * GLA_DV, LANES), 0)
    slane = lax.broadcasted_iota(jnp.int32, (2 * GLA_DV, LANES), 1)
    diag = (srow < GLA_DV) == (slane < GLA_DK)
    st_new = jnp.where(diag, st * jnp.exp(last) + upd, 0.0)
    return out, st_new


def _gla_kernel(qf_ref, kf_ref, vf_ref, laf_ref, qb_ref, kb_ref, vb_ref, lab_ref, of_ref, ob_ref, stf_ref, stb_ref):
    @pl.when(pl.program_id(1) == 0)
    def _():
        stf_ref[...] = jnp.zeros_like(stf_ref)
        stb_ref[...] = jnp.zeros_like(stb_ref)

    nchunk = TOK_TILE // GLA_CHUNK
    npair = GLA_HEADS // 2

    def body(ci, carry):
        rf = pl.multiple_of(ci * GLA_CHUNK, GLA_CHUNK)
        rb = pl.multiple_of((nchunk - 1 - ci) * GLA_CHUNK, GLA_CHUNK)
        for pr in range(npair):
            ks = slice(pr * LANES, (pr + 1) * LANES)
            vs = slice(pr * 2 * GLA_DV, (pr + 1) * 2 * GLA_DV)
            o, st = _gla_chunk(qf_ref[0, pl.ds(rf, GLA_CHUNK), ks], kf_ref[0, pl.ds(rf, GLA_CHUNK), ks],
                               vf_ref[0, pl.ds(rf, GLA_CHUNK), vs], laf_ref[0, pl.ds(rf, GLA_CHUNK), ks],
                               stf_ref[pr], False)
            of_ref[0, pl.ds(rf, GLA_CHUNK), vs] = o
            stf_ref[pr] = st
            o, st = _gla_chunk(qb_ref[0, pl.ds(rb, GLA_CHUNK), ks], kb_ref[0, pl.ds(rb, GLA_CHUNK), ks],
                               vb_ref[0, pl.ds(rb, GLA_CHUNK), vs], lab_ref[0, pl.ds(rb, GLA_CHUNK), ks],
                               stb_ref[pr], True)
            ob_ref[0, pl.ds(rb, GLA_CHUNK), vs] = o
            stb_ref[pr] = st
        return carry

    lax.fori_loop(0, nchunk, body, 0)


def _gla(gq, gk, gv, laf, lab, nct):
    b, l, dk = gq.shape
    dv = gv.shape[2]
    nt = l // TOK_TILE
    fwd = lambda width: pl.BlockSpec((1, TOK_TILE, width), lambda bi, s: (bi, s, 0))
    bwd = lambda width: pl.BlockSpec((1, TOK_TILE, width), lambda bi, s: (bi, _bwd_tile(s, nct, nt), 0))
    npair = GLA_HEADS // 2
    return pl.pallas_call(
        _gla_kernel,
        grid=(b, nt),
        in_specs=[fwd(dk), fwd(dk), fwd(dv), fwd(dk), bwd(dk), bwd(dk), bwd(dv), bwd(dk)],
        out_specs=[fwd(dv), bwd(dv)],
        out_shape=[jax.ShapeDtypeStruct((b, l, dv), F32)] * 2,
        scratch_shapes=[pltpu.VMEM((npair, 2 * GLA_DV, LANES), F32)] * 2,
        compiler_params=_cp("parallel", "arbitrary"),
        name="gla_scan",
    )(gq, gk, gv, laf, gq, gk, gv, lab)


def _ev_out_kernel(att_ref, of_ref, ob_ref, gg_ref, gn_ref, wout_ref, h_ref, mod_ref, lng_ref, lnb_ref, o_ref, *, alpha):
    o = of_ref[0] + ob_ref[0]
    g = gg_ref[0]
    parts = []
    for hd in range(GLA_HEADS):
        sl = slice(hd * GLA_DV, (hd + 1) * GLA_DV)
        oh = o[:, sl]
        y = oh * lax.rsqrt(jnp.mean(oh * oh, axis=-1, keepdims=True) + LN_EPS) * gn_ref[:, sl]
        gh = g[:, sl]
        parts.append((y * (gh * jax.nn.sigmoid(gh))).astype(BF16))
    gla = jnp.concatenate(parts, axis=1)
    na = att_ref.shape[2]
    mix = (jnp.dot(att_ref[0], wout_ref[0:na, :], preferred_element_type=F32)
           + jnp.dot(gla, wout_ref[na:, :], preferred_element_type=F32))
    mod = mod_ref[0]
    o_ref[0] = _ln(alpha * h_ref[0] + mod[2:3] * mix, lng_ref[...], lnb_ref[...])


def _ev_out(att, o_f, o_b, gg, gn, wout, h, modt, lng, lnb, nct, alpha):
    b, l, d = h.shape
    nt = l // TOK_TILE
    tok = lambda a: pl.BlockSpec((1, TOK_TILE, a.shape[2]), lambda bi, i: (bi, i, 0))
    full = lambda a: pl.BlockSpec(a.shape, lambda bi, i: (0,) * a.ndim)
    return pl.pallas_call(
        functools.partial(_ev_out_kernel, alpha=alpha),
        grid=(b, nt),
        in_specs=[tok(att), tok(o_f), tok(o_b), tok(gg), full(gn), full(wout), tok(h),
                  pl.BlockSpec((1, N_MOD, d), _mod_index(nct, 0)), full(lng), full(lnb)],
        out_specs=tok(h),
        out_shape=jax.ShapeDtypeStruct((b, l, d), F32),
        compiler_params=_cp("parallel", "parallel"),
        name="ev_out",
    )(att, o_f, o_b, gg, gn, wout, h, modt, lng, lnb)


def _od_proj_kernel(h_ref, mod_ref, win_ref, br_ref, rnn_ref):
    mod = mod_ref[0]
    xin = (h_ref[0] * (1.0 + mod[1:2]) + mod[0:1]).astype(BF16)
    p = jnp.dot(xin, win_ref[...], preferred_element_type=F32)
    br_ref[0] = p[:, :RG_WIDTH]
    rnn_ref[0] = p[:, RG_WIDTH:]


def _od_proj(h, modt, nct, win):
    b, l, d = h.shape
    nt = l // TOK_TILE
    tok = lambda width: pl.BlockSpec((1, TOK_TILE, width), lambda bi, i: (bi, i, 0))
    return pl.pallas_call(
        _od_proj_kernel,
        grid=(b, nt),
        in_specs=[tok(d), pl.BlockSpec((1, N_MOD, d), _mod_index(nct, 0)),
                  pl.BlockSpec(win.shape, lambda bi, i: (0, 0))],
        out_specs=[tok(RG_WIDTH), tok(RG_WIDTH)],
        out_shape=[jax.ShapeDtypeStruct((b, l, RG_WIDTH), F32)] * 2,
        compiler_params=_cp("parallel", "parallel"),
        name="od_proj",
    )(h, modt, win)


def _rg_prepare(z_ref, pv_ref, nx_ref, prev_zero, next_zero, cw_ref, cb_ref, gw_ref, gb_ref, lam_ref, a_ref, x_ref):
    t = TOK_TILE
    z = z_ref[0]
    pv = jnp.where(prev_zero, 0.0, pv_ref[0])
    nx = jnp.where(next_zero, 0.0, nx_ref[0])
    ext = jnp.concatenate([pv, z, nx], axis=0)
    base = SUBLANES - RG_CONV_LEFT
    y = cb_ref[...] + ext[base:base + t] * cw_ref[0:1, :]
    for tap in range(1, RG_CONV):
        y = y + ext[base + tap:base + tap + t] * cw_ref[tap:tap + 1, :]
    for n in range(RG_BLOCKS):
        sl = slice(n * RG_BW, (n + 1) * RG_BW)
        yb = y[:, sl]
        ri = jnp.dot(yb.astype(BF16), gw_ref[n], preferred_element_type=F32)
        r = jax.nn.sigmoid(ri[:, :RG_BW] + gb_ref[0:1, sl])
        i = jax.nn.sigmoid(ri[:, RG_BW:] + gb_ref[1:2, sl])
        log_a = (-RG_C) * jax.nn.softplus(-lam_ref[:, sl]) * r
        th = jnp.tanh(log_a)
        a_ref[:, sl] = jnp.exp(log_a)
        x_ref[:, sl] = jnp.sqrt(-2.0 * th / (1.0 - th)) * (i * yb)


def _rglru_kernel(zf_ref, pf_ref, nf_ref, zb_ref, pb_ref, nb_ref, cw_ref, cb_ref,
                  gwf_ref, gbf_ref, lamf_ref, gwb_ref, gbb_ref, lamb_ref,
                  yf_ref, yb_ref, af_ref, xf_ref, ab_ref, xb_ref, hf_ref, hb_ref, *, nct, nt):
    s = pl.program_id(1)

    @pl.when(s == 0)
    def _():
        hf_ref[...] = jnp.zeros_like(hf_ref)
        hb_ref[...] = jnp.zeros_like(hb_ref)

    def edges(j):
        return (j == 0) | (j == nct), (j == nct - 1) | (j == nt - 1)

    pz, nz = edges(s)
    _rg_prepare(zf_ref, pf_ref, nf_ref, pz, nz, cw_ref, cb_ref, gwf_ref, gbf_ref, lamf_ref, af_ref, xf_ref)
    pz, nz = edges(_bwd_tile(s, nct, nt))
    _rg_prepare(zb_ref, pb_ref, nb_ref, pz, nz, cw_ref, cb_ref, gwb_ref, gbb_ref, lamb_ref, ab_ref, xb_ref)

    def body(kk, carry):
        hf, hb = carry
        tf = kk
        tb = TOK_TILE - 1 - kk
        hf = af_ref[pl.ds(tf, 1), :] * hf + xf_ref[pl.ds(tf, 1), :]
        yf_ref[0, pl.ds(tf, 1), :] = hf
        hb = ab_ref[pl.ds(tb, 1), :] * hb + xb_ref[pl.ds(tb, 1), :]
        yb_ref[0, pl.ds(tb, 1), :] = hb
        return hf, hb

    hf, hb = lax.fori_loop(0, TOK_TILE, body, (hf_ref[...], hb_ref[...]), unroll=8)
    hf_ref[...] = hf
    hb_ref[...] = hb


def _rglru(rnn, nct, w):
    b, l, wd = rnn.shape
    nt = l // TOK_TILE
    per = TOK_TILE // SUBLANES
    nh = l // SUBLANES
    bt = lambda s: _bwd_tile(s, nct, nt)
    tile = lambda f: pl.BlockSpec((1, TOK_TILE, wd), lambda bi, s: (bi, f(s), 0))
    prev = lambda f: pl.BlockSpec((1, SUBLANES, wd), lambda bi, s: (bi, jnp.maximum(f(s) * per - 1, 0), 0))
    nxt = lambda f: pl.BlockSpec((1, SUBLANES, wd), lambda bi, s: (bi, jnp.minimum((f(s) + 1) * per, nh - 1), 0))
    full = lambda a: pl.BlockSpec(a.shape, lambda bi, s: (0,) * a.ndim)
    ident = lambda s: s
    vm = lambda rows: pltpu.VMEM((rows, wd), F32)
    return pl.pallas_call(
        functools.partial(_rglru_kernel, nct=nct, nt=nt),
        grid=(b, nt),
        in_specs=[tile(ident), prev(ident), nxt(ident), tile(bt), prev(bt), nxt(bt),
                  full(w["cw"]), full(w["cb"]), full(w["gwf"]), full(w["gbf"]), full(w["lamf"]),
                  full(w["gwb"]), full(w["gbb"]), full(w["lamb"])],
        out_specs=[tile(ident), tile(bt)],
        out_shape=[jax.ShapeDtypeStruct((b, l, wd), F32)] * 2,
        scratch_shapes=[vm(TOK_TILE), vm(TOK_TILE), vm(TOK_TILE), vm(TOK_TILE), vm(1), vm(1)],
        compiler_params=_cp("parallel", "arbitrary"),
        name="rglru_scan",
    )(rnn, rnn, rnn, rnn, rnn, rnn, w["cw"], w["cb"], w["gwf"], w["gbf"], w["lamf"], w["gwb"], w["gbb"], w["lamb"])


def _od_out_kernel(br_ref, yf_ref, yb_ref, wout_ref, h_ref, mod_ref, lng_ref, lnb_ref, o_ref, *, alpha):
    m = (_gelu(br_ref[0]) * (yf_ref[0] + yb_ref[0])).astype(BF16)
    mix = jnp.dot(m, wout_ref[...], preferred_element_type=F32)
    mod = mod_ref[0]
    o_ref[0] = _ln(alpha * h_ref[0] + mod[2:3] * mix, lng_ref[...], lnb_ref[...])


def _od_out(br, yf, yb, wout, h, modt, lng, lnb, nct, off, alpha):
    b, l, d = h.shape
    nt = l // TOK_TILE - off
    tok = lambda a: pl.BlockSpec((1, TOK_TILE, a.shape[2]), lambda bi, i: (bi, i + off, 0))
    full = lambda a: pl.BlockSpec(a.shape, lambda bi, i: (0,) * a.ndim)
    return pl.pallas_call(
        functools.partial(_od_out_kernel, alpha=alpha),
        grid=(b, nt),
        in_specs=[tok(br), tok(yf), tok(yb), full(wout), tok(h),
                  pl.BlockSpec((1, N_MOD, d), _mod_index(nct, off)), full(lng), full(lnb)],
        out_specs=pl.BlockSpec((1, TOK_TILE, d), lambda bi, i: (bi, i, 0)),
        out_shape=jax.ShapeDtypeStruct((b, nt * TOK_TILE, d), F32),
        compiler_params=_cp("parallel", "parallel"),
        name="od_out",
    )(br, yf, yb, wout, h, modt, lng, lnb)


def _top16_rows(s, vals_ref, idx_ref, base):
    nk = s.shape[0]
    rid = lax.broadcasted_iota(jnp.int32, s.shape, 0).astype(F32)
    for r in range(PEER_TOPK):
        m = jnp.max(s, axis=0, keepdims=True)
        pos = jnp.min(jnp.where(s == m, rid, float(nk)), axis=0, keepdims=True)
        vals_ref[base + r:base + r + 1, :] = m
        idx_ref[base + r:base + r + 1, :] = pos
        s = jnp.where(rid == pos, -jnp.inf, s)


def _cand_blocks(t1, t2, combine):
    blocks = [combine(t1[0:1], t2[0:16])]
    for a in range(1, 8):
        blocks.append(combine(t1[a:a + 1], t2[0:8]))
    blocks.append(combine(t1[8:16], t2[0:1]))
    return jnp.concatenate(blocks, axis=0)


def _cand_flat_index(ntok):
    a16 = lax.broadcasted_iota(jnp.int32, (16, ntok), 0)
    a8 = lax.broadcasted_iota(jnp.int32, (8, ntok), 0)
    blocks = [a16]
    for a in range(1, 8):
        blocks.append(a * PEER_TOPK + a8)
    blocks.append((a8 + 8) * PEER_TOPK)
    return jnp.concatenate(blocks, axis=0).astype(F32)


def _peer_route_kernel(h_ref, mod_ref, wq_ref, k1_ref, k2_ref, idx_ref, gate_ref, t1_ref, i1_ref, t2_ref, i2_ref, ts_ref, ti_ref):
    mod = mod_ref[0]
    hm = h_ref[0] * (1.0 + mod[4:5]) + mod[3:4]
    q = jnp.dot(hm, wq_ref[...], precision=HI, preferred_element_type=F32)
    ntok = hm.shape[0]
    flat = _cand_flat_index(ntok)
    nc = flat.shape[0]
    nn = (((1,), (1,)), ((), ()))
    for h in range(PEER_HEADS):
        q1 = q[:, h * PEER_QDIM:h * PEER_QDIM + PEER_HALF]
        q2 = q[:, h * PEER_QDIM + PEER_HALF:(h + 1) * PEER_QDIM]
        s1 = lax.dot_general(k1_ref[...], q1, nn, precision=HI, preferred_element_type=F32)
        s2 = lax.dot_general(k2_ref[...], q2, nn, precision=HI, preferred_element_type=F32)
        _top16_rows(s1, t1_ref, i1_ref, 0)
        _top16_rows(s2, t2_ref, i2_ref, 0)
        cs = _cand_blocks(t1_ref[...], t2_ref[...], lambda x, y: x + y)
        ce = _cand_blocks(i1_ref[...], i2_ref[...], lambda x, y: x * PEER_NK + y)
        for r in range(PEER_TOPK):
            m = jnp.max(cs, axis=0, keepdims=True)
            pos = jnp.min(jnp.where(cs == m, flat, float(PEER_TOPK * PEER_TOPK)), axis=0, keepdims=True)
            hit = flat == pos
            row = h * PEER_TOPK + r
            ts_ref[row:row + 1, :] = m
            ti_ref[row:row + 1, :] = jnp.max(jnp.where(hit, ce, -1.0), axis=0, keepdims=True)
            cs = jnp.where(hit, -jnp.inf, cs)
        blk = slice(h * PEER_TOPK, (h + 1) * PEER_TOPK)
        top = ts_ref[blk, :]
        e = jnp.exp(top - top[0:1, :])
        ts_ref[blk, :] = e / jnp.sum(e, axis=0, keepdims=True)
    idx_ref[0] = ti_ref[...].T.astype(jnp.int32)
    gate_ref[0] = ts_ref[...].T


def _peer_route(h, modt, nct, off, wq, k1, k2):
    b, l, d = h.shape
    nt = l // TOK_TILE
    tok = lambda width: pl.BlockSpec((1, TOK_TILE, width), lambda bi, i: (bi, i, 0))
    full = lambda a: pl.BlockSpec(a.shape, lambda bi, i: (0,) * a.ndim)
    vmf = lambda rows: pltpu.VMEM((rows, TOK_TILE), F32)
    return pl.pallas_call(
        _peer_route_kernel,
        grid=(b, nt),
        in_specs=[tok(d), pl.BlockSpec((1, N_MOD, d), _mod_index(nct, off)), full(wq), full(k1), full(k2)],
        out_specs=[tok(PEER_PAIRS), tok(PEER_PAIRS)],
        out_shape=[jax.ShapeDtypeStruct((b, l, PEER_PAIRS), jnp.int32), jax.ShapeDtypeStruct((b, l, PEER_PAIRS), F32)],
        scratch_shapes=[vmf(PEER_TOPK), vmf(PEER_TOPK), vmf(PEER_TOPK), vmf(PEER_TOPK), vmf(PEER_PAIRS), vmf(PEER_PAIRS)],
        compiler_params=_cp("parallel", "parallel"),
        name="peer_route",
    )(h, modt, wq, k1, k2)


def _peer_expert_kernel(idx_ref, idxn_ref, h_ref, mod_ref, gate_ref, lng_ref, lnb_ref, tab_ref, o_ref, buf0, buf1, sem,
                        *, alpha, nsteps):
    g = pl.program_id(0)
    npair = PEER_TOK * PEER_PAIRS
    bufs = (buf0, buf1)

    def row_copy(e, half, p):
        return pltpu.make_async_copy(tab_ref.at[e], bufs[half].at[:, p, :], sem.at[half])

    def wait_all(half):
        pltpu.make_async_copy(bufs[half], bufs[half], sem.at[half]).wait()

    @pl.when(g == 0)
    def _():
        def body(p, carry):
            row_copy(idx_ref[0, 0, p], 0, p).start()
            return carry
        lax.fori_loop(0, npair, body, 0, unroll=8)

    mod = mod_ref[0]
    h = h_ref[...]
    hm = h * (1.0 + mod[4:5]) + mod[3:4]
    gates = gate_ref[...]
    nchunk = h.shape[1] // LANES
    ones = jnp.ones((LANES, LANES), F32)
    eye = (lax.broadcasted_iota(jnp.int32, (LANES, LANES), 0) == lax.broadcasted_iota(jnp.int32, (LANES, LANES), 1))

    def token(bufr, row, t):
        prs = slice(t * PEER_PAIRS, (t + 1) * PEER_PAIRS)
        part = jnp.zeros((PEER_PAIRS, LANES), F32)
        for c in range(nchunk):
            uf = lax.bitcast_convert_type(bufr[c, prs, :] & jnp.uint32(0xFFFF0000), F32)
            part = part + uf * hm[row:row + 1, c * LANES:(c + 1) * LANES]
        act = jnp.dot(part, ones, precision=HI, preferred_element_type=F32)
        gdiag = jnp.where(eye, gates[row:row + 1, :], 0.0)
        coef = _gelu(act) * jnp.dot(gdiag, ones, precision=HI, preferred_element_type=F32)
        outs = []
        for c in range(nchunk):
            vf = lax.bitcast_convert_type(bufr[c, prs, :] << 16, F32)
            outs.append(jnp.sum(coef * vf, axis=0, keepdims=True))
        return jnp.concatenate(outs, axis=1)

    rows = []
    for half in range(2):
        wait_all(half)
        nxt_ref, nxt_base = (idx_ref, npair) if half == 0 else (idxn_ref, 0)
        for t in range(PEER_TOK):
            for p in range(t * PEER_PAIRS, (t + 1) * PEER_PAIRS):
                row_copy(nxt_ref[0, 0, nxt_base + p], 1 - half, p).start(priority=p % 2)
            rows.append(token(bufs[half], half * PEER_TOK + t, t))
    f = jnp.concatenate(rows, axis=0)
    o_ref[...] = _ln(alpha * h + mod[5:6] * f, lng_ref[...], lnb_ref[...])

    @pl.when(g == nsteps - 1)
    def _():
        wait_all(0)


def _peer_expert(h, modt, nct, off, idx, gate, table, lng, lnb, alpha):
    b, l, d = h.shape
    n = b * l
    step_tok = 2 * PEER_TOK
    nsteps = n // step_tok
    per_b = l // step_tok
    tiles_per = TOK_TILE // step_tok
    npair = PEER_TOK * PEER_PAIRS
    h2 = h.reshape(n, d)
    idx3 = idx.reshape(nsteps, 1, 2 * npair)
    gate2 = gate.reshape(n, PEER_PAIRS)
    smem = lambda f: pl.BlockSpec((1, 1, 2 * npair), f, memory_space=pltpu.SMEM)
    mod_idx = _mod_index(nct, off)
    pair_buf = pltpu.VMEM((d // LANES, npair, LANES), jnp.uint32)
    out = pl.pallas_call(
        functools.partial(_peer_expert_kernel, alpha=alpha, nsteps=nsteps),
        grid=(nsteps,),
        in_specs=[smem(lambda g: (g, 0, 0)), smem(lambda g: (jnp.minimum(g + 1, nsteps - 1), 0, 0)),
                  pl.BlockSpec((step_tok, d), lambda g: (g, 0)),
                  pl.BlockSpec((1, N_MOD, d), lambda g: mod_idx(g // per_b, (g % per_b) // tiles_per)),
                  pl.BlockSpec((step_tok, PEER_PAIRS), lambda g: (g, 0)),
                  pl.BlockSpec(lng.shape, lambda g: (0, 0)), pl.BlockSpec(lnb.shape, lambda g: (0, 0)),
                  pl.BlockSpec(memory_space=pl.ANY)],
        out_specs=pl.BlockSpec((step_tok, d), lambda g: (g, 0)),
        out_shape=jax.ShapeDtypeStruct((n, d), F32),
        scratch_shapes=[pair_buf, pair_buf, pltpu.SemaphoreType.DMA((2,))],
        compiler_params=_cp("arbitrary"),
        name="peer_expert",
    )(idx3, idx3, h2, modt, gate2, lng, lnb, table)
    return out.reshape(b, l, d)


def _peer_table(u, v):
    n, d = u.shape
    ub = lax.bitcast_convert_type(u.astype(BF16), jnp.uint16).astype(jnp.uint32)
    vb = lax.bitcast_convert_type(v.astype(BF16), jnp.uint16).astype(jnp.uint32)
    return ((ub << 16) | vb).reshape(n, d // LANES, LANES)


def _rope_tables(seq_len, ctx_len):
    rows = seq_len // GRID_W
    row = np.repeat(np.arange(rows, dtype=np.float32), GRID_W)
    col = np.tile(np.arange(GRID_W, dtype=np.float32), rows)
    n_freq = MLA_ROPE // 4
    inv = jnp.power(ROPE_BASE, -jnp.arange(n_freq, dtype=F32) / n_freq)
    ang = jnp.concatenate([row[:, None] * inv, col[:, None] * inv], axis=-1)
    ang = jnp.concatenate([jnp.zeros((ctx_len, MLA_ROPE // 2), F32), ang], axis=0)
    cos, sin = jnp.cos(ang), jnp.sin(ang)
    l = seq_len + ctx_len
    half = MLA_ROPE // 2
    c = jnp.ones((l, LANES), F32).at[:, MLA_NOPE:MLA_NOPE + half].set(cos).at[:, MLA_NOPE + half:MLA_NOPE + 2 * half].set(cos)
    s1 = jnp.zeros((l, LANES), F32).at[:, MLA_NOPE:MLA_NOPE + half].set(-sin)
    s2 = jnp.zeros((l, LANES), F32).at[:, MLA_NOPE + half:MLA_NOPE + 2 * half].set(sin)
    return c, s1, s2


def _ev_weights(w_in, q_norm, kv_norm, w_uq, w_ukv, wa2, ba, seq_len, ctx_len):
    d = w_in.shape[0]
    widths = (MLA_Q_RANK, MLA_KV_RANK, MLA_ROPE, GLA_HEADS * GLA_DK, GLA_HEADS * GLA_DK,
              GLA_HEADS * GLA_DV, GLA_HEADS * GLA_DV, GLA_GATE_RANK, GLA_GATE_RANK)
    offs = np.cumsum((0,) + widths)
    part = lambda i: w_in[:, offs[i]:offs[i + 1]]
    kpe = jnp.zeros((d, HEAD_PAD), F32).at[:, MLA_NOPE:MLA_NOPE + MLA_ROPE].set(part(2))
    gate = jnp.zeros((d, LANES), F32).at[:, :GLA_GATE_RANK].set(part(7)).at[:, GLA_GATE_RANK:2 * GLA_GATE_RANK].set(part(8))
    win = jnp.concatenate([part(0), part(1), kpe, part(3), part(4), part(5), part(6), gate], axis=1).astype(BF16)
    hq = MLA_NOPE + MLA_ROPE
    wuq = jnp.pad(w_uq.reshape(MLA_Q_RANK, MLA_HEADS, hq), ((0, 0), (0, 0), (0, HEAD_PAD - hq)))
    wuq = wuq.reshape(MLA_Q_RANK, MLA_HEADS * HEAD_PAD).astype(BF16)
    ukv = w_ukv.reshape(MLA_KV_RANK, MLA_HEADS, MLA_NOPE + MLA_V)
    wk = jnp.pad(ukv[:, :, :MLA_NOPE], ((0, 0), (0, 0), (0, HEAD_PAD - MLA_NOPE))).reshape(MLA_KV_RANK, MLA_HEADS * HEAD_PAD)
    wv = ukv[:, :, MLA_NOPE:].reshape(MLA_KV_RANK, MLA_HEADS * MLA_V)
    wukv = jnp.concatenate([wk, wv], axis=1).astype(BF16)
    gdk = GLA_HEADS * GLA_DK
    wa2p = jnp.zeros((LANES, 2 * gdk), F32).at[:GLA_GATE_RANK, :gdk].set(wa2[0])
    wa2p = wa2p.at[GLA_GATE_RANK:2 * GLA_GATE_RANK, gdk:].set(wa2[1]).astype(BF16)
    c, s1, s2 = _rope_tables(seq_len, ctx_len)
    return dict(win=win, qn=q_norm[None, :], kvn=kv_norm[None, :], wuq=wuq, wukv=wukv, wa2=wa2p,
                ba=ba.reshape(1, 2 * gdk), rope_c=c, rope_s1=s1, rope_s2=s2)


def _od_weights(conv_w, conv_b, gate_w, gate_b, lam):
    gw = lambda dd: jnp.concatenate([gate_w[dd, 0], gate_w[dd, 1]], axis=-1).astype(BF16)
    return dict(cw=conv_w, cb=conv_b[None, :], gwf=gw(0), gbf=gate_b[0], lamf=lam[0][None, :],
                gwb=gw(1), gbb=gate_b[1], lamb=lam[1][None, :])


def kernel(x, c, ctx, c_ctx, mod_w, mod_b, ln_g, ln_b,
           ev_w_in, ev_q_norm, ev_kv_norm, ev_w_uq, ev_w_ukv, ev_gla_wa2, ev_gla_ba, ev_gla_norm, ev_w_out,
           od_w_in, od_conv_w, od_conv_b, od_gate_w, od_gate_b, od_lambda, od_w_out,
           peer_wq, peer_k1, peer_k2, peer_u, peer_v):
    b, seq_len, d = x.shape
    ctx_len = ctx.shape[1]
    depth = mod_w.shape[0]
    assert ctx_len % TOK_TILE == 0 and seq_len % TOK_TILE == 0 and seq_len % GRID_W == 0
    alpha = (2 * depth) ** 0.25

    rows = -(-(b + 1) // SUBLANES) * SUBLANES
    cc = jnp.zeros((rows, d), F32).at[:b].set(c).at[b].set(c_ctx)

    h = jnp.concatenate([ctx, x], axis=1)
    nct = ctx_len // TOK_TILE
    off = 0
    for layer in range(depth):
        last = layer == depth - 1
        j = layer // 2
        mods = _mod_vectors(cc, mod_w[layer], mod_b[layer][None, :]).reshape(rows, N_MOD, d)
        modt = jnp.stack([jnp.broadcast_to(mods[b], (b, N_MOD, d)), mods[:b]], axis=1).reshape(2 * b, N_MOD, d)
        lng = ln_g[layer][:, None, :]
        lnb = ln_b[layer][:, None, :]
        assert off == 0
        if layer % 2 == 0:
            w = _ev_weights(ev_w_in[j], ev_q_norm[j], ev_kv_norm[j], ev_w_uq[j], ev_w_ukv[j],
                            ev_gla_wa2[j], ev_gla_ba[j], seq_len, ctx_len)
            q, k, v, gq, gk, gv, gg, laf, lab = _ev_proj(h, modt, nct, w)
            att = _attention(q, k, v, nct, ctx_len)
            o_f, o_b = _gla(gq, gk, gv, laf, lab, nct)
            assert not last, "the final layer drops the context rows only after an odd layer"
            h = _ev_out(att, o_f, o_b, gg, ev_gla_norm[j][None, :], ev_w_out[j].astype(BF16), h, modt,
                        lng[0], lnb[0], nct, alpha)
        else:
            w = _od_weights(od_conv_w[j], od_conv_b[j], od_gate_w[j], od_gate_b[j], od_lambda[j])
            br, rnn = _od_proj(h, modt, nct, od_w_in[j].astype(BF16))
            yf, yb = _rglru(rnn, nct, w)
            off = nct if last else 0
            h = _od_out(br, yf, yb, od_w_out[j].astype(BF16), h, modt, lng[0], lnb[0], nct, off, alpha)
        idx, gate = _peer_route(h, modt, nct, off, peer_wq[layer], peer_k1[layer], peer_k2[layer])
        table = _peer_table(peer_u[layer], peer_v[layer])
        h = _peer_expert(h, modt, nct, off, idx, gate, table, lng[1], lnb[1], alpha)
    return h if off else h[:, ctx_len:]
```

```python
import functools
import math

import numpy as np
import jax
import jax.numpy as jnp
from jax import lax
from jax.experimental import pallas as pl
from jax.experimental.pallas import tpu as pltpu

F32 = jnp.float32
BF16 = jnp.bfloat16
HI = lax.Precision.HIGHEST

N_MOD = 6
LN_EPS = 1e-6
GRID_W = 64
MLA_HEADS = 8
MLA_Q_RANK = 384
MLA_KV_RANK = 256
MLA_NOPE = 64
MLA_ROPE = 32
MLA_V = 64
MLA_SCALE = (MLA_NOPE + MLA_ROPE) ** -0.5
ROPE_BASE = 10000.0
GLA_HEADS = 4
GLA_DK = 64
GLA_DV = 128
GLA_GATE_RANK = 16
GLA_TAU = 16.0
RG_WIDTH = 1280
RG_BLOCKS = 10
RG_BW = RG_WIDTH // RG_BLOCKS
RG_CONV = 4
RG_CONV_LEFT = 2
RG_C = 8.0
PEER_HEADS = 8
PEER_NK = 128
PEER_TOPK = 16
PEER_QDIM = 256
PEER_HALF = PEER_QDIM // 2

LANES = 128
SUBLANES = 8
TOK_TILE = 256
GLA_CHUNK = 64
GLA_SUB = 16
PEER_TOK = 8
PEER_PITCH = 12
VMEM_LIMIT = 56 * 1024 * 1024

HEAD_PAD = LANES
PEER_PAIRS = PEER_HEADS * PEER_TOPK


def _cp(*sem):
    return pltpu.CompilerParams(dimension_semantics=sem, vmem_limit_bytes=VMEM_LIMIT)


def _gelu(x):
    return 0.5 * x * (1.0 + lax.erf(x * (2.0 ** -0.5)))


def _ln(y, g, b):
    mu = jnp.mean(y, axis=-1, keepdims=True)
    d = y - mu
    var = jnp.mean(d * d, axis=-1, keepdims=True)
    return d * lax.rsqrt(var + LN_EPS) * g + b


def _mod_index(nct, off):
    def index(b, i):
        return (2 * b + jnp.where(i + off >= nct, 1, 0), 0, 0)
    return index


def _mod_kernel(c_ref, w_ref, b_ref, o_ref):
    x = c_ref[...]
    s = x * jax.nn.sigmoid(x)
    o_ref[...] = jnp.dot(s, w_ref[...], precision=HI, preferred_element_type=F32) + b_ref[...]


def _mod_vectors(cc, w, b):
    rows, d = cc.shape
    n = w.shape[1] // d
    return pl.pallas_call(
        _mod_kernel,
        grid=(n,),
        in_specs=[pl.BlockSpec((rows, d), lambda j: (0, 0)),
                  pl.BlockSpec((d, d), lambda j: (0, j)),
                  pl.BlockSpec((1, d), lambda j: (0, j))],
        out_specs=pl.BlockSpec((rows, d), lambda j: (0, j)),
        out_shape=jax.ShapeDtypeStruct((rows, w.shape[1]), F32),
        compiler_params=_cp("arbitrary"),
        name="mod_vectors",
    )(cc, w, b)


def _rope_block(x, c, s1, s2):
    return x * c + pltpu.roll(x, LANES - MLA_ROPE // 2, axis=1) * s1 + pltpu.roll(x, MLA_ROPE // 2, axis=1) * s2


def _ev_proj_kernel(h_ref, mod_ref, win_ref, qn_ref, kvn_ref, wuq_ref, wukv_ref, wa2_ref, ba_ref,
                    rc_ref, rs1_ref, rs2_ref,
                    q_ref, k_ref, v_ref, gq_ref, gk_ref, gv_ref, gg_ref, laf_ref, lab_ref):
    x = h_ref[0]
    mod = mod_ref[0]
    xin = (x * (1.0 + mod[1:2]) + mod[0:1]).astype(BF16)
    p = jnp.dot(xin, win_ref[...], preferred_element_type=F32)
    o = 0
    cq = p[:, o:o + MLA_Q_RANK]; o += MLA_Q_RANK
    ckv = p[:, o:o + MLA_KV_RANK]; o += MLA_KV_RANK
    kpe = p[:, o:o + HEAD_PAD]; o += HEAD_PAD
    gdk = GLA_HEADS * GLA_DK
    gdv = GLA_HEADS * GLA_DV
    gq_ref[0] = p[:, o:o + gdk] * (GLA_DK ** -0.5); o += gdk
    gk_ref[0] = p[:, o:o + gdk]; o += gdk
    gv_ref[0] = p[:, o:o + gdv]; o += gdv
    gg_ref[0] = p[:, o:o + gdv]; o += gdv
    ga = p[:, o:o + LANES]

    c, s1, s2 = rc_ref[...], rs1_ref[...], rs2_ref[...]
    nq = cq * lax.rsqrt(jnp.mean(cq * cq, axis=-1, keepdims=True) + LN_EPS) * qn_ref[...]
    q = jnp.dot(nq.astype(BF16), wuq_ref[...], preferred_element_type=F32)
    nkv = ckv * lax.rsqrt(jnp.mean(ckv * ckv, axis=-1, keepdims=True) + LN_EPS) * kvn_ref[...]
    kv = jnp.dot(nkv.astype(BF16), wukv_ref[...], preferred_element_type=F32)
    kpe_r = _rope_block(kpe, c, s1, s2)
    for h in range(MLA_HEADS):
        sl = slice(h * HEAD_PAD, (h + 1) * HEAD_PAD)
        q_ref[0, :, sl] = _rope_block(q[:, sl], c, s1, s2).astype(BF16)
        k_ref[0, :, sl] = (kv[:, sl] + kpe_r).astype(BF16)
    v_ref[0] = kv[:, MLA_HEADS * HEAD_PAD:].astype(BF16)

    z = jnp.dot(ga.astype(BF16), wa2_ref[...], preferred_element_type=F32) + ba_ref[...]
    la = jax.nn.log_sigmoid(z) * (1.0 / GLA_TAU)
    laf_ref[0] = la[:, :gdk]
    lab_ref[0] = la[:, gdk:]


def _ev_proj(h, modt, nct, w):
    b, l, d = h.shape
    nt = l // TOK_TILE
    tok = lambda width: pl.BlockSpec((1, TOK_TILE, width), lambda bi, i: (bi, i, 0))
    full = lambda a: pl.BlockSpec(a.shape, lambda bi, i: (0,) * a.ndim)
    rope = pl.BlockSpec((TOK_TILE, LANES), lambda bi, i: (i, 0))
    gdk = GLA_HEADS * GLA_DK
    gdv = GLA_HEADS * GLA_DV
    widths = [(MLA_HEADS * HEAD_PAD, BF16), (MLA_HEADS * HEAD_PAD, BF16), (MLA_HEADS * MLA_V, BF16),
              (gdk, F32), (gdk, F32), (gdv, F32), (gdv, F32), (gdk, F32), (gdk, F32)]
    return pl.pallas_call(
        _ev_proj_kernel,
        grid=(b, nt),
        in_specs=[tok(d), pl.BlockSpec((1, N_MOD, d), _mod_index(nct, 0)),
                  full(w["win"]), full(w["qn"]), full(w["kvn"]), full(w["wuq"]), full(w["wukv"]),
                  full(w["wa2"]), full(w["ba"]), rope, rope, rope],
        out_specs=[tok(wd) for wd, _ in widths],
        out_shape=[jax.ShapeDtypeStruct((b, l, wd), dt) for wd, dt in widths],
        compiler_params=_cp("parallel", "parallel"),
        name="ev_proj",
    )(h, modt, w["win"], w["qn"], w["kvn"], w["wuq"], w["wukv"], w["wa2"], w["ba"],
      w["rope_c"], w["rope_s1"], w["rope_s2"])


def _attn_kernel(q_ref, k_ref, v_ref, o_ref, *, nct, ctx_len):
    def attend(nk):
        lane = lax.broadcasted_iota(jnp.int32, (TOK_TILE, LANES), 1)
        for hp in range(MLA_HEADS // 2):
            vp = v_ref[0, 0:nk, hp * LANES:(hp + 1) * LANES]
            outs = []
            for hh in range(2):
                h = 2 * hp + hh
                qh = q_ref[0, :, h * HEAD_PAD:(h + 1) * HEAD_PAD]
                kh = k_ref[0, 0:nk, h * HEAD_PAD:(h + 1) * HEAD_PAD]
                s = lax.dot_general(qh, kh, (((1,), (1,)), ((), ())), preferred_element_type=F32)
                m = jnp.max(s, axis=-1, keepdims=True)
                p = jnp.exp((s - m) * MLA_SCALE)
                den = jnp.sum(p, axis=-1, keepdims=True)
                o = jnp.dot(p.astype(BF16), vp, preferred_element_type=F32)
                outs.append(o / den)
            o_ref[0, :, hp * LANES:(hp + 1) * LANES] = jnp.where(lane < MLA_V, outs[0], outs[1]).astype(BF16)

    is_ctx = pl.program_id(1) < nct

    @pl.when(is_ctx)
    def _():
        attend(ctx_len)

    @pl.when(jnp.logical_not(is_ctx))
    def _():
        attend(k_ref.shape[1])


def _attention(q, k, v, nct, ctx_len):
    b, l, _ = q.shape
    nt = l // TOK_TILE
    return pl.pallas_call(
        functools.partial(_attn_kernel, nct=nct, ctx_len=ctx_len),
        grid=(b, nt),
        in_specs=[pl.BlockSpec((1, TOK_TILE, q.shape[2]), lambda bi, i: (bi, i, 0)),
                  pl.BlockSpec((1, l, k.shape[2]), lambda bi, i: (bi, 0, 0)),
                  pl.BlockSpec((1, l, v.shape[2]), lambda bi, i: (bi, 0, 0))],
        out_specs=pl.BlockSpec((1, TOK_TILE, v.shape[2]), lambda bi, i: (bi, i, 0)),
        out_shape=jax.ShapeDtypeStruct((b, l, v.shape[2]), BF16),
        compiler_params=_cp("parallel", "arbitrary"),
        name="mla_attention",
    )(q, k, v)


def _bwd_tile(s, nct, nt):
    return jnp.where(s < nct, nct - 1 - s, nt - 1 - (s - nct))


def _gla_chunk(q, k, v, la, st, rev):
    c = GLA_CHUNK
    nsub = c // GLA_SUB
    row = lax.broadcasted_iota(jnp.int32, (c, c), 0)
    col = lax.broadcasted_iota(jnp.int32, (c, c), 1)
    tri = (col >= row) if rev else (col <= row)
    cum = jnp.dot(tri.astype(F32), la, precision=HI, preferred_element_type=F32)
    rid = lax.broadcasted_iota(jnp.int32, (c, LANES), 0)
    lane = lax.broadcasted_iota(jnp.int32, (c, LANES), 1)
    head0 = lane < GLA_DK

    qparts, kparts, eref = [], [], jnp.zeros((c, LANES), F32)
    for jb in range(nsub):
        r = jb * GLA_SUB if rev else jb * GLA_SUB + GLA_SUB - 1
        ej = cum[r:r + 1, :]
        valid = (rid <= jb * GLA_SUB + GLA_SUB - 1) if rev else (rid >= jb * GLA_SUB)
        qparts.append(jnp.where(valid, q * jnp.exp(jnp.where(valid, cum - ej, 0.0)), 0.0))
        inblk = (rid >= jb * GLA_SUB) & (rid < (jb + 1) * GLA_SUB)
        eref = jnp.where(inblk, ej, eref)
    kt = k * jnp.exp(eref - cum)
    for jb in range(nsub):
        inblk = (rid >= jb * GLA_SUB) & (rid < (jb + 1) * GLA_SUB)
        kparts.append(jnp.where(inblk, kt, 0.0))
    qbig = jnp.concatenate(qparts, axis=1)
    kbig = jnp.concatenate(kparts, axis=1).astype(BF16)
    h0big = lax.broadcasted_iota(jnp.int32, (c, nsub * LANES), 1) % LANES < GLA_DK
    qst = jnp.concatenate([jnp.where(h0big, qbig, 0.0), jnp.where(h0big, 0.0, qbig)], axis=0).astype(BF16)
    sc = lax.dot_general(qst, kbig, (((1,), (1,)), ((), ())), preferred_element_type=F32)
    ri = lax.broadcasted_iota(jnp.int32, (2 * c, c), 0) % c
    ci = lax.broadcasted_iota(jnp.int32, (2 * c, c), 1)
    sc = jnp.where((ci >= ri) if rev else (ci <= ri), sc, 0.0)
    vb = v.astype(BF16)
    intra = jnp.dot(sc.astype(BF16), vb, preferred_element_type=F32)
    qs = (q * jnp.exp(cum)).astype(BF16)
    inter = lax.dot_general(qs, st.astype(BF16), (((1,), (1,)), ((), ())), preferred_element_type=F32)
    out = jnp.concatenate([intra[0:c, 0:GLA_DV] + inter[:, 0:GLA_DV],
                           intra[c:2 * c, GLA_DV:2 * GLA_DV] + inter[:, GLA_DV:2 * GLA_DV]], axis=1)
    last = cum[0:1, :] if rev else cum[c - 1:c, :]
    kd = (k * jnp.exp(last - cum)).astype(BF16)
    upd = lax.dot_general(vb, kd, (((0,), (0,)), ((), ())), preferred_element_type=F32)
    srow = lax.broadcasted_iota(jnp.int32, (2 * GLA_DV, LANES), 0)
    slane = lax.broadcasted_iota(jnp.int32, (2 * GLA_DV, LANES), 1)
    diag = (srow < GLA_DV) == (slane < GLA_DK)
    st_new = jnp.where(diag, st * jnp.exp(last) + upd, 0.0)
    return out, st_new


def _gla_kernel(qf_ref, kf_ref, vf_ref, laf_ref, qb_ref, kb_ref, vb_ref, lab_ref, of_ref, ob_ref, stf_ref, stb_ref):
    @pl.when(pl.program_id(1) == 0)
    def _():
        stf_ref[...] = jnp.zeros_like(stf_ref)
        stb_ref[...] = jnp.zeros_like(stb_ref)

    nchunk = TOK_TILE // GLA_CHUNK
    npair = GLA_HEADS // 2

    def body(ci, carry):
        rf = pl.multiple_of(ci * GLA_CHUNK, GLA_CHUNK)
        rb = pl.multiple_of((nchunk - 1 - ci) * GLA_CHUNK, GLA_CHUNK)
        for pr in range(npair):
            ks = slice(pr * LANES, (pr + 1) * LANES)
            vs = slice(pr * 2 * GLA_DV, (pr + 1) * 2 * GLA_DV)
            o, st = _gla_chunk(qf_ref[0, pl.ds(rf, GLA_CHUNK), ks], kf_ref[0, pl.ds(rf, GLA_CHUNK), ks],
                               vf_ref[0, pl.ds(rf, GLA_CHUNK), vs], laf_ref[0, pl.ds(rf, GLA_CHUNK), ks],
                               stf_ref[pr], False)
            of_ref[0, pl.ds(rf, GLA_CHUNK), vs] = o
            stf_ref[pr] = st
            o, st = _gla_chunk(qb_ref[0, pl.ds(rb, GLA_CHUNK), ks], kb_ref[0, pl.ds(rb, GLA_CHUNK), ks],
                               vb_ref[0, pl.ds(rb, GLA_CHUNK), vs], lab_ref[0, pl.ds(rb, GLA_CHUNK), ks],
                               stb_ref[pr], True)
            ob_ref[0, pl.ds(rb, GLA_CHUNK), vs] = o
            stb_ref[pr] = st
        return carry

    lax.fori_loop(0, nchunk, body, 0)


def _gla(gq, gk, gv, laf, lab, nct):
    b, l, dk = gq.shape
    dv = gv.shape[2]
    nt = l // TOK_TILE
    fwd = lambda width: pl.BlockSpec((1, TOK_TILE, width), lambda bi, s: (bi, s, 0))
    bwd = lambda width: pl.BlockSpec((1, TOK_TILE, width), lambda bi, s: (bi, _bwd_tile(s, nct, nt), 0))
    npair = GLA_HEADS // 2
    return pl.pallas_call(
        _gla_kernel,
        grid=(b, nt),
        in_specs=[fwd(dk), fwd(dk), fwd(dv), fwd(dk), bwd(dk), bwd(dk), bwd(dv), bwd(dk)],
        out_specs=[fwd(dv), bwd(dv)],
        out_shape=[jax.ShapeDtypeStruct((b, l, dv), F32)] * 2,
        scratch_shapes=[pltpu.VMEM((npair, 2 * GLA_DV, LANES), F32)] * 2,
        compiler_params=_cp("parallel", "arbitrary"),
        name="gla_scan",
    )(gq, gk, gv, laf, gq, gk, gv, lab)


def _ev_out_kernel(att_ref, of_ref, ob_ref, gg_ref, gn_ref, wout_ref, h_ref, mod_ref, lng_ref, lnb_ref, o_ref, *, alpha):
    o = of_ref[0] + ob_ref[0]
    g = gg_ref[0]
    parts = []
    for hd in range(GLA_HEADS):
        sl = slice(hd * GLA_DV, (hd + 1) * GLA_DV)
        oh = o[:, sl]
        y = oh * lax.rsqrt(jnp.mean(oh * oh, axis=-1, keepdims=True) + LN_EPS) * gn_ref[:, sl]
        gh = g[:, sl]
        parts.append((y * (gh * jax.nn.sigmoid(gh))).astype(BF16))
    gla = jnp.concatenate(parts, axis=1)
    na = att_ref.shape[2]
    mix = (jnp.dot(att_ref[0], wout_ref[0:na, :], preferred_element_type=F32)
           + jnp.dot(gla, wout_ref[na:, :], preferred_element_type=F32))
    mod = mod_ref[0]
    o_ref[0] = _ln(alpha * h_ref[0] + mod[2:3] * mix, lng_ref[...], lnb_ref[...])


def _ev_out(att, o_f, o_b, gg, gn, wout, h, modt, lng, lnb, nct, alpha):
    b, l, d = h.shape
    nt = l // TOK_TILE
    tok = lambda a: pl.BlockSpec((1, TOK_TILE, a.shape[2]), lambda bi, i: (bi, i, 0))
    full = lambda a: pl.BlockSpec(a.shape, lambda bi, i: (0,) * a.ndim)
    return pl.pallas_call(
        functools.partial(_ev_out_kernel, alpha=alpha),
        grid=(b, nt),
        in_specs=[tok(att), tok(o_f), tok(o_b), tok(gg), full(gn), full(wout), tok(h),
                  pl.BlockSpec((1, N_MOD, d), _mod_index(nct, 0)), full(lng), full(lnb)],
        out_specs=tok(h),
        out_shape=jax.ShapeDtypeStruct((b, l, d), F32),
        compiler_params=_cp("parallel", "parallel"),
        name="ev_out",
    )(att, o_f, o_b, gg, gn, wout, h, modt, lng, lnb)


def _od_proj_kernel(h_ref, mod_ref, win_ref, br_ref, rnn_ref):
    mod = mod_ref[0]
    xin = (h_ref[0] * (1.0 + mod[1:2]) + mod[0:1]).astype(BF16)
    p = jnp.dot(xin, win_ref[...], preferred_element_type=F32)
    br_ref[0] = p[:, :RG_WIDTH]
    rnn_ref[0] = p[:, RG_WIDTH:]


def _od_proj(h, modt, nct, win):
    b, l, d = h.shape
    nt = l // TOK_TILE
    tok = lambda width: pl.BlockSpec((1, TOK_TILE, width), lambda bi, i: (bi, i, 0))
    return pl.pallas_call(
        _od_proj_kernel,
        grid=(b, nt),
        in_specs=[tok(d), pl.BlockSpec((1, N_MOD, d), _mod_index(nct, 0)),
                  pl.BlockSpec(win.shape, lambda bi, i: (0, 0))],
        out_specs=[tok(RG_WIDTH), tok(RG_WIDTH)],
        out_shape=[jax.ShapeDtypeStruct((b, l, RG_WIDTH), F32)] * 2,
        compiler_params=_cp("parallel", "parallel"),
        name="od_proj",
    )(h, modt, win)


def _rg_prepare(z_ref, pv_ref, nx_ref, prev_zero, next_zero, cw_ref, cb_ref, gw_ref, gb_ref, lam_ref, a_ref, x_ref):
    t = TOK_TILE
    z = z_ref[0]
    pv = jnp.where(prev_zero, 0.0, pv_ref[0])
    nx = jnp.where(next_zero, 0.0, nx_ref[0])
    ext = jnp.concatenate([pv, z, nx], axis=0)
    base = SUBLANES - RG_CONV_LEFT
    y = cb_ref[...] + ext[base:base + t] * cw_ref[0:1, :]
    for tap in range(1, RG_CONV):
        y = y + ext[base + tap:base + tap + t] * cw_ref[tap:tap + 1, :]
    for n in range(RG_BLOCKS):
        sl = slice(n * RG_BW, (n + 1) * RG_BW)
        yb = y[:, sl]
        ri = jnp.dot(yb.astype(BF16), gw_ref[n], preferred_element_type=F32)
        r = jax.nn.sigmoid(ri[:, :RG_BW] + gb_ref[0:1, sl])
        i = jax.nn.sigmoid(ri[:, RG_BW:] + gb_ref[1:2, sl])
        log_a = (-RG_C) * jax.nn.softplus(-lam_ref[:, sl]) * r
        th = jnp.tanh(log_a)
        a_ref[:, sl] = jnp.exp(log_a)
        x_ref[:, sl] = jnp.sqrt(-2.0 * th / (1.0 - th)) * (i * yb)


def _rglru_kernel(zf_ref, pf_ref, nf_ref, zb_ref, pb_ref, nb_ref, cw_ref, cb_ref,
                  gwf_ref, gbf_ref, lamf_ref, gwb_ref, gbb_ref, lamb_ref,
                  yf_ref, yb_ref, af_ref, xf_ref, ab_ref, xb_ref, hf_ref, hb_ref, *, nct, nt):
    s = pl.program_id(1)

    @pl.when(s == 0)
    def _():
        hf_ref[...] = jnp.zeros_like(hf_ref)
        hb_ref[...] = jnp.zeros_like(hb_ref)

    def edges(j):
        return (j == 0) | (j == nct), (j == nct - 1) | (j == nt - 1)

    pz, nz = edges(s)
    _rg_prepare(zf_ref, pf_ref, nf_ref, pz, nz, cw_ref, cb_ref, gwf_ref, gbf_ref, lamf_ref, af_ref, xf_ref)
    pz, nz = edges(_bwd_tile(s, nct, nt))
    _rg_prepare(zb_ref, pb_ref, nb_ref, pz, nz, cw_ref, cb_ref, gwb_ref, gbb_ref, lamb_ref, ab_ref, xb_ref)

    def body(kk, carry):
        hf, hb = carry
        tf = kk
        tb = TOK_TILE - 1 - kk
        hf = af_ref[pl.ds(tf, 1), :] * hf + xf_ref[pl.ds(tf, 1), :]
        yf_ref[0, pl.ds(tf, 1), :] = hf
        hb = ab_ref[pl.ds(tb, 1), :] * hb + xb_ref[pl.ds(tb, 1), :]
        yb_ref[0, pl.ds(tb, 1), :] = hb
        return hf, hb

    hf, hb = lax.fori_loop(0, TOK_TILE, body, (hf_ref[...], hb_ref[...]), unroll=8)
    hf_ref[...] = hf
    hb_ref[...] = hb


def _rglru(rnn, nct, w):
    b, l, wd = rnn.shape
    nt = l // TOK_TILE
    per = TOK_TILE // SUBLANES
    nh = l // SUBLANES
    bt = lambda s: _bwd_tile(s, nct, nt)
    tile = lambda f: pl.BlockSpec((1, TOK_TILE, wd), lambda bi, s: (bi, f(s), 0))
    prev = lambda f: pl.BlockSpec((1, SUBLANES, wd), lambda bi, s: (bi, jnp.maximum(f(s) * per - 1, 0), 0))
    nxt = lambda f: pl.BlockSpec((1, SUBLANES, wd), lambda bi, s: (bi, jnp.minimum((f(s) + 1) * per, nh - 1), 0))
    full = lambda a: pl.BlockSpec(a.shape, lambda bi, s: (0,) * a.ndim)
    ident = lambda s: s
    vm = lambda rows: pltpu.VMEM((rows, wd), F32)
    return pl.pallas_call(
        functools.partial(_rglru_kernel, nct=nct, nt=nt),
        grid=(b, nt),
        in_specs=[tile(ident), prev(ident), nxt(ident), tile(bt), prev(bt), nxt(bt),
                  full(w["cw"]), full(w["cb"]), full(w["gwf"]), full(w["gbf"]), full(w["lamf"]),
                  full(w["gwb"]), full(w["gbb"]), full(w["lamb"])],
        out_specs=[tile(ident), tile(bt)],
        out_shape=[jax.ShapeDtypeStruct((b, l, wd), F32)] * 2,
        scratch_shapes=[vm(TOK_TILE), vm(TOK_TILE), vm(TOK_TILE), vm(TOK_TILE), vm(1), vm(1)],
        compiler_params=_cp("parallel", "arbitrary"),
        name="rglru_scan",
    )(rnn, rnn, rnn, rnn, rnn, rnn, w["cw"], w["cb"], w["gwf"], w["gbf"], w["lamf"], w["gwb"], w["gbb"], w["lamb"])


def _od_out_kernel(br_ref, yf_ref, yb_ref, wout_ref, h_ref, mod_ref, lng_ref, lnb_ref, o_ref, *, alpha):
    m = (_gelu(br_ref[0]) * (yf_ref[0] + yb_ref[0])).astype(BF16)
    mix = jnp.dot(m, wout_ref[...], preferred_element_type=F32)
    mod = mod_ref[0]
    o_ref[0] = _ln(alpha * h_ref[0] + mod[2:3] * mix, lng_ref[...], lnb_ref[...])


def _od_out(br, yf, yb, wout, h, modt, lng, lnb, nct, off, alpha):
    b, l, d = h.shape
    nt = l // TOK_TILE - off
    tok = lambda a: pl.BlockSpec((1, TOK_TILE, a.shape[2]), lambda bi, i: (bi, i + off, 0))
    full = lambda a: pl.BlockSpec(a.shape, lambda bi, i: (0,) * a.ndim)
    return pl.pallas_call(
        functools.partial(_od_out_kernel, alpha=alpha),
        grid=(b, nt),
        in_specs=[tok(br), tok(yf), tok(yb), full(wout), tok(h),
                  pl.BlockSpec((1, N_MOD, d), _mod_index(nct, off)), full(lng), full(lnb)],
        out_specs=pl.BlockSpec((1, TOK_TILE, d), lambda bi, i: (bi, i, 0)),
        out_shape=jax.ShapeDtypeStruct((b, nt * TOK_TILE, d), F32),
        compiler_params=_cp("parallel", "parallel"),
        name="od_out",
    )(br, yf, yb, wout, h, modt, lng, lnb)


def _top16_rows(s, vals_ref, idx_ref, base):
    nk = s.shape[0]
    rid = lax.broadcasted_iota(jnp.int32, s.shape, 0).astype(F32)
    for r in range(PEER_TOPK):
        m = jnp.max(s, axis=0, keepdims=True)
        pos = jnp.min(jnp.where(s == m, rid, float(nk)), axis=0, keepdims=True)
        vals_ref[base + r:base + r + 1, :] = m
        idx_ref[base + r:base + r + 1, :] = pos
        s = jnp.where(rid == pos, -jnp.inf, s)


def _cand_blocks(t1, t2, combine):
    blocks = [combine(t1[0:1], t2[0:16])]
    for a in range(1, 8):
        blocks.append(combine(t1[a:a + 1], t2[0:8]))
    blocks.append(combine(t1[8:16], t2[0:1]))
    return jnp.concatenate(blocks, axis=0)


def _cand_flat_index(ntok):
    a16 = lax.broadcasted_iota(jnp.int32, (16, ntok), 0)
    a8 = lax.broadcasted_iota(jnp.int32, (8, ntok), 0)
    blocks = [a16]
    for a in range(1, 8):
        blocks.append(a * PEER_TOPK + a8)
    blocks.append((a8 + 8) * PEER_TOPK)
    return jnp.concatenate(blocks, axis=0).astype(F32)


def _split_bf16(x):
    hi = x.astype(BF16)
    return hi, (x - hi.astype(F32)).astype(BF16)


def _dot3(a, b, dims):
    dg = lambda x, y: lax.dot_general(x, y, dims, preferred_element_type=F32)
    return dg(a[0], b[0]) + (dg(a[0], b[1]) + dg(a[1], b[0]))


def _peer_route_kernel(h_ref, mod_ref, wq_ref, k_ref, idx_ref, gate_ref, t1_ref, i1_ref, t2_ref, i2_ref, ts_ref, ti_ref):
    mod = mod_ref[0]
    hm = h_ref[0] * (1.0 + mod[4:5]) + mod[3:4]
    q = _dot3(_split_bf16(hm), (wq_ref[0], wq_ref[1]), (((1,), (0,)), ((), ())))
    ntok = hm.shape[0]
    flat = _cand_flat_index(ntok)
    nn = (((1,), (1,)), ((), ()))
    qs = _split_bf16(q)
    for h in range(PEER_HEADS):
        c1 = slice(h * PEER_QDIM, h * PEER_QDIM + PEER_HALF)
        c2 = slice(h * PEER_QDIM + PEER_HALF, (h + 1) * PEER_QDIM)
        s1 = _dot3((k_ref[0], k_ref[1]), (qs[0][:, c1], qs[1][:, c1]), nn)
        s2 = _dot3((k_ref[2], k_ref[3]), (qs[0][:, c2], qs[1][:, c2]), nn)
        _top16_rows(s1, t1_ref, i1_ref, 0)
        _top16_rows(s2, t2_ref, i2_ref, 0)
        cs = _cand_blocks(t1_ref[...], t2_ref[...], lambda x, y: x + y)
        ce = _cand_blocks(i1_ref[...], i2_ref[...], lambda x, y: x * PEER_NK + y)
        for r in range(PEER_TOPK):
            m = jnp.max(cs, axis=0, keepdims=True)
            pos = jnp.min(jnp.where(cs == m, flat, float(PEER_TOPK * PEER_TOPK)), axis=0, keepdims=True)
            hit = flat == pos
            row = h * PEER_TOPK + r
            ts_ref[row:row + 1, :] = m
            ti_ref[row:row + 1, :] = jnp.max(jnp.where(hit, ce, -1.0), axis=0, keepdims=True)
            cs = jnp.where(hit, -jnp.inf, cs)
        blk = slice(h * PEER_TOPK, (h + 1) * PEER_TOPK)
        top = ts_ref[blk, :]
        e = jnp.exp(top - top[0:1, :])
        ts_ref[blk, :] = e / jnp.sum(e, axis=0, keepdims=True)
    idx_ref[0] = ti_ref[...].T.astype(jnp.int32)
    gate_ref[0] = ts_ref[...].T


def _peer_route(h, modt, nct, off, wq, k1, k2):
    b, l, d = h.shape
    nt = l // TOK_TILE
    tok = lambda width: pl.BlockSpec((1, TOK_TILE, width), lambda bi, i: (bi, i, 0))
    full = lambda a: pl.BlockSpec(a.shape, lambda bi, i: (0,) * a.ndim)
    vmf = lambda rows: pltpu.VMEM((rows, TOK_TILE), F32)
    wq = jnp.stack(_split_bf16(wq))
    keys = jnp.stack(_split_bf16(k1) + _split_bf16(k2))
    return pl.pallas_call(
        _peer_route_kernel,
        grid=(b, nt),
        in_specs=[tok(d), pl.BlockSpec((1, N_MOD, d), _mod_index(nct, off)), full(wq), full(keys)],
        out_specs=[tok(PEER_PAIRS), tok(PEER_PAIRS)],
        out_shape=[jax.ShapeDtypeStruct((b, l, PEER_PAIRS), jnp.int32), jax.ShapeDtypeStruct((b, l, PEER_PAIRS), F32)],
        scratch_shapes=[vmf(PEER_TOPK), vmf(PEER_TOPK), vmf(PEER_TOPK), vmf(PEER_TOPK), vmf(PEER_PAIRS), vmf(PEER_PAIRS)],
        compiler_params=_cp("parallel", "parallel"),
        name="peer_route",
    )(h, modt, wq, keys)


def _peer_expert_kernel(idx_ref, idxn_ref, h_ref, mod_ref, gate_ref, lng_ref, lnb_ref, tab_ref, o_ref, buf0, buf1, sem,
                        *, alpha, nsteps):
    g = pl.program_id(0)
    npair = PEER_TOK * PEER_PAIRS
    bufs = (buf0, buf1)

    nchunk = h_ref.shape[1] // LANES

    def row_copy(e, half, p):
        return pltpu.make_async_copy(tab_ref.at[e], bufs[half].at[pl.ds(PEER_PITCH * p, nchunk), :], sem.at[half])

    def wait_all(half):
        done = bufs[half].at[pl.ds(0, npair * nchunk), :]
        pltpu.make_async_copy(done, done, sem.at[half]).wait()

    @pl.when(g == 0)
    def _():
        def body(p, carry):
            row_copy(idx_ref[0, 0, p], 0, p).start()
            return carry
        lax.fori_loop(0, npair, body, 0, unroll=8)

    mod = mod_ref[0]
    h = h_ref[...]
    hm = h * (1.0 + mod[4:5]) + mod[3:4]
    gates = gate_ref[...]
    ones = jnp.ones((LANES, LANES), BF16)
    eye = (lax.broadcasted_iota(jnp.int32, (LANES, LANES), 0) == lax.broadcasted_iota(jnp.int32, (LANES, LANES), 1))

    def row_sums(x):
        hi, lo = _split_bf16(x)
        return jnp.dot(hi, ones, preferred_element_type=F32) + jnp.dot(lo, ones, preferred_element_type=F32)

    def token(bufr, row, t):
        def chunk(c):
            return bufr[pl.ds(t * PEER_PAIRS * PEER_PITCH + c, PEER_PAIRS, stride=PEER_PITCH), :]
        part = jnp.zeros((PEER_PAIRS, LANES), F32)
        for c in range(nchunk):
            uf = lax.bitcast_convert_type(chunk(c) & jnp.uint32(0xFFFF0000), F32)
            part = part + uf * hm[row:row + 1, c * LANES:(c + 1) * LANES]
        act = row_sums(part)
        coef = _gelu(act) * row_sums(jnp.where(eye, gates[row:row + 1, :], 0.0))
        outs = []
        for c in range(nchunk):
            vf = lax.bitcast_convert_type(chunk(c) << 16, F32)
            outs.append(jnp.sum(coef * vf, axis=0, keepdims=True))
        return jnp.concatenate(outs, axis=1)

    rows = []
    for half in range(2):
        wait_all(half)
        nxt_ref, nxt_base = (idx_ref, npair) if half == 0 else (idxn_ref, 0)
        for t in range(PEER_TOK):
            for p in range(t * PEER_PAIRS, (t + 1) * PEER_PAIRS):
                row_copy(nxt_ref[0, 0, nxt_base + p], 1 - half, p).start(priority=p % 2)
            rows.append(token(bufs[half], half * PEER_TOK + t, t))
    f = jnp.concatenate(rows, axis=0)
    o_ref[...] = _ln(alpha * h + mod[5:6] * f, lng_ref[...], lnb_ref[...])

    @pl.when(g == nsteps - 1)
    def _():
        wait_all(0)


def _peer_expert(h, modt, nct, off, idx, gate, table, lng, lnb, alpha):
    b, l, d = h.shape
    n = b * l
    step_tok = 2 * PEER_TOK
    nsteps = n // step_tok
    per_b = l // step_tok
    tiles_per = TOK_TILE // step_tok
    npair = PEER_TOK * PEER_PAIRS
    h2 = h.reshape(n, d)
    idx3 = idx.reshape(nsteps, 1, 2 * npair)
    gate2 = gate.reshape(n, PEER_PAIRS)
    smem = lambda f: pl.BlockSpec((1, 1, 2 * npair), f, memory_space=pltpu.SMEM)
    mod_idx = _mod_index(nct, off)
    pair_buf = pltpu.VMEM((npair * PEER_PITCH, LANES), jnp.uint32)
    out = pl.pallas_call(
        functools.partial(_peer_expert_kernel, alpha=alpha, nsteps=nsteps),
        grid=(nsteps,),
        in_specs=[smem(lambda g: (g, 0, 0)), smem(lambda g: (jnp.minimum(g + 1, nsteps - 1), 0, 0)),
                  pl.BlockSpec((step_tok, d), lambda g: (g, 0)),
                  pl.BlockSpec((1, N_MOD, d), lambda g: mod_idx(g // per_b, (g % per_b) // tiles_per)),
                  pl.BlockSpec((step_tok, PEER_PAIRS), lambda g: (g, 0)),
                  pl.BlockSpec(lng.shape, lambda g: (0, 0)), pl.BlockSpec(lnb.shape, lambda g: (0, 0)),
                  pl.BlockSpec(memory_space=pl.ANY)],
        out_specs=pl.BlockSpec((step_tok, d), lambda g: (g, 0)),
        out_shape=jax.ShapeDtypeStruct((n, d), F32),
        scratch_shapes=[pair_buf, pair_buf, pltpu.SemaphoreType.DMA((2,))],
        compiler_params=_cp("arbitrary"),
        name="peer_expert",
    )(idx3, idx3, h2, modt, gate2, lng, lnb, table)
    return out.reshape(b, l, d)


def _peer_table(u, v):
    n, d = u.shape
    ub = lax.bitcast_convert_type(u.astype(BF16), jnp.uint16).astype(jnp.uint32)
    vb = lax.bitcast_convert_type(v.astype(BF16), jnp.uint16).astype(jnp.uint32)
    return ((ub << 16) | vb).reshape(n, d // LANES, LANES)


def _rope_tables(seq_len, ctx_len):
    rows = seq_len // GRID_W
    row = np.repeat(np.arange(rows, dtype=np.float32), GRID_W)
    col = np.tile(np.arange(GRID_W, dtype=np.float32), rows)
    n_freq = MLA_ROPE // 4
    inv = jnp.power(ROPE_BASE, -jnp.arange(n_freq, dtype=F32) / n_freq)
    ang = jnp.concatenate([row[:, None] * inv, col[:, None] * inv], axis=-1)
    ang = jnp.concatenate([jnp.zeros((ctx_len, MLA_ROPE // 2), F32), ang], axis=0)
    cos, sin = jnp.cos(ang), jnp.sin(ang)
    l = seq_len + ctx_len
    half = MLA_ROPE // 2
    c = jnp.ones((l, LANES), F32).at[:, MLA_NOPE:MLA_NOPE + half].set(cos).at[:, MLA_NOPE + half:MLA_NOPE + 2 * half].set(cos)
    s1 = jnp.zeros((l, LANES), F32).at[:, MLA_NOPE:MLA_NOPE + half].set(-sin)
    s2 = jnp.zeros((l, LANES), F32).at[:, MLA_NOPE + half:MLA_NOPE + 2 * half].set(sin)
    return c, s1, s2


def _ev_weights(w_in, q_norm, kv_norm, w_uq, w_ukv, wa2, ba, seq_len, ctx_len):
    d = w_in.shape[0]
    widths = (MLA_Q_RANK, MLA_KV_RANK, MLA_ROPE, GLA_HEADS * GLA_DK, GLA_HEADS * GLA_DK,
              GLA_HEADS * GLA_DV, GLA_HEADS * GLA_DV, GLA_GATE_RANK, GLA_GATE_RANK)
    offs = np.cumsum((0,) + widths)
    part = lambda i: w_in[:, offs[i]:offs[i + 1]]
    kpe = jnp.zeros((d, HEAD_PAD), F32).at[:, MLA_NOPE:MLA_NOPE + MLA_ROPE].set(part(2))
    gate = jnp.zeros((d, LANES), F32).at[:, :GLA_GATE_RANK].set(part(7)).at[:, GLA_GATE_RANK:2 * GLA_GATE_RANK].set(part(8))
    win = jnp.concatenate([part(0), part(1), kpe, part(3), part(4), part(5), part(6), gate], axis=1).astype(BF16)
    hq = MLA_NOPE + MLA_ROPE
    wuq = jnp.pad(w_uq.reshape(MLA_Q_RANK, MLA_HEADS, hq), ((0, 0), (0, 0), (0, HEAD_PAD - hq)))
    wuq = wuq.reshape(MLA_Q_RANK, MLA_HEADS * HEAD_PAD).astype(BF16)
    ukv = w_ukv.reshape(MLA_KV_RANK, MLA_HEADS, MLA_NOPE + MLA_V)
    wk = jnp.pad(ukv[:, :, :MLA_NOPE], ((0, 0), (0, 0), (0, HEAD_PAD - MLA_NOPE))).reshape(MLA_KV_RANK, MLA_HEADS * HEAD_PAD)
    wv = ukv[:, :, MLA_NOPE:].reshape(MLA_KV_RANK, MLA_HEADS * MLA_V)
    wukv = jnp.concatenate([wk, wv], axis=1).astype(BF16)
    gdk = GLA_HEADS * GLA_DK
    wa2p = jnp.zeros((LANES, 2 * gdk), F32).at[:GLA_GATE_RANK, :gdk].set(wa2[0])
    wa2p = wa2p.at[GLA_GATE_RANK:2 * GLA_GATE_RANK, gdk:].set(wa2[1]).astype(BF16)
    c, s1, s2 = _rope_tables(seq_len, ctx_len)
    return dict(win=win, qn=q_norm[None, :], kvn=kv_norm[None, :], wuq=wuq, wukv=wukv, wa2=wa2p,
                ba=ba.reshape(1, 2 * gdk), rope_c=c, rope_s1=s1, rope_s2=s2)


def _od_weights(conv_w, conv_b, gate_w, gate_b, lam):
    gw = lambda dd: jnp.concatenate([gate_w[dd, 0], gate_w[dd, 1]], axis=-1).astype(BF16)
    return dict(cw=conv_w, cb=conv_b[None, :], gwf=gw(0), gbf=gate_b[0], lamf=lam[0][None, :],
                gwb=gw(1), gbb=gate_b[1], lamb=lam[1][None, :])


def kernel(x, c, ctx, c_ctx, mod_w, mod_b, ln_g, ln_b,
           ev_w_in, ev_q_norm, ev_kv_norm, ev_w_uq, ev_w_ukv, ev_gla_wa2, ev_gla_ba, ev_gla_norm, ev_w_out,
           od_w_in, od_conv_w, od_conv_b, od_gate_w, od_gate_b, od_lambda, od_w_out,
           peer_wq, peer_k1, peer_k2, peer_u, peer_v):
    b, seq_len, d = x.shape
    ctx_len = ctx.shape[1]
    depth = mod_w.shape[0]
    assert ctx_len % TOK_TILE == 0 and seq_len % TOK_TILE == 0 and seq_len % GRID_W == 0
    alpha = (2 * depth) ** 0.25

    rows = -(-(b + 1) // SUBLANES) * SUBLANES
    cc = jnp.zeros((rows, d), F32).at[:b].set(c).at[b].set(c_ctx)

    h = jnp.concatenate([ctx, x], axis=1)
    nct = ctx_len // TOK_TILE
    off = 0
    for layer in range(depth):
        last = layer == depth - 1
        j = layer // 2
        mods = _mod_vectors(cc, mod_w[layer], mod_b[layer][None, :]).reshape(rows, N_MOD, d)
        modt = jnp.stack([jnp.broadcast_to(mods[b], (b, N_MOD, d)), mods[:b]], axis=1).reshape(2 * b, N_MOD, d)
        lng = ln_g[layer][:, None, :]
        lnb = ln_b[layer][:, None, :]
        assert off == 0
        if layer % 2 == 0:
            w = _ev_weights(ev_w_in[j], ev_q_norm[j], ev_kv_norm[j], ev_w_uq[j], ev_w_ukv[j],
                            ev_gla_wa2[j], ev_gla_ba[j], seq_len, ctx_len)
            q, k, v, gq, gk, gv, gg, laf, lab = _ev_proj(h, modt, nct, w)
            att = _attention(q, k, v, nct, ctx_len)
            o_f, o_b = _gla(gq, gk, gv, laf, lab, nct)
            assert not last, "the final layer drops the context rows only after an odd layer"
            h = _ev_out(att, o_f, o_b, gg, ev_gla_norm[j][None, :], ev_w_out[j].astype(BF16), h, modt,
                        lng[0], lnb[0], nct, alpha)
        else:
            w = _od_weights(od_conv_w[j], od_conv_b[j], od_gate_w[j], od_gate_b[j], od_lambda[j])
            br, rnn = _od_proj(h, modt, nct, od_w_in[j].astype(BF16))
            yf, yb = _rglru(rnn, nct, w)
            off = nct if last else 0
            h = _od_out(br, yf, yb, od_w_out[j].astype(BF16), h, modt, lng[0], lnb[0], nct, off, alpha)
        idx, gate = _peer_route(h, modt, nct, off, peer_wq[layer], peer_k1[layer], peer_k2[layer])
        table = _peer_table(peer_u[layer], peer_v[layer])
        h = _peer_expert(h, modt, nct, off, idx, gate, table, lng[1], lnb[1], alpha)
    return h if off else h[:, ctx_len:]
```

```python
import functools

import numpy as np
import jax
import jax.numpy as jnp
from jax import lax
from jax.experimental import pallas as pl
from jax.experimental.pallas import tpu as pltpu

F32 = jnp.float32
BF16 = jnp.bfloat16
HI = lax.Precision.HIGHEST

N_MOD = 6
LN_EPS = 1e-6
GRID_W = 64
MLA_HEADS = 8
MLA_Q_RANK = 384
MLA_KV_RANK = 256
MLA_NOPE = 64
MLA_ROPE = 32
MLA_V = 64
MLA_SCALE = (MLA_NOPE + MLA_ROPE) ** -0.5
ROPE_BASE = 10000.0
GLA_HEADS = 4
GLA_DK = 64
GLA_DV = 128
GLA_GATE_RANK = 16
GLA_TAU = 16.0
RG_WIDTH = 1280
RG_BLOCKS = 10
RG_BW = RG_WIDTH // RG_BLOCKS
RG_CONV = 4
RG_CONV_LEFT = 2
RG_C = 8.0
PEER_HEADS = 8
PEER_NK = 128
PEER_TOPK = 16
PEER_QDIM = 256
PEER_HALF = PEER_QDIM // 2

LANES = 128
SUBLANES = 8
TOK_TILE = 256
GLA_CHUNK = 64
GLA_SUB = 16
PEER_DENSE_TOK = 512
PEER_EXPERT_BLK = 512
VMEM_LIMIT = 56 * 1024 * 1024

HEAD_PAD = LANES


def _cp(*sem):
    return pltpu.CompilerParams(dimension_semantics=sem, vmem_limit_bytes=VMEM_LIMIT)


def _gelu(x):
    return 0.5 * x * (1.0 + lax.erf(x * (2.0 ** -0.5)))


def _ln(y, g, b):
    mu = jnp.mean(y, axis=-1, keepdims=True)
    d = y - mu
    var = jnp.mean(d * d, axis=-1, keepdims=True)
    return d * lax.rsqrt(var + LN_EPS) * g + b


def _mod_index(nct, off):
    def index(b, i):
        return (2 * b + jnp.where(i + off >= nct, 1, 0), 0, 0)
    return index


def _mod_kernel(c_ref, w_ref, b_ref, o_ref):
    x = c_ref[...]
    s = x * jax.nn.sigmoid(x)
    o_ref[...] = jnp.dot(s, w_ref[...], precision=HI, preferred_element_type=F32) + b_ref[...]


def _mod_vectors(cc, w, b):
    rows, d = cc.shape
    n = w.shape[1] // d
    return pl.pallas_call(
        _mod_kernel,
        grid=(n,),
        in_specs=[pl.BlockSpec((rows, d), lambda j: (0, 0)),
                  pl.BlockSpec((d, d), lambda j: (0, j)),
                  pl.BlockSpec((1, d), lambda j: (0, j))],
        out_specs=pl.BlockSpec((rows, d), lambda j: (0, j)),
        out_shape=jax.ShapeDtypeStruct((rows, w.shape[1]), F32),
        compiler_params=_cp("arbitrary"),
        name="mod_vectors",
    )(cc, w, b)


def _rope_block(x, c, s1, s2):
    return x * c + pltpu.roll(x, LANES - MLA_ROPE // 2, axis=1) * s1 + pltpu.roll(x, MLA_ROPE // 2, axis=1) * s2


def _ev_proj_kernel(h_ref, mod_ref, win_ref, qn_ref, kvn_ref, wuq_ref, wukv_ref, wa2_ref, ba_ref,
                    rc_ref, rs1_ref, rs2_ref,
                    q_ref, k_ref, v_ref, gq_ref, gk_ref, gv_ref, gg_ref, laf_ref, lab_ref):
    x = h_ref[0]
    mod = mod_ref[0]
    xin = (x * (1.0 + mod[1:2]) + mod[0:1]).astype(BF16)
    p = jnp.dot(xin, win_ref[...], preferred_element_type=F32)
    o = 0
    cq = p[:, o:o + MLA_Q_RANK]; o += MLA_Q_RANK
    ckv = p[:, o:o + MLA_KV_RANK]; o += MLA_KV_RANK
    kpe = p[:, o:o + HEAD_PAD]; o += HEAD_PAD
    gdk = GLA_HEADS * GLA_DK
    gdv = GLA_HEADS * GLA_DV
    gq_ref[0] = p[:, o:o + gdk] * (GLA_DK ** -0.5); o += gdk
    gk_ref[0] = p[:, o:o + gdk]; o += gdk
    gv_ref[0] = p[:, o:o + gdv]; o += gdv
    gg_ref[0] = p[:, o:o + gdv]; o += gdv
    ga = p[:, o:o + LANES]

    c, s1, s2 = rc_ref[...], rs1_ref[...], rs2_ref[...]
    nq = cq * lax.rsqrt(jnp.mean(cq * cq, axis=-1, keepdims=True) + LN_EPS) * qn_ref[...]
    q = jnp.dot(nq.astype(BF16), wuq_ref[...], preferred_element_type=F32)
    nkv = ckv * lax.rsqrt(jnp.mean(ckv * ckv, axis=-1, keepdims=True) + LN_EPS) * kvn_ref[...]
    kv = jnp.dot(nkv.astype(BF16), wukv_ref[...], preferred_element_type=F32)
    kpe_r = _rope_block(kpe, c, s1, s2)
    for h in range(MLA_HEADS):
        sl = slice(h * HEAD_PAD, (h + 1) * HEAD_PAD)
        q_ref[0, :, sl] = _rope_block(q[:, sl], c, s1, s2).astype(BF16)
        k_ref[0, :, sl] = (kv[:, sl] + kpe_r).astype(BF16)
    v_ref[0] = kv[:, MLA_HEADS * HEAD_PAD:].astype(BF16)

    z = jnp.dot(ga.astype(BF16), wa2_ref[...], preferred_element_type=F32) + ba_ref[...]
    la = jax.nn.log_sigmoid(z) * (1.0 / GLA_TAU)
    laf_ref[0] = la[:, :gdk]
    lab_ref[0] = la[:, gdk:]


def _ev_proj(h, modt, nct, w):
    b, l, d = h.shape
    nt = l // TOK_TILE
    tok = lambda width: pl.BlockSpec((1, TOK_TILE, width), lambda bi, i: (bi, i, 0))
    full = lambda a: pl.BlockSpec(a.shape, lambda bi, i: (0,) * a.ndim)
    rope = pl.BlockSpec((TOK_TILE, LANES), lambda bi, i: (i, 0))
    gdk = GLA_HEADS * GLA_DK
    gdv = GLA_HEADS * GLA_DV
    widths = [(MLA_HEADS * HEAD_PAD, BF16), (MLA_HEADS * HEAD_PAD, BF16), (MLA_HEADS * MLA_V, BF16),
              (gdk, F32), (gdk, F32), (gdv, F32), (gdv, F32), (gdk, F32), (gdk, F32)]
    return pl.pallas_call(
        _ev_proj_kernel,
        grid=(b, nt),
        in_specs=[tok(d), pl.BlockSpec((1, N_MOD, d), _mod_index(nct, 0)),
                  full(w["win"]), full(w["qn"]), full(w["kvn"]), full(w["wuq"]), full(w["wukv"]),
                  full(w["wa2"]), full(w["ba"]), rope, rope, rope],
        out_specs=[tok(wd) for wd, _ in widths],
        out_shape=[jax.ShapeDtypeStruct((b, l, wd), dt) for wd, dt in widths],
        compiler_params=_cp("parallel", "parallel"),
        name="ev_proj",
    )(h, modt, w["win"], w["qn"], w["kvn"], w["wuq"], w["wukv"], w["wa2"], w["ba"],
      w["rope_c"], w["rope_s1"], w["rope_s2"])


def _attn_kernel(q_ref, k_ref, v_ref, o_ref, *, nct, ctx_len):
    def attend(nk):
        lane = lax.broadcasted_iota(jnp.int32, (TOK_TILE, LANES), 1)
        for hp in range(MLA_HEADS // 2):
            vp = v_ref[0, 0:nk, hp * LANES:(hp + 1) * LANES]
            outs = []
            for hh in range(2):
                h = 2 * hp + hh
                qh = q_ref[0, :, h * HEAD_PAD:(h + 1) * HEAD_PAD]
                kh = k_ref[0, 0:nk, h * HEAD_PAD:(h + 1) * HEAD_PAD]
                s = lax.dot_general(qh, kh, (((1,), (1,)), ((), ())), preferred_element_type=F32)
                m = jnp.max(s, axis=-1, keepdims=True)
                p = jnp.exp((s - m) * MLA_SCALE)
                den = jnp.sum(p, axis=-1, keepdims=True)
                o = jnp.dot(p.astype(BF16), vp, preferred_element_type=F32)
                outs.append(o / den)
            o_ref[0, :, hp * LANES:(hp + 1) * LANES] = jnp.where(lane < MLA_V, outs[0], outs[1]).astype(BF16)

    is_ctx = pl.program_id(1) < nct

    @pl.when(is_ctx)
    def _():
        attend(ctx_len)

    @pl.when(jnp.logical_not(is_ctx))
    def _():
        attend(k_ref.shape[1])


def _attention(q, k, v, nct, ctx_len):
    b, l, _ = q.shape
    nt = l // TOK_TILE
    return pl.pallas_call(
        functools.partial(_attn_kernel, nct=nct, ctx_len=ctx_len),
        grid=(b, nt),
        in_specs=[pl.BlockSpec((1, TOK_TILE, q.shape[2]), lambda bi, i: (bi, i, 0)),
                  pl.BlockSpec((1, l, k.shape[2]), lambda bi, i: (bi, 0, 0)),
                  pl.BlockSpec((1, l, v.shape[2]), lambda bi, i: (bi, 0, 0))],
        out_specs=pl.BlockSpec((1, TOK_TILE, v.shape[2]), lambda bi, i: (bi, i, 0)),
        out_shape=jax.ShapeDtypeStruct((b, l, v.shape[2]), BF16),
        compiler_params=_cp("parallel", "arbitrary"),
        name="mla_attention",
    )(q, k, v)


def _bwd_tile(s, nct, nt):
    return jnp.where(s < nct, nct - 1 - s, nt - 1 - (s - nct))


def _gla_chunk(q, k, v, la, st, rev):
    c = GLA_CHUNK
    nsub = c // GLA_SUB
    row = lax.broadcasted_iota(jnp.int32, (c, c), 0)
    col = lax.broadcasted_iota(jnp.int32, (c, c), 1)
    tri = (col >= row) if rev else (col <= row)
    cum = jnp.dot(tri.astype(F32), la, precision=HI, preferred_element_type=F32)
    rid = lax.broadcasted_iota(jnp.int32, (c, LANES), 0)
    lane = lax.broadcasted_iota(jnp.int32, (c, LANES), 1)
    head0 = lane < GLA_DK

    qparts, kparts, eref = [], [], jnp.zeros((c, LANES), F32)
    for jb in range(nsub):
        r = jb * GLA_SUB if rev else jb * GLA_SUB + GLA_SUB - 1
        ej = cum[r:r + 1, :]
        valid = (rid <= jb * GLA_SUB + GLA_SUB - 1) if rev else (rid >= jb * GLA_SUB)
        qparts.append(jnp.where(valid, q * jnp.exp(jnp.where(valid, cum - ej, 0.0)), 0.0))
        inblk = (rid >= jb * GLA_SUB) & (rid < (jb + 1) * GLA_SUB)
        eref = jnp.where(inblk, ej, eref)
    kt = k * jnp.exp(eref - cum)
    for jb in range(nsub):
        inblk = (rid >= jb * GLA_SUB) & (rid < (jb + 1) * GLA_SUB)
        kparts.append(jnp.where(inblk, kt, 0.0))
    qbig = jnp.concatenate(qparts, axis=1)
    kbig = jnp.concatenate(kparts, axis=1).astype(BF16)
    h0big = lax.broadcasted_iota(jnp.int32, (c, nsub * LANES), 1) % LANES < GLA_DK
    qst = jnp.concatenate([jnp.where(h0big, qbig, 0.0), jnp.where(h0big, 0.0, qbig)], axis=0).astype(BF16)
    sc = lax.dot_general(qst, kbig, (((1,), (1,)), ((), ())), preferred_element_type=F32)
    ri = lax.broadcasted_iota(jnp.int32, (2 * c, c), 0) % c
    ci = lax.broadcasted_iota(jnp.int32, (2 * c, c), 1)
    sc = jnp.where((ci >= ri) if rev else (ci <= ri), sc, 0.0)
    vb = v.astype(BF16)
    intra = jnp.dot(sc.astype(BF16), vb, preferred_element_type=F32)
    qs = (q * jnp.exp(cum)).astype(BF16)
    inter = lax.dot_general(qs, st.astype(BF16), (((1,), (1,)), ((), ())), preferred_element_type=F32)
    out = jnp.concatenate([intra[0:c, 0:GLA_DV] + inter[:, 0:GLA_DV],
                           intra[c:2 * c, GLA_DV:2 * GLA_DV] + inter[:, GLA_DV:2 * GLA_DV]], axis=1)
    last = cum[0:1, :] if rev else cum[c - 1:c, :]
    kd = (k * jnp.exp(last - cum)).astype(BF16)
    upd = lax.dot_general(vb, kd, (((0,), (0,)), ((), ())), preferred_element_type=F32)
    srow = lax.broadcasted_iota(jnp.int32, (2 * GLA_DV, LANES), 0)
    slane = lax.broadcasted_iota(jnp.int32, (2 * GLA_DV, LANES), 1)
    diag = (srow < GLA_DV) == (slane < GLA_DK)
    st_new = jnp.where(diag, st * jnp.exp(last) + upd, 0.0)
    return out, st_new


def _gla_kernel(qf_ref, kf_ref, vf_ref, laf_ref, qb_ref, kb_ref, vb_ref, lab_ref, of_ref, ob_ref, stf_ref, stb_ref):
    @pl.when(pl.program_id(1) == 0)
    def _():
        stf_ref[...] = jnp.zeros_like(stf_ref)
        stb_ref[...] = jnp.zeros_like(stb_ref)

    nchunk = TOK_TILE // GLA_CHUNK
    npair = GLA_HEADS // 2

    def body(ci, carry):
        rf = pl.multiple_of(ci * GLA_CHUNK, GLA_CHUNK)
        rb = pl.multiple_of((nchunk - 1 - ci) * GLA_CHUNK, GLA_CHUNK)
        for pr in range(npair):
            ks = slice(pr * LANES, (pr + 1) * LANES)
            vs = slice(pr * 2 * GLA_DV, (pr + 1) * 2 * GLA_DV)
            o, st = _gla_chunk(qf_ref[0, pl.ds(rf, GLA_CHUNK), ks], kf_ref[0, pl.ds(rf, GLA_CHUNK), ks],
                               vf_ref[0, pl.ds(rf, GLA_CHUNK), vs], laf_ref[0, pl.ds(rf, GLA_CHUNK), ks],
                               stf_ref[pr], False)
            of_ref[0, pl.ds(rf, GLA_CHUNK), vs] = o
            stf_ref[pr] = st
            o, st = _gla_chunk(qb_ref[0, pl.ds(rb, GLA_CHUNK), ks], kb_ref[0, pl.ds(rb, GLA_CHUNK), ks],
                               vb_ref[0, pl.ds(rb, GLA_CHUNK), vs], lab_ref[0, pl.ds(rb, GLA_CHUNK), ks],
                               stb_ref[pr], True)
            ob_ref[0, pl.ds(rb, GLA_CHUNK), vs] = o
            stb_ref[pr] = st
        return carry

    lax.fori_loop(0, nchunk, body, 0)


def _gla(gq, gk, gv, laf, lab, nct):
    b, l, dk = gq.shape
    dv = gv.shape[2]
    nt = l // TOK_TILE
    fwd = lambda width: pl.BlockSpec((1, TOK_TILE, width), lambda bi, s: (bi, s, 0))
    bwd = lambda width: pl.BlockSpec((1, TOK_TILE, width), lambda bi, s: (bi, _bwd_tile(s, nct, nt), 0))
    npair = GLA_HEADS // 2
    return pl.pallas_call(
        _gla_kernel,
        grid=(b, nt),
        in_specs=[fwd(dk), fwd(dk), fwd(dv), fwd(dk), bwd(dk), bwd(dk), bwd(dv), bwd(dk)],
        out_specs=[fwd(dv), bwd(dv)],
        out_shape=[jax.ShapeDtypeStruct((b, l, dv), F32)] * 2,
        scratch_shapes=[pltpu.VMEM((npair, 2 * GLA_DV, LANES), F32)] * 2,
        compiler_params=_cp("parallel", "arbitrary"),
        name="gla_scan",
    )(gq, gk, gv, laf, gq, gk, gv, lab)


def _ev_out_kernel(att_ref, of_ref, ob_ref, gg_ref, gn_ref, wout_ref, h_ref, mod_ref, lng_ref, lnb_ref, o_ref, *, alpha):
    o = of_ref[0] + ob_ref[0]
    g = gg_ref[0]
    parts = []
    for hd in range(GLA_HEADS):
        sl = slice(hd * GLA_DV, (hd + 1) * GLA_DV)
        oh = o[:, sl]
        y = oh * lax.rsqrt(jnp.mean(oh * oh, axis=-1, keepdims=True) + LN_EPS) * gn_ref[:, sl]
        gh = g[:, sl]
        parts.append((y * (gh * jax.nn.sigmoid(gh))).astype(BF16))
    gla = jnp.concatenate(parts, axis=1)
    na = att_ref.shape[2]
    mix = (jnp.dot(att_ref[0], wout_ref[0:na, :], preferred_element_type=F32)
           + jnp.dot(gla, wout_ref[na:, :], preferred_element_type=F32))
    mod = mod_ref[0]
    o_ref[0] = _ln(alpha * h_ref[0] + mod[2:3] * mix, lng_ref[...], lnb_ref[...])


def _ev_out(att, o_f, o_b, gg, gn, wout, h, modt, lng, lnb, nct, alpha):
    b, l, d = h.shape
    nt = l // TOK_TILE
    tok = lambda a: pl.BlockSpec((1, TOK_TILE, a.shape[2]), lambda bi, i: (bi, i, 0))
    full = lambda a: pl.BlockSpec(a.shape, lambda bi, i: (0,) * a.ndim)
    return pl.pallas_call(
        functools.partial(_ev_out_kernel, alpha=alpha),
        grid=(b, nt),
        in_specs=[tok(att), tok(o_f), tok(o_b), tok(gg), full(gn), full(wout), tok(h),
                  pl.BlockSpec((1, N_MOD, d), _mod_index(nct, 0)), full(lng), full(lnb)],
        out_specs=tok(h),
        out_shape=jax.ShapeDtypeStruct((b, l, d), F32),
        compiler_params=_cp("parallel", "parallel"),
        name="ev_out",
    )(att, o_f, o_b, gg, gn, wout, h, modt, lng, lnb)


def _od_proj_kernel(h_ref, mod_ref, win_ref, br_ref, rnn_ref):
    mod = mod_ref[0]
    xin = (h_ref[0] * (1.0 + mod[1:2]) + mod[0:1]).astype(BF16)
    p = jnp.dot(xin, win_ref[...], preferred_element_type=F32)
    br_ref[0] = p[:, :RG_WIDTH]
    rnn_ref[0] = p[:, RG_WIDTH:]


def _od_proj(h, modt, nct, win):
    b, l, d = h.shape
    nt = l // TOK_TILE
    tok = lambda width: pl.BlockSpec((1, TOK_TILE, width), lambda bi, i: (bi, i, 0))
    return pl.pallas_call(
        _od_proj_kernel,
        grid=(b, nt),
        in_specs=[tok(d), pl.BlockSpec((1, N_MOD, d), _mod_index(nct, 0)),
                  pl.BlockSpec(win.shape, lambda bi, i: (0, 0))],
        out_specs=[tok(RG_WIDTH), tok(RG_WIDTH)],
        out_shape=[jax.ShapeDtypeStruct((b, l, RG_WIDTH), F32)] * 2,
        compiler_params=_cp("parallel", "parallel"),
        name="od_proj",
    )(h, modt, win)


def _rg_prepare(z_ref, pv_ref, nx_ref, prev_zero, next_zero, cw_ref, cb_ref, gw_ref, gb_ref, lam_ref, a_ref, x_ref):
    t = TOK_TILE
    z = z_ref[0]
    pv = jnp.where(prev_zero, 0.0, pv_ref[0])
    nx = jnp.where(next_zero, 0.0, nx_ref[0])
    ext = jnp.concatenate([pv, z, nx], axis=0)
    base = SUBLANES - RG_CONV_LEFT
    y = cb_ref[...] + ext[base:base + t] * cw_ref[0:1, :]
    for tap in range(1, RG_CONV):
        y = y + ext[base + tap:base + tap + t] * cw_ref[tap:tap + 1, :]
    for n in range(RG_BLOCKS):
        sl = slice(n * RG_BW, (n + 1) * RG_BW)
        yb = y[:, sl]
        ri = jnp.dot(yb.astype(BF16), gw_ref[n], preferred_element_type=F32)
        r = jax.nn.sigmoid(ri[:, :RG_BW] + gb_ref[0:1, sl])
        i = jax.nn.sigmoid(ri[:, RG_BW:] + gb_ref[1:2, sl])
        log_a = (-RG_C) * jax.nn.softplus(-lam_ref[:, sl]) * r
        th = jnp.tanh(log_a)
        a_ref[:, sl] = jnp.exp(log_a)
        x_ref[:, sl] = jnp.sqrt(-2.0 * th / (1.0 - th)) * (i * yb)


def _rglru_kernel(zf_ref, pf_ref, nf_ref, zb_ref, pb_ref, nb_ref, cw_ref, cb_ref,
                  gwf_ref, gbf_ref, lamf_ref, gwb_ref, gbb_ref, lamb_ref,
                  yf_ref, yb_ref, af_ref, xf_ref, ab_ref, xb_ref, hf_ref, hb_ref, *, nct, nt):
    s = pl.program_id(1)

    @pl.when(s == 0)
    def _():
        hf_ref[...] = jnp.zeros_like(hf_ref)
        hb_ref[...] = jnp.zeros_like(hb_ref)

    def edges(j):
        return (j == 0) | (j == nct), (j == nct - 1) | (j == nt - 1)

    pz, nz = edges(s)
    _rg_prepare(zf_ref, pf_ref, nf_ref, pz, nz, cw_ref, cb_ref, gwf_ref, gbf_ref, lamf_ref, af_ref, xf_ref)
    pz, nz = edges(_bwd_tile(s, nct, nt))
    _rg_prepare(zb_ref, pb_ref, nb_ref, pz, nz, cw_ref, cb_ref, gwb_ref, gbb_ref, lamb_ref, ab_ref, xb_ref)

    def body(kk, carry):
        hf, hb = carry
        tf = kk
        tb = TOK_TILE - 1 - kk
        hf = af_ref[pl.ds(tf, 1), :] * hf + xf_ref[pl.ds(tf, 1), :]
        yf_ref[0, pl.ds(tf, 1), :] = hf
        hb = ab_ref[pl.ds(tb, 1), :] * hb + xb_ref[pl.ds(tb, 1), :]
        yb_ref[0, pl.ds(tb, 1), :] = hb
        return hf, hb

    hf, hb = lax.fori_loop(0, TOK_TILE, body, (hf_ref[...], hb_ref[...]), unroll=8)
    hf_ref[...] = hf
    hb_ref[...] = hb


def _rglru(rnn, nct, w):
    b, l, wd = rnn.shape
    nt = l // TOK_TILE
    per = TOK_TILE // SUBLANES
    nh = l // SUBLANES
    bt = lambda s: _bwd_tile(s, nct, nt)
    tile = lambda f: pl.BlockSpec((1, TOK_TILE, wd), lambda bi, s: (bi, f(s), 0))
    prev = lambda f: pl.BlockSpec((1, SUBLANES, wd), lambda bi, s: (bi, jnp.maximum(f(s) * per - 1, 0), 0))
    nxt = lambda f: pl.BlockSpec((1, SUBLANES, wd), lambda bi, s: (bi, jnp.minimum((f(s) + 1) * per, nh - 1), 0))
    full = lambda a: pl.BlockSpec(a.shape, lambda bi, s: (0,) * a.ndim)
    ident = lambda s: s
    vm = lambda rows: pltpu.VMEM((rows, wd), F32)
    return pl.pallas_call(
        functools.partial(_rglru_kernel, nct=nct, nt=nt),
        grid=(b, nt),
        in_specs=[tile(ident), prev(ident), nxt(ident), tile(bt), prev(bt), nxt(bt),
                  full(w["cw"]), full(w["cb"]), full(w["gwf"]), full(w["gbf"]), full(w["lamf"]),
                  full(w["gwb"]), full(w["gbb"]), full(w["lamb"])],
        out_specs=[tile(ident), tile(bt)],
        out_shape=[jax.ShapeDtypeStruct((b, l, wd), F32)] * 2,
        scratch_shapes=[vm(TOK_TILE), vm(TOK_TILE), vm(TOK_TILE), vm(TOK_TILE), vm(1), vm(1)],
        compiler_params=_cp("parallel", "arbitrary"),
        name="rglru_scan",
    )(rnn, rnn, rnn, rnn, rnn, rnn, w["cw"], w["cb"], w["gwf"], w["gbf"], w["lamf"], w["gwb"], w["gbb"], w["lamb"])


def _od_out_kernel(br_ref, yf_ref, yb_ref, wout_ref, h_ref, mod_ref, lng_ref, lnb_ref, o_ref, *, alpha):
    m = (_gelu(br_ref[0]) * (yf_ref[0] + yb_ref[0])).astype(BF16)
    mix = jnp.dot(m, wout_ref[...], preferred_element_type=F32)
    mod = mod_ref[0]
    o_ref[0] = _ln(alpha * h_ref[0] + mod[2:3] * mix, lng_ref[...], lnb_ref[...])


def _od_out(br, yf, yb, wout, h, modt, lng, lnb, nct, off, alpha):
    b, l, d = h.shape
    nt = l // TOK_TILE - off
    tok = lambda a: pl.BlockSpec((1, TOK_TILE, a.shape[2]), lambda bi, i: (bi, i + off, 0))
    full = lambda a: pl.BlockSpec(a.shape, lambda bi, i: (0,) * a.ndim)
    return pl.pallas_call(
        functools.partial(_od_out_kernel, alpha=alpha),
        grid=(b, nt),
        in_specs=[tok(br), tok(yf), tok(yb), full(wout), tok(h),
                  pl.BlockSpec((1, N_MOD, d), _mod_index(nct, off)), full(lng), full(lnb)],
        out_specs=pl.BlockSpec((1, TOK_TILE, d), lambda bi, i: (bi, i, 0)),
        out_shape=jax.ShapeDtypeStruct((b, nt * TOK_TILE, d), F32),
        compiler_params=_cp("parallel", "parallel"),
        name="od_out",
    )(br, yf, yb, wout, h, modt, lng, lnb)


def _top16_rows(s, vals_ref):
    nk = s.shape[0]
    rid = lax.broadcasted_iota(jnp.int32, s.shape, 0).astype(F32)
    rank = jnp.full(s.shape, float(PEER_TOPK), F32)
    ex = jnp.zeros(s.shape, F32)
    best = None
    for r in range(PEER_TOPK):
        m = jnp.max(s, axis=0, keepdims=True)
        pos = jnp.min(jnp.where(s == m, rid, float(nk)), axis=0, keepdims=True)
        hit = rid == pos
        best = m if r == 0 else best
        vals_ref[r:r + 1, :] = m
        rank = jnp.where(hit, float(r), rank)
        ex = jnp.where(hit, jnp.exp(m - best), ex)
        s = jnp.where(hit, -jnp.inf, s)
    return rank, ex


def _cand_sums(t1, t2):
    blocks = [t1[0:1] + t2[0:16]]
    for a in range(1, 8):
        blocks.append(t1[a:a + 1] + t2[0:8])
    blocks.append(t1[8:16] + t2[0:1])
    return jnp.concatenate(blocks, axis=0)


def _selected_counts(t1, t2, flat):
    cs0 = _cand_sums(t1, t2)
    cs = cs0
    sel = jnp.zeros(cs.shape, F32)
    for _ in range(PEER_TOPK):
        m = jnp.max(cs, axis=0, keepdims=True)
        pos = jnp.min(jnp.where(cs == m, flat, float(PEER_TOPK * PEER_TOPK)), axis=0, keepdims=True)
        hit = flat == pos
        sel = jnp.where(hit, 1.0, sel)
        cs = jnp.where(hit, -jnp.inf, cs)
    z = jnp.sum(sel * jnp.exp(cs0 - cs0[0:1]), axis=0, keepdims=True)
    counts = [jnp.sum(sel[0:16], axis=0, keepdims=True)]
    for a in range(1, 8):
        counts.append(jnp.sum(sel[8 + 8 * a:16 + 8 * a], axis=0, keepdims=True))
    for a in range(8, 16):
        counts.append(sel[64 + a:65 + a])
    return counts, z


def _cand_flat_index(ntok):
    a16 = lax.broadcasted_iota(jnp.int32, (16, ntok), 0)
    a8 = lax.broadcasted_iota(jnp.int32, (8, ntok), 0)
    blocks = [a16]
    for a in range(1, 8):
        blocks.append(a * PEER_TOPK + a8)
    blocks.append((a8 + 8) * PEER_TOPK)
    return jnp.concatenate(blocks, axis=0).astype(F32)


def _split_bf16(x):
    hi = x.astype(BF16)
    return hi, (x - hi.astype(F32)).astype(BF16)


def _dot3(a, b, dims):
    dg = lambda x, y: lax.dot_general(x, y, dims, preferred_element_type=F32)
    return dg(a[0], b[0]) + (dg(a[0], b[1]) + dg(a[1], b[0]))


def _peer_route_kernel(h_ref, mod_ref, wq_ref, k_ref, hmt_ref, n1_ref, e1_ref, r2_ref, e2_ref, t1_ref, t2_ref):
    mod = mod_ref[0]
    hm = h_ref[0] * (1.0 + mod[4:5]) + mod[3:4]
    hmt_ref[...] = hm.T.astype(BF16)
    q = _dot3(_split_bf16(hm), (wq_ref[0], wq_ref[1]), (((1,), (0,)), ((), ())))
    flat = _cand_flat_index(hm.shape[0])
    nn = (((1,), (1,)), ((), ()))
    qs = _split_bf16(q)
    for h in range(PEER_HEADS):
        c1 = slice(h * PEER_QDIM, h * PEER_QDIM + PEER_HALF)
        c2 = slice(h * PEER_QDIM + PEER_HALF, (h + 1) * PEER_QDIM)
        s1 = _dot3((k_ref[0], k_ref[1]), (qs[0][:, c1], qs[1][:, c1]), nn)
        s2 = _dot3((k_ref[2], k_ref[3]), (qs[0][:, c2], qs[1][:, c2]), nn)
        r1, e1 = _top16_rows(s1, t1_ref)
        r2, e2 = _top16_rows(s2, t2_ref)
        counts, z = _selected_counts(t1_ref[...], t2_ref[...], flat)
        n1 = jnp.zeros(r1.shape, F32)
        for a in range(PEER_TOPK):
            n1 = jnp.where(r1 == float(a), counts[a], n1)
        n1_ref[h] = n1
        e1_ref[h] = e1 / z
        r2_ref[h] = r2
        e2_ref[h] = e2


def _peer_route(h, modt, nct, off, wq, k1, k2):
    b, l, d = h.shape
    nt = l // TOK_TILE
    ntok = b * l
    full = lambda a: pl.BlockSpec(a.shape, lambda bi, i: (0,) * a.ndim)
    vmf = lambda rows: pltpu.VMEM((rows, TOK_TILE), F32)
    wq = jnp.stack(_split_bf16(wq))
    keys = jnp.stack(_split_bf16(k1) + _split_bf16(k2))
    per_key = pl.BlockSpec((PEER_HEADS, PEER_NK, TOK_TILE), lambda bi, i: (0, 0, bi * nt + i))
    per_key_shape = jax.ShapeDtypeStruct((PEER_HEADS, PEER_NK, ntok), F32)
    return pl.pallas_call(
        _peer_route_kernel,
        grid=(b, nt),
        in_specs=[pl.BlockSpec((1, TOK_TILE, d), lambda bi, i: (bi, i, 0)),
                  pl.BlockSpec((1, N_MOD, d), _mod_index(nct, off)), full(wq), full(keys)],
        out_specs=[pl.BlockSpec((d, TOK_TILE), lambda bi, i: (0, bi * nt + i))] + [per_key] * 4,
        out_shape=[jax.ShapeDtypeStruct((d, ntok), BF16)] + [per_key_shape] * 4,
        scratch_shapes=[vmf(PEER_TOPK), vmf(PEER_TOPK)],
        compiler_params=_cp("parallel", "parallel"),
        name="peer_route",
    )(h, modt, wq, keys)


def _peer_dense_kernel(hmt_ref, u_ref, vt_ref, n1_ref, e1_ref, r2_ref, e2_ref, o_ref, acc_ref, w_ref):
    j = pl.program_id(1)

    @pl.when(j == 0)
    def _():
        acc_ref[...] = jnp.zeros_like(acc_ref)

    hmt = hmt_ref[...]
    per_step = u_ref.shape[0] // PEER_NK
    for al in range(per_step):
        a = j * per_step + al
        rows = slice(al * PEER_NK, (al + 1) * PEER_NK)
        s = jnp.dot(u_ref[rows, :], hmt, preferred_element_type=F32)
        g = jnp.zeros(s.shape, F32)
        for h in range(PEER_HEADS):
            n1 = n1_ref[h, pl.ds(a, 1), :]
            e1 = e1_ref[h, pl.ds(a, 1), :]
            g = g + jnp.where(r2_ref[h] < n1, e2_ref[h] * e1, 0.0)
        w_ref[rows, :] = (_gelu(s) * g).astype(BF16)
    acc_ref[...] += jnp.dot(vt_ref[...], w_ref[...], preferred_element_type=F32)

    @pl.when(j == pl.num_programs(1) - 1)
    def _():
        o_ref[...] = acc_ref[...].T


def _peer_dense(hmt, n1, e1, r2, e2, u, v):
    d, ntok = hmt.shape
    nexp = u.shape[0]
    tt = PEER_DENSE_TOK if ntok % PEER_DENSE_TOK == 0 else TOK_TILE
    ub = u.astype(BF16)
    vt = v.T.astype(BF16)
    per_key = pl.BlockSpec((PEER_HEADS, PEER_NK, tt), lambda i, j: (0, 0, i))
    return pl.pallas_call(
        _peer_dense_kernel,
        grid=(ntok // tt, nexp // PEER_EXPERT_BLK),
        in_specs=[pl.BlockSpec((d, tt), lambda i, j: (0, i)),
                  pl.BlockSpec((PEER_EXPERT_BLK, d), lambda i, j: (j, 0)),
                  pl.BlockSpec((d, PEER_EXPERT_BLK), lambda i, j: (0, j)),
                  per_key, per_key, per_key, per_key],
        out_specs=pl.BlockSpec((tt, d), lambda i, j: (i, 0)),
        out_shape=jax.ShapeDtypeStruct((ntok, d), F32),
        scratch_shapes=[pltpu.VMEM((d, tt), F32), pltpu.VMEM((PEER_EXPERT_BLK, tt), BF16)],
        compiler_params=_cp("parallel", "arbitrary"),
        name="peer_dense",
    )(hmt, ub, vt, n1, e1, r2, e2)


def _peer_res_kernel(h_ref, f_ref, mod_ref, lng_ref, lnb_ref, o_ref, *, alpha):
    mod = mod_ref[0]
    o_ref[0] = _ln(alpha * h_ref[0] + mod[5:6] * f_ref[0], lng_ref[...], lnb_ref[...])


def _peer_res(h, f, modt, nct, off, lng, lnb, alpha):
    b, l, d = h.shape
    tok = pl.BlockSpec((1, TOK_TILE, d), lambda bi, i: (bi, i, 0))
    full = lambda a: pl.BlockSpec(a.shape, lambda bi, i: (0,) * a.ndim)
    return pl.pallas_call(
        functools.partial(_peer_res_kernel, alpha=alpha),
        grid=(b, l // TOK_TILE),
        in_specs=[tok, tok, pl.BlockSpec((1, N_MOD, d), _mod_index(nct, off)), full(lng), full(lnb)],
        out_specs=tok,
        out_shape=jax.ShapeDtypeStruct((b, l, d), F32),
        compiler_params=_cp("parallel", "parallel"),
        name="peer_res",
    )(h, f.reshape(b, l, d), modt, lng, lnb)


def _rope_tables(seq_len, ctx_len):
    rows = seq_len // GRID_W
    row = np.repeat(np.arange(rows, dtype=np.float32), GRID_W)
    col = np.tile(np.arange(GRID_W, dtype=np.float32), rows)
    n_freq = MLA_ROPE // 4
    inv = jnp.power(ROPE_BASE, -jnp.arange(n_freq, dtype=F32) / n_freq)
    ang = jnp.concatenate([row[:, None] * inv, col[:, None] * inv], axis=-1)
    ang = jnp.concatenate([jnp.zeros((ctx_len, MLA_ROPE // 2), F32), ang], axis=0)
    cos, sin = jnp.cos(ang), jnp.sin(ang)
    l = seq_len + ctx_len
    half = MLA_ROPE // 2
    c = jnp.ones((l, LANES), F32).at[:, MLA_NOPE:MLA_NOPE + half].set(cos).at[:, MLA_NOPE + half:MLA_NOPE + 2 * half].set(cos)
    s1 = jnp.zeros((l, LANES), F32).at[:, MLA_NOPE:MLA_NOPE + half].set(-sin)
    s2 = jnp.zeros((l, LANES), F32).at[:, MLA_NOPE + half:MLA_NOPE + 2 * half].set(sin)
    return c, s1, s2


def _ev_weights(w_in, q_norm, kv_norm, w_uq, w_ukv, wa2, ba, seq_len, ctx_len):
    d = w_in.shape[0]
    widths = (MLA_Q_RANK, MLA_KV_RANK, MLA_ROPE, GLA_HEADS * GLA_DK, GLA_HEADS * GLA_DK,
              GLA_HEADS * GLA_DV, GLA_HEADS * GLA_DV, GLA_GATE_RANK, GLA_GATE_RANK)
    offs = np.cumsum((0,) + widths)
    part = lambda i: w_in[:, offs[i]:offs[i + 1]]
    kpe = jnp.zeros((d, HEAD_PAD), F32).at[:, MLA_NOPE:MLA_NOPE + MLA_ROPE].set(part(2))
    gate = jnp.zeros((d, LANES), F32).at[:, :GLA_GATE_RANK].set(part(7)).at[:, GLA_GATE_RANK:2 * GLA_GATE_RANK].set(part(8))
    win = jnp.concatenate([part(0), part(1), kpe, part(3), part(4), part(5), part(6), gate], axis=1).astype(BF16)
    hq = MLA_NOPE + MLA_ROPE
    wuq = jnp.pad(w_uq.reshape(MLA_Q_RANK, MLA_HEADS, hq), ((0, 0), (0, 0), (0, HEAD_PAD - hq)))
    wuq = wuq.reshape(MLA_Q_RANK, MLA_HEADS * HEAD_PAD).astype(BF16)
    ukv = w_ukv.reshape(MLA_KV_RANK, MLA_HEADS, MLA_NOPE + MLA_V)
    wk = jnp.pad(ukv[:, :, :MLA_NOPE], ((0, 0), (0, 0), (0, HEAD_PAD - MLA_NOPE))).reshape(MLA_KV_RANK, MLA_HEADS * HEAD_PAD)
    wv = ukv[:, :, MLA_NOPE:].reshape(MLA_KV_RANK, MLA_HEADS * MLA_V)
    wukv = jnp.concatenate([wk, wv], axis=1).astype(BF16)
    gdk = GLA_HEADS * GLA_DK
    wa2p = jnp.zeros((LANES, 2 * gdk), F32).at[:GLA_GATE_RANK, :gdk].set(wa2[0])
    wa2p = wa2p.at[GLA_GATE_RANK:2 * GLA_GATE_RANK, gdk:].set(wa2[1]).astype(BF16)
    c, s1, s2 = _rope_tables(seq_len, ctx_len)
    return dict(win=win, qn=q_norm[None, :], kvn=kv_norm[None, :], wuq=wuq, wukv=wukv, wa2=wa2p,
                ba=ba.reshape(1, 2 * gdk), rope_c=c, rope_s1=s1, rope_s2=s2)


def _od_weights(conv_w, conv_b, gate_w, gate_b, lam):
    gw = lambda dd: jnp.concatenate([gate_w[dd, 0], gate_w[dd, 1]], axis=-1).astype(BF16)
    return dict(cw=conv_w, cb=conv_b[None, :], gwf=gw(0), gbf=gate_b[0], lamf=lam[0][None, :],
                gwb=gw(1), gbb=gate_b[1], lamb=lam[1][None, :])


def kernel(x, c, ctx, c_ctx, mod_w, mod_b, ln_g, ln_b,
           ev_w_in, ev_q_norm, ev_kv_norm, ev_w_uq, ev_w_ukv, ev_gla_wa2, ev_gla_ba, ev_gla_norm, ev_w_out,
           od_w_in, od_conv_w, od_conv_b, od_gate_w, od_gate_b, od_lambda, od_w_out,
           peer_wq, peer_k1, peer_k2, peer_u, peer_v):
    b, seq_len, d = x.shape
    ctx_len = ctx.shape[1]
    depth = mod_w.shape[0]
    assert ctx_len % TOK_TILE == 0 and seq_len % TOK_TILE == 0 and seq_len % GRID_W == 0
    alpha = (2 * depth) ** 0.25

    rows = -(-(b + 1) // SUBLANES) * SUBLANES
    cc = jnp.zeros((rows, d), F32).at[:b].set(c).at[b].set(c_ctx)

    h = jnp.concatenate([ctx, x], axis=1)
    nct = ctx_len // TOK_TILE
    off = 0
    for layer in range(depth):
        last = layer == depth - 1
        j = layer // 2
        mods = _mod_vectors(cc, mod_w[layer], mod_b[layer][None, :]).reshape(rows, N_MOD, d)
        modt = jnp.stack([jnp.broadcast_to(mods[b], (b, N_MOD, d)), mods[:b]], axis=1).reshape(2 * b, N_MOD, d)
        lng = ln_g[layer][:, None, :]
        lnb = ln_b[layer][:, None, :]
        assert off == 0
        if layer % 2 == 0:
            w = _ev_weights(ev_w_in[j], ev_q_norm[j], ev_kv_norm[j], ev_w_uq[j], ev_w_ukv[j],
                            ev_gla_wa2[j], ev_gla_ba[j], seq_len, ctx_len)
            q, k, v, gq, gk, gv, gg, laf, lab = _ev_proj(h, modt, nct, w)
            att = _attention(q, k, v, nct, ctx_len)
            o_f, o_b = _gla(gq, gk, gv, laf, lab, nct)
            assert not last, "the final layer drops the context rows only after an odd layer"
            h = _ev_out(att, o_f, o_b, gg, ev_gla_norm[j][None, :], ev_w_out[j].astype(BF16), h, modt,
                        lng[0], lnb[0], nct, alpha)
        else:
            w = _od_weights(od_conv_w[j], od_conv_b[j], od_gate_w[j], od_gate_b[j], od_lambda[j])
            br, rnn = _od_proj(h, modt, nct, od_w_in[j].astype(BF16))
            yf, yb = _rglru(rnn, nct, w)
            off = nct if last else 0
            h = _od_out(br, yf, yb, od_w_out[j].astype(BF16), h, modt, lng[0], lnb[0], nct, off, alpha)
        hmt, n1, e1, r2, e2 = _peer_route(h, modt, nct, off, peer_wq[layer], peer_k1[layer], peer_k2[layer])
        f = _peer_dense(hmt, n1, e1, r2, e2, peer_u[layer], peer_v[layer])
        h = _peer_res(h, f, modt, nct, off, lng[1], lnb[1], alpha)
    return h if off else h[:, ctx_len:]
```

```python
import functools

import numpy as np
import jax
import jax.numpy as jnp
from jax import lax
from jax.experimental import pallas as pl
from jax.experimental.pallas import tpu as pltpu

F32 = jnp.float32
BF16 = jnp.bfloat16
HI = lax.Precision.HIGHEST

N_MOD = 6
LN_EPS = 1e-6
GRID_W = 64
MLA_HEADS = 8
MLA_Q_RANK = 384
MLA_KV_RANK = 256
MLA_NOPE = 64
MLA_ROPE = 32
MLA_V = 64
MLA_SCALE = (MLA_NOPE + MLA_ROPE) ** -0.5
ROPE_BASE = 10000.0
GLA_HEADS = 4
GLA_DK = 64
GLA_DV = 128
GLA_GATE_RANK = 16
GLA_TAU = 16.0
RG_WIDTH = 1280
RG_BLOCKS = 10
RG_BW = RG_WIDTH // RG_BLOCKS
RG_CONV = 4
RG_CONV_LEFT = 2
RG_C = 8.0
PEER_HEADS = 8
PEER_NK = 128
PEER_TOPK = 16
PEER_QDIM = 256
PEER_HALF = PEER_QDIM // 2

LANES = 128
SUBLANES = 8
TOK_TILE = 256
GLA_CHUNK = 64
GLA_SUB = 16
PEER_DENSE_TOK = 512
PEER_EXPERT_BLK = 1024
VMEM_LIMIT = 56 * 1024 * 1024

HEAD_PAD = LANES


def _cp(*sem):
    return pltpu.CompilerParams(dimension_semantics=sem, vmem_limit_bytes=VMEM_LIMIT)


def _gelu(x):
    return 0.5 * x * (1.0 + lax.erf(x * (2.0 ** -0.5)))


def _ln(y, g, b):
    mu = jnp.mean(y, axis=-1, keepdims=True)
    d = y - mu
    var = jnp.mean(d * d, axis=-1, keepdims=True)
    return d * lax.rsqrt(var + LN_EPS) * g + b


def _mod_index(nct, off):
    def index(b, i):
        return (2 * b + jnp.where(i + off >= nct, 1, 0), 0, 0)
    return index


def _mod_kernel(c_ref, w_ref, b_ref, o_ref):
    x = c_ref[...]
    s = x * jax.nn.sigmoid(x)
    o_ref[...] = jnp.dot(s, w_ref[...], precision=HI, preferred_element_type=F32) + b_ref[...]


def _mod_vectors(cc, w, b):
    rows, d = cc.shape
    n = w.shape[1] // d
    return pl.pallas_call(
        _mod_kernel,
        grid=(n,),
        in_specs=[pl.BlockSpec((rows, d), lambda j: (0, 0)),
                  pl.BlockSpec((d, d), lambda j: (0, j)),
                  pl.BlockSpec((1, d), lambda j: (0, j))],
        out_specs=pl.BlockSpec((rows, d), lambda j: (0, j)),
        out_shape=jax.ShapeDtypeStruct((rows, w.shape[1]), F32),
        compiler_params=_cp("arbitrary"),
        name="mod_vectors",
    )(cc, w, b)


def _rope_block(x, c, s1, s2):
    return x * c + pltpu.roll(x, LANES - MLA_ROPE // 2, axis=1) * s1 + pltpu.roll(x, MLA_ROPE // 2, axis=1) * s2


def _ev_proj_kernel(h_ref, mod_ref, win_ref, qn_ref, kvn_ref, wuq_ref, wukv_ref, wa2_ref, ba_ref,
                    rc_ref, rs1_ref, rs2_ref,
                    q_ref, k_ref, v_ref, gq_ref, gk_ref, gv_ref, gg_ref, laf_ref, lab_ref):
    x = h_ref[0]
    mod = mod_ref[0]
    xin = (x * (1.0 + mod[1:2]) + mod[0:1]).astype(BF16)
    p = jnp.dot(xin, win_ref[...], preferred_element_type=F32)
    o = 0
    cq = p[:, o:o + MLA_Q_RANK]; o += MLA_Q_RANK
    ckv = p[:, o:o + MLA_KV_RANK]; o += MLA_KV_RANK
    kpe = p[:, o:o + HEAD_PAD]; o += HEAD_PAD
    gdk = GLA_HEADS * GLA_DK
    gdv = GLA_HEADS * GLA_DV
    gq_ref[0] = p[:, o:o + gdk] * (GLA_DK ** -0.5); o += gdk
    gk_ref[0] = p[:, o:o + gdk]; o += gdk
    gv_ref[0] = p[:, o:o + gdv]; o += gdv
    gg_ref[0] = p[:, o:o + gdv]; o += gdv
    ga = p[:, o:o + LANES]

    c, s1, s2 = rc_ref[...], rs1_ref[...], rs2_ref[...]
    nq = cq * lax.rsqrt(jnp.mean(cq * cq, axis=-1, keepdims=True) + LN_EPS) * qn_ref[...]
    q = jnp.dot(nq.astype(BF16), wuq_ref[...], preferred_element_type=F32)
    nkv = ckv * lax.rsqrt(jnp.mean(ckv * ckv, axis=-1, keepdims=True) + LN_EPS) * kvn_ref[...]
    kv = jnp.dot(nkv.astype(BF16), wukv_ref[...], preferred_element_type=F32)
    kpe_r = _rope_block(kpe, c, s1, s2)
    for h in range(MLA_HEADS):
        sl = slice(h * HEAD_PAD, (h + 1) * HEAD_PAD)
        q_ref[0, :, sl] = _rope_block(q[:, sl], c, s1, s2).astype(BF16)
        k_ref[0, :, sl] = (kv[:, sl] + kpe_r).astype(BF16)
    v_ref[0] = kv[:, MLA_HEADS * HEAD_PAD:].astype(BF16)

    z = jnp.dot(ga.astype(BF16), wa2_ref[...], preferred_element_type=F32) + ba_ref[...]
    la = jax.nn.log_sigmoid(z) * (1.0 / GLA_TAU)
    laf_ref[0] = la[:, :gdk]
    lab_ref[0] = la[:, gdk:]


def _ev_proj(h, modt, nct, w):
    b, l, d = h.shape
    nt = l // TOK_TILE
    tok = lambda width: pl.BlockSpec((1, TOK_TILE, width), lambda bi, i: (bi, i, 0))
    full = lambda a: pl.BlockSpec(a.shape, lambda bi, i: (0,) * a.ndim)
    rope = pl.BlockSpec((TOK_TILE, LANES), lambda bi, i: (i, 0))
    gdk = GLA_HEADS * GLA_DK
    gdv = GLA_HEADS * GLA_DV
    widths = [(MLA_HEADS * HEAD_PAD, BF16), (MLA_HEADS * HEAD_PAD, BF16), (MLA_HEADS * MLA_V, BF16),
              (gdk, F32), (gdk, F32), (gdv, F32), (gdv, F32), (gdk, F32), (gdk, F32)]
    return pl.pallas_call(
        _ev_proj_kernel,
        grid=(b, nt),
        in_specs=[tok(d), pl.BlockSpec((1, N_MOD, d), _mod_index(nct, 0)),
                  full(w["win"]), full(w["qn"]), full(w["kvn"]), full(w["wuq"]), full(w["wukv"]),
                  full(w["wa2"]), full(w["ba"]), rope, rope, rope],
        out_specs=[tok(wd) for wd, _ in widths],
        out_shape=[jax.ShapeDtypeStruct((b, l, wd), dt) for wd, dt in widths],
        compiler_params=_cp("parallel", "parallel"),
        name="ev_proj",
    )(h, modt, w["win"], w["qn"], w["kvn"], w["wuq"], w["wukv"], w["wa2"], w["ba"],
      w["rope_c"], w["rope_s1"], w["rope_s2"])


def _attn_kernel(q_ref, k_ref, v_ref, o_ref, *, nct, ctx_len):
    def attend(nk):
        lane = lax.broadcasted_iota(jnp.int32, (TOK_TILE, LANES), 1)
        for hp in range(MLA_HEADS // 2):
            vp = v_ref[0, 0:nk, hp * LANES:(hp + 1) * LANES]
            outs = []
            for hh in range(2):
                h = 2 * hp + hh
                qh = q_ref[0, :, h * HEAD_PAD:(h + 1) * HEAD_PAD]
                kh = k_ref[0, 0:nk, h * HEAD_PAD:(h + 1) * HEAD_PAD]
                s = lax.dot_general(qh, kh, (((1,), (1,)), ((), ())), preferred_element_type=F32)
                m = jnp.max(s, axis=-1, keepdims=True)
                p = jnp.exp((s - m) * MLA_SCALE)
                den = jnp.sum(p, axis=-1, keepdims=True)
                o = jnp.dot(p.astype(BF16), vp, preferred_element_type=F32)
                outs.append(o / den)
            o_ref[0, :, hp * LANES:(hp + 1) * LANES] = jnp.where(lane < MLA_V, outs[0], outs[1]).astype(BF16)

    is_ctx = pl.program_id(1) < nct

    @pl.when(is_ctx)
    def _():
        attend(ctx_len)

    @pl.when(jnp.logical_not(is_ctx))
    def _():
        attend(k_ref.shape[1])


def _attention(q, k, v, nct, ctx_len):
    b, l, _ = q.shape
    nt = l // TOK_TILE
    return pl.pallas_call(
        functools.partial(_attn_kernel, nct=nct, ctx_len=ctx_len),
        grid=(b, nt),
        in_specs=[pl.BlockSpec((1, TOK_TILE, q.shape[2]), lambda bi, i: (bi, i, 0)),
                  pl.BlockSpec((1, l, k.shape[2]), lambda bi, i: (bi, 0, 0)),
                  pl.BlockSpec((1, l, v.shape[2]), lambda bi, i: (bi, 0, 0))],
        out_specs=pl.BlockSpec((1, TOK_TILE, v.shape[2]), lambda bi, i: (bi, i, 0)),
        out_shape=jax.ShapeDtypeStruct((b, l, v.shape[2]), BF16),
        compiler_params=_cp("parallel", "arbitrary"),
        name="mla_attention",
    )(q, k, v)


def _bwd_tile(s, nct, nt):
    return jnp.where(s < nct, nct - 1 - s, nt - 1 - (s - nct))


def _gla_chunk(q, k, v, la, st, rev):
    c = GLA_CHUNK
    nsub = c // GLA_SUB
    row = lax.broadcasted_iota(jnp.int32, (c, c), 0)
    col = lax.broadcasted_iota(jnp.int32, (c, c), 1)
    tri = (col >= row) if rev else (col <= row)
    cum = jnp.dot(tri.astype(F32), la, precision=HI, preferred_element_type=F32)
    rid = lax.broadcasted_iota(jnp.int32, (c, LANES), 0)
    lane = lax.broadcasted_iota(jnp.int32, (c, LANES), 1)
    head0 = lane < GLA_DK

    qparts, kparts, eref = [], [], jnp.zeros((c, LANES), F32)
    for jb in range(nsub):
        r = jb * GLA_SUB if rev else jb * GLA_SUB + GLA_SUB - 1
        ej = cum[r:r + 1, :]
        valid = (rid <= jb * GLA_SUB + GLA_SUB - 1) if rev else (rid >= jb * GLA_SUB)
        qparts.append(jnp.where(valid, q * jnp.exp(jnp.where(valid, cum - ej, 0.0)), 0.0))
        inblk = (rid >= jb * GLA_SUB) & (rid < (jb + 1) * GLA_SUB)
        eref = jnp.where(inblk, ej, eref)
    kt = k * jnp.exp(eref - cum)
    for jb in range(nsub):
        inblk = (rid >= jb * GLA_SUB) & (rid < (jb + 1) * GLA_SUB)
        kparts.append(jnp.where(inblk, kt, 0.0))
    qbig = jnp.concatenate(qparts, axis=1)
    kbig = jnp.concatenate(kparts, axis=1).astype(BF16)
    h0big = lax.broadcasted_iota(jnp.int32, (c, nsub * LANES), 1) % LANES < GLA_DK
    qst = jnp.concatenate([jnp.where(h0big, qbig, 0.0), jnp.where(h0big, 0.0, qbig)], axis=0).astype(BF16)
    sc = lax.dot_general(qst, kbig, (((1,), (1,)), ((), ())), preferred_element_type=F32)
    ri = lax.broadcasted_iota(jnp.int32, (2 * c, c), 0) % c
    ci = lax.broadcasted_iota(jnp.int32, (2 * c, c), 1)
    sc = jnp.where((ci >= ri) if rev else (ci <= ri), sc, 0.0)
    vb = v.astype(BF16)
    intra = jnp.dot(sc.astype(BF16), vb, preferred_element_type=F32)
    qs = (q * jnp.exp(cum)).astype(BF16)
    inter = lax.dot_general(qs, st.astype(BF16), (((1,), (1,)), ((), ())), preferred_element_type=F32)
    out = jnp.concatenate([intra[0:c, 0:GLA_DV] + inter[:, 0:GLA_DV],
                           intra[c:2 * c, GLA_DV:2 * GLA_DV] + inter[:, GLA_DV:2 * GLA_DV]], axis=1)
    last = cum[0:1, :] if rev else cum[c - 1:c, :]
    kd = (k * jnp.exp(last - cum)).astype(BF16)
    upd = lax.dot_general(vb, kd, (((0,), (0,)), ((), ())), preferred_element_type=F32)
    srow = lax.broadcasted_iota(jnp.int32, (2 * GLA_DV, LANES), 0)
    slane = lax.broadcasted_iota(jnp.int32, (2 * GLA_DV, LANES), 1)
    diag = (srow < GLA_DV) == (slane < GLA_DK)
    st_new = jnp.where(diag, st * jnp.exp(last) + upd, 0.0)
    return out, st_new


def _gla_kernel(qf_ref, kf_ref, vf_ref, laf_ref, qb_ref, kb_ref, vb_ref, lab_ref, of_ref, ob_ref, stf_ref, stb_ref):
    @pl.when(pl.program_id(1) == 0)
    def _():
        stf_ref[...] = jnp.zeros_like(stf_ref)
        stb_ref[...] = jnp.zeros_like(stb_ref)

    nchunk = TOK_TILE // GLA_CHUNK
    npair = GLA_HEADS // 2

    def body(ci, carry):
        rf = pl.multiple_of(ci * GLA_CHUNK, GLA_CHUNK)
        rb = pl.multiple_of((nchunk - 1 - ci) * GLA_CHUNK, GLA_CHUNK)
        for pr in range(npair):
            ks = slice(pr * LANES, (pr + 1) * LANES)
            vs = slice(pr * 2 * GLA_DV, (pr + 1) * 2 * GLA_DV)
            o, st = _gla_chunk(qf_ref[0, pl.ds(rf, GLA_CHUNK), ks], kf_ref[0, pl.ds(rf, GLA_CHUNK), ks],
                               vf_ref[0, pl.ds(rf, GLA_CHUNK), vs], laf_ref[0, pl.ds(rf, GLA_CHUNK), ks],
                               stf_ref[pr], False)
            of_ref[0, pl.ds(rf, GLA_CHUNK), vs] = o
            stf_ref[pr] = st
            o, st = _gla_chunk(qb_ref[0, pl.ds(rb, GLA_CHUNK), ks], kb_ref[0, pl.ds(rb, GLA_CHUNK), ks],
                               vb_ref[0, pl.ds(rb, GLA_CHUNK), vs], lab_ref[0, pl.ds(rb, GLA_CHUNK), ks],
                               stb_ref[pr], True)
            ob_ref[0, pl.ds(rb, GLA_CHUNK), vs] = o
            stb_ref[pr] = st
        return carry

    lax.fori_loop(0, nchunk, body, 0)


def _gla(gq, gk, gv, laf, lab, nct):
    b, l, dk = gq.shape
    dv = gv.shape[2]
    nt = l // TOK_TILE
    fwd = lambda width: pl.BlockSpec((1, TOK_TILE, width), lambda bi, s: (bi, s, 0))
    bwd = lambda width: pl.BlockSpec((1, TOK_TILE, width), lambda bi, s: (bi, _bwd_tile(s, nct, nt), 0))
    npair = GLA_HEADS // 2
    return pl.pallas_call(
        _gla_kernel,
        grid=(b, nt),
        in_specs=[fwd(dk), fwd(dk), fwd(dv), fwd(dk), bwd(dk), bwd(dk), bwd(dv), bwd(dk)],
        out_specs=[fwd(dv), bwd(dv)],
        out_shape=[jax.ShapeDtypeStruct((b, l, dv), F32)] * 2,
        scratch_shapes=[pltpu.VMEM((npair, 2 * GLA_DV, LANES), F32)] * 2,
        compiler_params=_cp("parallel", "arbitrary"),
        name="gla_scan",
    )(gq, gk, gv, laf, gq, gk, gv, lab)


def _ev_out_kernel(att_ref, of_ref, ob_ref, gg_ref, gn_ref, wout_ref, h_ref, mod_ref, lng_ref, lnb_ref, o_ref, *, alpha):
    o = of_ref[0] + ob_ref[0]
    g = gg_ref[0]
    parts = []
    for hd in range(GLA_HEADS):
        sl = slice(hd * GLA_DV, (hd + 1) * GLA_DV)
        oh = o[:, sl]
        y = oh * lax.rsqrt(jnp.mean(oh * oh, axis=-1, keepdims=True) + LN_EPS) * gn_ref[:, sl]
        gh = g[:, sl]
        parts.append((y * (gh * jax.nn.sigmoid(gh))).astype(BF16))
    gla = jnp.concatenate(parts, axis=1)
    na = att_ref.shape[2]
    mix = (jnp.dot(att_ref[0], wout_ref[0:na, :], preferred_element_type=F32)
           + jnp.dot(gla, wout_ref[na:, :], preferred_element_type=F32))
    mod = mod_ref[0]
    o_ref[0] = _ln(alpha * h_ref[0] + mod[2:3] * mix, lng_ref[...], lnb_ref[...])


def _ev_out(att, o_f, o_b, gg, gn, wout, h, modt, lng, lnb, nct, alpha):
    b, l, d = h.shape
    nt = l // TOK_TILE
    tok = lambda a: pl.BlockSpec((1, TOK_TILE, a.shape[2]), lambda bi, i: (bi, i, 0))
    full = lambda a: pl.BlockSpec(a.shape, lambda bi, i: (0,) * a.ndim)
    return pl.pallas_call(
        functools.partial(_ev_out_kernel, alpha=alpha),
        grid=(b, nt),
        in_specs=[tok(att), tok(o_f), tok(o_b), tok(gg), full(gn), full(wout), tok(h),
                  pl.BlockSpec((1, N_MOD, d), _mod_index(nct, 0)), full(lng), full(lnb)],
        out_specs=tok(h),
        out_shape=jax.ShapeDtypeStruct((b, l, d), F32),
        compiler_params=_cp("parallel", "parallel"),
        name="ev_out",
    )(att, o_f, o_b, gg, gn, wout, h, modt, lng, lnb)


def _od_proj_kernel(h_ref, mod_ref, win_ref, br_ref, rnn_ref):
    mod = mod_ref[0]
    xin = (h_ref[0] * (1.0 + mod[1:2]) + mod[0:1]).astype(BF16)
    p = jnp.dot(xin, win_ref[...], preferred_element_type=F32)
    br_ref[0] = p[:, :RG_WIDTH]
    rnn_ref[0] = p[:, RG_WIDTH:]


def _od_proj(h, modt, nct, win):
    b, l, d = h.shape
    nt = l // TOK_TILE
    tok = lambda width: pl.BlockSpec((1, TOK_TILE, width), lambda bi, i: (bi, i, 0))
    return pl.pallas_call(
        _od_proj_kernel,
        grid=(b, nt),
        in_specs=[tok(d), pl.BlockSpec((1, N_MOD, d), _mod_index(nct, 0)),
                  pl.BlockSpec(win.shape, lambda bi, i: (0, 0))],
        out_specs=[tok(RG_WIDTH), tok(RG_WIDTH)],
        out_shape=[jax.ShapeDtypeStruct((b, l, RG_WIDTH), F32)] * 2,
        compiler_params=_cp("parallel", "parallel"),
        name="od_proj",
    )(h, modt, win)


def _rg_prepare(z_ref, pv_ref, nx_ref, prev_zero, next_zero, cw_ref, cb_ref, gw_ref, gb_ref, lam_ref, a_ref, x_ref):
    t = TOK_TILE
    z = z_ref[0]
    pv = jnp.where(prev_zero, 0.0, pv_ref[0])
    nx = jnp.where(next_zero, 0.0, nx_ref[0])
    ext = jnp.concatenate([pv, z, nx], axis=0)
    base = SUBLANES - RG_CONV_LEFT
    y = cb_ref[...] + ext[base:base + t] * cw_ref[0:1, :]
    for tap in range(1, RG_CONV):
        y = y + ext[base + tap:base + tap + t] * cw_ref[tap:tap + 1, :]
    for n in range(RG_BLOCKS):
        sl = slice(n * RG_BW, (n + 1) * RG_BW)
        yb = y[:, sl]
        ri = jnp.dot(yb.astype(BF16), gw_ref[n], preferred_element_type=F32)
        r = jax.nn.sigmoid(ri[:, :RG_BW] + gb_ref[0:1, sl])
        i = jax.nn.sigmoid(ri[:, RG_BW:] + gb_ref[1:2, sl])
        log_a = (-RG_C) * jax.nn.softplus(-lam_ref[:, sl]) * r
        th = jnp.tanh(log_a)
        a_ref[:, sl] = jnp.exp(log_a)
        x_ref[:, sl] = jnp.sqrt(-2.0 * th / (1.0 - th)) * (i * yb)


def _rglru_kernel(zf_ref, pf_ref, nf_ref, zb_ref, pb_ref, nb_ref, cw_ref, cb_ref,
                  gwf_ref, gbf_ref, lamf_ref, gwb_ref, gbb_ref, lamb_ref,
                  yf_ref, yb_ref, af_ref, xf_ref, ab_ref, xb_ref, hf_ref, hb_ref, *, nct, nt):
    s = pl.program_id(1)

    @pl.when(s == 0)
    def _():
        hf_ref[...] = jnp.zeros_like(hf_ref)
        hb_ref[...] = jnp.zeros_like(hb_ref)

    def edges(j):
        return (j == 0) | (j == nct), (j == nct - 1) | (j == nt - 1)

    pz, nz = edges(s)
    _rg_prepare(zf_ref, pf_ref, nf_ref, pz, nz, cw_ref, cb_ref, gwf_ref, gbf_ref, lamf_ref, af_ref, xf_ref)
    pz, nz = edges(_bwd_tile(s, nct, nt))
    _rg_prepare(zb_ref, pb_ref, nb_ref, pz, nz, cw_ref, cb_ref, gwb_ref, gbb_ref, lamb_ref, ab_ref, xb_ref)

    def body(kk, carry):
        hf, hb = carry
        tf = kk
        tb = TOK_TILE - 1 - kk
        hf = af_ref[pl.ds(tf, 1), :] * hf + xf_ref[pl.ds(tf, 1), :]
        yf_ref[0, pl.ds(tf, 1), :] = hf
        hb = ab_ref[pl.ds(tb, 1), :] * hb + xb_ref[pl.ds(tb, 1), :]
        yb_ref[0, pl.ds(tb, 1), :] = hb
        return hf, hb

    hf, hb = lax.fori_loop(0, TOK_TILE, body, (hf_ref[...], hb_ref[...]), unroll=8)
    hf_ref[...] = hf
    hb_ref[...] = hb


def _rglru(rnn, nct, w):
    b, l, wd = rnn.shape
    nt = l // TOK_TILE
    per = TOK_TILE // SUBLANES
    nh = l // SUBLANES
    bt = lambda s: _bwd_tile(s, nct, nt)
    tile = lambda f: pl.BlockSpec((1, TOK_TILE, wd), lambda bi, s: (bi, f(s), 0))
    prev = lambda f: pl.BlockSpec((1, SUBLANES, wd), lambda bi, s: (bi, jnp.maximum(f(s) * per - 1, 0), 0))
    nxt = lambda f: pl.BlockSpec((1, SUBLANES, wd), lambda bi, s: (bi, jnp.minimum((f(s) + 1) * per, nh - 1), 0))
    full = lambda a: pl.BlockSpec(a.shape, lambda bi, s: (0,) * a.ndim)
    ident = lambda s: s
    vm = lambda rows: pltpu.VMEM((rows, wd), F32)
    return pl.pallas_call(
        functools.partial(_rglru_kernel, nct=nct, nt=nt),
        grid=(b, nt),
        in_specs=[tile(ident), prev(ident), nxt(ident), tile(bt), prev(bt), nxt(bt),
                  full(w["cw"]), full(w["cb"]), full(w["gwf"]), full(w["gbf"]), full(w["lamf"]),
                  full(w["gwb"]), full(w["gbb"]), full(w["lamb"])],
        out_specs=[tile(ident), tile(bt)],
        out_shape=[jax.ShapeDtypeStruct((b, l, wd), F32)] * 2,
        scratch_shapes=[vm(TOK_TILE), vm(TOK_TILE), vm(TOK_TILE), vm(TOK_TILE), vm(1), vm(1)],
        compiler_params=_cp("parallel", "arbitrary"),
        name="rglru_scan",
    )(rnn, rnn, rnn, rnn, rnn, rnn, w["cw"], w["cb"], w["gwf"], w["gbf"], w["lamf"], w["gwb"], w["gbb"], w["lamb"])


def _od_out_kernel(br_ref, yf_ref, yb_ref, wout_ref, h_ref, mod_ref, lng_ref, lnb_ref, o_ref, *, alpha):
    m = (_gelu(br_ref[0]) * (yf_ref[0] + yb_ref[0])).astype(BF16)
    mix = jnp.dot(m, wout_ref[...], preferred_element_type=F32)
    mod = mod_ref[0]
    o_ref[0] = _ln(alpha * h_ref[0] + mod[2:3] * mix, lng_ref[...], lnb_ref[...])


def _od_out(br, yf, yb, wout, h, modt, lng, lnb, nct, off, alpha):
    b, l, d = h.shape
    nt = l // TOK_TILE - off
    tok = lambda a: pl.BlockSpec((1, TOK_TILE, a.shape[2]), lambda bi, i: (bi, i + off, 0))
    full = lambda a: pl.BlockSpec(a.shape, lambda bi, i: (0,) * a.ndim)
    return pl.pallas_call(
        functools.partial(_od_out_kernel, alpha=alpha),
        grid=(b, nt),
        in_specs=[tok(br), tok(yf), tok(yb), full(wout), tok(h),
                  pl.BlockSpec((1, N_MOD, d), _mod_index(nct, off)), full(lng), full(lnb)],
        out_specs=pl.BlockSpec((1, TOK_TILE, d), lambda bi, i: (bi, i, 0)),
        out_shape=jax.ShapeDtypeStruct((b, nt * TOK_TILE, d), F32),
        compiler_params=_cp("parallel", "parallel"),
        name="od_out",
    )(br, yf, yb, wout, h, modt, lng, lnb)


def _top16_rows(s, vals_ref):
    nk = s.shape[0]
    rid = lax.broadcasted_iota(jnp.int32, s.shape, 0).astype(F32)
    rank = jnp.full(s.shape, float(PEER_TOPK), F32)
    left = s
    for r in range(PEER_TOPK):
        m = jnp.max(left, axis=0, keepdims=True)
        pos = jnp.min(jnp.where(left == m, rid, float(nk)), axis=0, keepdims=True)
        hit = rid == pos
        vals_ref[r:r + 1, :] = m
        rank = jnp.where(hit, float(r), rank)
        left = jnp.where(hit, -jnp.inf, left)
    ex = jnp.where(rank < float(PEER_TOPK), jnp.exp(s - vals_ref[0:1, :]), 0.0)
    return rank, ex


def _cand_sums(t1, t2):
    blocks = [t1[0:1] + t2[0:16]]
    for a in range(1, 8):
        blocks.append(t1[a:a + 1] + t2[0:8])
    blocks.append(t1[8:16] + t2[0:1])
    return jnp.concatenate(blocks, axis=0)


def _selected_counts(t1, t2, flat):
    cs0 = _cand_sums(t1, t2)
    cs = cs0
    sel = jnp.zeros(cs.shape, F32)
    for _ in range(PEER_TOPK):
        m = jnp.max(cs, axis=0, keepdims=True)
        pos = jnp.min(jnp.where(cs == m, flat, float(PEER_TOPK * PEER_TOPK)), axis=0, keepdims=True)
        hit = flat == pos
        sel = jnp.where(hit, 1.0, sel)
        cs = jnp.where(hit, -jnp.inf, cs)
    z = jnp.sum(sel * jnp.exp(cs0 - cs0[0:1]), axis=0, keepdims=True)
    counts = [jnp.sum(sel[0:16], axis=0, keepdims=True)]
    for a in range(1, 8):
        counts.append(jnp.sum(sel[8 + 8 * a:16 + 8 * a], axis=0, keepdims=True))
    for a in range(8, 16):
        counts.append(sel[64 + a:65 + a])
    return counts, z


def _cand_flat_index(ntok):
    a16 = lax.broadcasted_iota(jnp.int32, (16, ntok), 0)
    a8 = lax.broadcasted_iota(jnp.int32, (8, ntok), 0)
    blocks = [a16]
    for a in range(1, 8):
        blocks.append(a * PEER_TOPK + a8)
    blocks.append((a8 + 8) * PEER_TOPK)
    return jnp.concatenate(blocks, axis=0).astype(F32)


def _split_bf16(x):
    hi = x.astype(BF16)
    return hi, (x - hi.astype(F32)).astype(BF16)


def _dot3(a, b, dims):
    dg = lambda x, y: lax.dot_general(x, y, dims, preferred_element_type=F32)
    return dg(a[0], b[0]) + (dg(a[0], b[1]) + dg(a[1], b[0]))


def _peer_route_kernel(h_ref, mod_ref, wq_ref, k_ref, hmt_ref, n1_ref, e1_ref, r2_ref, e2_ref, t1_ref, t2_ref):
    mod = mod_ref[0]
    hm = h_ref[0] * (1.0 + mod[4:5]) + mod[3:4]
    hmt_ref[...] = hm.T.astype(BF16)
    q = _dot3(_split_bf16(hm), (wq_ref[0], wq_ref[1]), (((1,), (0,)), ((), ())))
    flat = _cand_flat_index(LANES)
    nn = (((1,), (1,)), ((), ()))
    qs = _split_bf16(q)
    for h in range(PEER_HEADS):
        c1 = slice(h * PEER_QDIM, h * PEER_QDIM + PEER_HALF)
        c2 = slice(h * PEER_QDIM + PEER_HALF, (h + 1) * PEER_QDIM)
        s1 = _dot3((k_ref[0], k_ref[1]), (qs[0][:, c1], qs[1][:, c1]), nn)
        s2 = _dot3((k_ref[2], k_ref[3]), (qs[0][:, c2], qs[1][:, c2]), nn)
        for tc in range(hm.shape[0] // LANES):
            cols = slice(tc * LANES, (tc + 1) * LANES)
            r1, e1 = _top16_rows(s1[:, cols], t1_ref)
            r2, e2 = _top16_rows(s2[:, cols], t2_ref)
            counts, z = _selected_counts(t1_ref[...], t2_ref[...], flat)
            n1 = jnp.zeros(r1.shape, F32)
            for a in range(PEER_TOPK):
                n1 = jnp.where(r1 == float(a), counts[a], n1)
            n1_ref[h, :, cols] = n1
            e1_ref[h, :, cols] = e1 / z
            r2_ref[h, :, cols] = r2.astype(BF16)
            e2_ref[h, :, cols] = e2.astype(BF16)


def _peer_route(h, modt, nct, off, wq, k1, k2):
    b, l, d = h.shape
    nt = l // TOK_TILE
    ntok = b * l
    full = lambda a: pl.BlockSpec(a.shape, lambda bi, i: (0,) * a.ndim)
    top_vals = pltpu.VMEM((PEER_TOPK, LANES), F32)
    wq = jnp.stack(_split_bf16(wq))
    keys = jnp.stack(_split_bf16(k1) + _split_bf16(k2))
    per_key = pl.BlockSpec((PEER_HEADS, PEER_NK, TOK_TILE), lambda bi, i: (0, 0, bi * nt + i))
    per_key_shape = lambda dt: jax.ShapeDtypeStruct((PEER_HEADS, PEER_NK, ntok), dt)
    return pl.pallas_call(
        _peer_route_kernel,
        grid=(b, nt),
        in_specs=[pl.BlockSpec((1, TOK_TILE, d), lambda bi, i: (bi, i, 0)),
                  pl.BlockSpec((1, N_MOD, d), _mod_index(nct, off)), full(wq), full(keys)],
        out_specs=[pl.BlockSpec((d, TOK_TILE), lambda bi, i: (0, bi * nt + i))] + [per_key] * 4,
        out_shape=[jax.ShapeDtypeStruct((d, ntok), BF16),
                   per_key_shape(F32), per_key_shape(F32), per_key_shape(BF16), per_key_shape(BF16)],
        scratch_shapes=[top_vals, top_vals],
        compiler_params=_cp("parallel", "parallel"),
        name="peer_route",
    )(h, modt, wq, keys)


def _peer_dense_kernel(hmt_ref, u_ref, vt_ref, n1_ref, e1_ref, r2_ref, e2_ref, o_ref, acc_ref, w_ref):
    j = pl.program_id(1)

    @pl.when(j == 0)
    def _():
        acc_ref[...] = jnp.zeros_like(acc_ref)

    hmt = hmt_ref[...]
    per_step = u_ref.shape[0] // PEER_NK
    for al in range(per_step):
        a = j * per_step + al
        rows = slice(al * PEER_NK, (al + 1) * PEER_NK)
        s = jnp.dot(u_ref[rows, :], hmt, preferred_element_type=F32)
        n1 = [n1_ref[h, pl.ds(a, 1), :].astype(BF16) for h in range(PEER_HEADS)]
        e1 = [e1_ref[h, pl.ds(a, 1), :].astype(BF16) for h in range(PEER_HEADS)]
        for tc in range(s.shape[1] // LANES):
            cols = slice(tc * LANES, (tc + 1) * LANES)
            g = jnp.zeros((PEER_NK, LANES), BF16)
            for h in range(PEER_HEADS):
                g = g + jnp.where(r2_ref[h, :, cols] < n1[h][:, cols], e2_ref[h, :, cols] * e1[h][:, cols],
                                  jnp.zeros((), BF16))
            w_ref[rows, cols] = _gelu(s[:, cols]).astype(BF16) * g
    acc_ref[...] += jnp.dot(vt_ref[...], w_ref[...], preferred_element_type=F32)

    @pl.when(j == pl.num_programs(1) - 1)
    def _():
        o_ref[...] = acc_ref[...].T


def _peer_dense(hmt, n1, e1, r2, e2, u, v):
    d, ntok = hmt.shape
    nexp = u.shape[0]
    tt = PEER_DENSE_TOK if ntok % PEER_DENSE_TOK == 0 else TOK_TILE
    ub = u.astype(BF16)
    vt = v.T.astype(BF16)
    per_key = pl.BlockSpec((PEER_HEADS, PEER_NK, tt), lambda i, j: (0, 0, i))
    return pl.pallas_call(
        _peer_dense_kernel,
        grid=(ntok // tt, nexp // PEER_EXPERT_BLK),
        in_specs=[pl.BlockSpec((d, tt), lambda i, j: (0, i)),
                  pl.BlockSpec((PEER_EXPERT_BLK, d), lambda i, j: (j, 0)),
                  pl.BlockSpec((d, PEER_EXPERT_BLK), lambda i, j: (0, j)),
                  per_key, per_key, per_key, per_key],
        out_specs=pl.BlockSpec((tt, d), lambda i, j: (i, 0)),
        out_shape=jax.ShapeDtypeStruct((ntok, d), F32),
        scratch_shapes=[pltpu.VMEM((d, tt), F32), pltpu.VMEM((PEER_EXPERT_BLK, tt), BF16)],
        compiler_params=_cp("parallel", "arbitrary"),
        name="peer_dense",
    )(hmt, ub, vt, n1, e1, r2, e2)


def _peer_res_kernel(h_ref, f_ref, mod_ref, lng_ref, lnb_ref, o_ref, *, alpha):
    mod = mod_ref[0]
    o_ref[0] = _ln(alpha * h_ref[0] + mod[5:6] * f_ref[0], lng_ref[...], lnb_ref[...])


def _peer_res(h, f, modt, nct, off, lng, lnb, alpha):
    b, l, d = h.shape
    tok = pl.BlockSpec((1, TOK_TILE, d), lambda bi, i: (bi, i, 0))
    full = lambda a: pl.BlockSpec(a.shape, lambda bi, i: (0,) * a.ndim)
    return pl.pallas_call(
        functools.partial(_peer_res_kernel, alpha=alpha),
        grid=(b, l // TOK_TILE),
        in_specs=[tok, tok, pl.BlockSpec((1, N_MOD, d), _mod_index(nct, off)), full(lng), full(lnb)],
        out_specs=tok,
        out_shape=jax.ShapeDtypeStruct((b, l, d), F32),
        compiler_params=_cp("parallel", "parallel"),
        name="peer_res",
    )(h, f.reshape(b, l, d), modt, lng, lnb)


def _rope_tables(seq_len, ctx_len):
    rows = seq_len // GRID_W
    row = np.repeat(np.arange(rows, dtype=np.float32), GRID_W)
    col = np.tile(np.arange(GRID_W, dtype=np.float32), rows)
    n_freq = MLA_ROPE // 4
    inv = jnp.power(ROPE_BASE, -jnp.arange(n_freq, dtype=F32) / n_freq)
    ang = jnp.concatenate([row[:, None] * inv, col[:, None] * inv], axis=-1)
    ang = jnp.concatenate([jnp.zeros((ctx_len, MLA_ROPE // 2), F32), ang], axis=0)
    cos, sin = jnp.cos(ang), jnp.sin(ang)
    l = seq_len + ctx_len
    half = MLA_ROPE // 2
    c = jnp.ones((l, LANES), F32).at[:, MLA_NOPE:MLA_NOPE + half].set(cos).at[:, MLA_NOPE + half:MLA_NOPE + 2 * half].set(cos)
    s1 = jnp.zeros((l, LANES), F32).at[:, MLA_NOPE:MLA_NOPE + half].set(-sin)
    s2 = jnp.zeros((l, LANES), F32).at[:, MLA_NOPE + half:MLA_NOPE + 2 * half].set(sin)
    return c, s1, s2


def _ev_weights(w_in, q_norm, kv_norm, w_uq, w_ukv, wa2, ba, seq_len, ctx_len):
    d = w_in.shape[0]
    widths = (MLA_Q_RANK, MLA_KV_RANK, MLA_ROPE, GLA_HEADS * GLA_DK, GLA_HEADS * GLA_DK,
              GLA_HEADS * GLA_DV, GLA_HEADS * GLA_DV, GLA_GATE_RANK, GLA_GATE_RANK)
    offs = np.cumsum((0,) + widths)
    part = lambda i: w_in[:, offs[i]:offs[i + 1]]
    kpe = jnp.zeros((d, HEAD_PAD), F32).at[:, MLA_NOPE:MLA_NOPE + MLA_ROPE].set(part(2))
    gate = jnp.zeros((d, LANES), F32).at[:, :GLA_GATE_RANK].set(part(7)).at[:, GLA_GATE_RANK:2 * GLA_GATE_RANK].set(part(8))
    win = jnp.concatenate([part(0), part(1), kpe, part(3), part(4), part(5), part(6), gate], axis=1).astype(BF16)
    hq = MLA_NOPE + MLA_ROPE
    wuq = jnp.pad(w_uq.reshape(MLA_Q_RANK, MLA_HEADS, hq), ((0, 0), (0, 0), (0, HEAD_PAD - hq)))
    wuq = wuq.reshape(MLA_Q_RANK, MLA_HEADS * HEAD_PAD).astype(BF16)
    ukv = w_ukv.reshape(MLA_KV_RANK, MLA_HEADS, MLA_NOPE + MLA_V)
    wk = jnp.pad(ukv[:, :, :MLA_NOPE], ((0, 0), (0, 0), (0, HEAD_PAD - MLA_NOPE))).reshape(MLA_KV_RANK, MLA_HEADS * HEAD_PAD)
    wv = ukv[:, :, MLA_NOPE:].reshape(MLA_KV_RANK, MLA_HEADS * MLA_V)
    wukv = jnp.concatenate([wk, wv], axis=1).astype(BF16)
    gdk = GLA_HEADS * GLA_DK
    wa2p = jnp.zeros((LANES, 2 * gdk), F32).at[:GLA_GATE_RANK, :gdk].set(wa2[0])
    wa2p = wa2p.at[GLA_GATE_RANK:2 * GLA_GATE_RANK, gdk:].set(wa2[1]).astype(BF16)
    c, s1, s2 = _rope_tables(seq_len, ctx_len)
    return dict(win=win, qn=q_norm[None, :], kvn=kv_norm[None, :], wuq=wuq, wukv=wukv, wa2=wa2p,
                ba=ba.reshape(1, 2 * gdk), rope_c=c, rope_s1=s1, rope_s2=s2)


def _od_weights(conv_w, conv_b, gate_w, gate_b, lam):
    gw = lambda dd: jnp.concatenate([gate_w[dd, 0], gate_w[dd, 1]], axis=-1).astype(BF16)
    return dict(cw=conv_w, cb=conv_b[None, :], gwf=gw(0), gbf=gate_b[0], lamf=lam[0][None, :],
                gwb=gw(1), gbb=gate_b[1], lamb=lam[1][None, :])


def kernel(x, c, ctx, c_ctx, mod_w, mod_b, ln_g, ln_b,
           ev_w_in, ev_q_norm, ev_kv_norm, ev_w_uq, ev_w_ukv, ev_gla_wa2, ev_gla_ba, ev_gla_norm, ev_w_out,
           od_w_in, od_conv_w, od_conv_b, od_gate_w, od_gate_b, od_lambda, od_w_out,
           peer_wq, peer_k1, peer_k2, peer_u, peer_v):
    b, seq_len, d = x.shape
    ctx_len = ctx.shape[1]
    depth = mod_w.shape[0]
    assert ctx_len % TOK_TILE == 0 and seq_len % TOK_TILE == 0 and seq_len % GRID_W == 0
    alpha = (2 * depth) ** 0.25

    rows = -(-(b + 1) // SUBLANES) * SUBLANES
    cc = jnp.zeros((rows, d), F32).at[:b].set(c).at[b].set(c_ctx)

    h = jnp.concatenate([ctx, x], axis=1)
    nct = ctx_len // TOK_TILE
    off = 0
    for layer in range(depth):
        last = layer == depth - 1
        j = layer // 2
        mods = _mod_vectors(cc, mod_w[layer], mod_b[layer][None, :]).reshape(rows, N_MOD, d)
        modt = jnp.stack([jnp.broadcast_to(mods[b], (b, N_MOD, d)), mods[:b]], axis=1).reshape(2 * b, N_MOD, d)
        lng = ln_g[layer][:, None, :]
        lnb = ln_b[layer][:, None, :]
        assert off == 0
        if layer % 2 == 0:
            w = _ev_weights(ev_w_in[j], ev_q_norm[j], ev_kv_norm[j], ev_w_uq[j], ev_w_ukv[j],
                            ev_gla_wa2[j], ev_gla_ba[j], seq_len, ctx_len)
            q, k, v, gq, gk, gv, gg, laf, lab = _ev_proj(h, modt, nct, w)
            att = _attention(q, k, v, nct, ctx_len)
            o_f, o_b = _gla(gq, gk, gv, laf, lab, nct)
            assert not last, "the final layer drops the context rows only after an odd layer"
            h = _ev_out(att, o_f, o_b, gg, ev_gla_norm[j][None, :], ev_w_out[j].astype(BF16), h, modt,
                        lng[0], lnb[0], nct, alpha)
        else:
            w = _od_weights(od_conv_w[j], od_conv_b[j], od_gate_w[j], od_gate_b[j], od_lambda[j])
            br, rnn = _od_proj(h, modt, nct, od_w_in[j].astype(BF16))
            yf, yb = _rglru(rnn, nct, w)
            off = nct if last else 0
            h = _od_out(br, yf, yb, od_w_out[j].astype(BF16), h, modt, lng[0], lnb[0], nct, off, alpha)
        hmt, n1, e1, r2, e2 = _peer_route(h, modt, nct, off, peer_wq[layer], peer_k1[layer], peer_k2[layer])
        f = _peer_dense(hmt, n1, e1, r2, e2, peer_u[layer], peer_v[layer])
        h = _peer_res(h, f, modt, nct, off, lng[1], lnb[1], alpha)
    return h if off else h[:, ctx_len:]
```

```python
import functools

import numpy as np
import jax
import jax.numpy as jnp
from jax import lax
from jax.experimental import pallas as pl
from jax.experimental.pallas import tpu as pltpu

F32 = jnp.float32
BF16 = jnp.bfloat16
HI = lax.Precision.HIGHEST

N_MOD = 6
LN_EPS = 1e-6
GRID_W = 64
MLA_HEADS = 8
MLA_Q_RANK = 384
MLA_KV_RANK = 256
MLA_NOPE = 64
MLA_ROPE = 32
MLA_V = 64
MLA_SCALE = (MLA_NOPE + MLA_ROPE) ** -0.5
ROPE_BASE = 10000.0
GLA_HEADS = 4
GLA_DK = 64
GLA_DV = 128
GLA_GATE_RANK = 16
GLA_TAU = 16.0
RG_WIDTH = 1280
RG_BLOCKS = 10
RG_BW = RG_WIDTH // RG_BLOCKS
RG_CONV = 4
RG_CONV_LEFT = 2
RG_C = 8.0
PEER_HEADS = 8
PEER_NK = 128
PEER_TOPK = 16
PEER_QDIM = 256
PEER_HALF = PEER_QDIM // 2

LANES = 128
SUBLANES = 8
TOK_TILE = 256
GLA_CHUNK = 64
GLA_SUB = 16
PEER_DENSE_TOK = 512
PEER_EXPERT_BLK = 1024
VMEM_LIMIT = 56 * 1024 * 1024

HEAD_PAD = LANES


def _cp(*sem):
    return pltpu.CompilerParams(dimension_semantics=sem, vmem_limit_bytes=VMEM_LIMIT)


def _gelu(x):
    return 0.5 * x * (1.0 + lax.erf(x * (2.0 ** -0.5)))


def _ln(y, g, b):
    mu = jnp.mean(y, axis=-1, keepdims=True)
    d = y - mu
    var = jnp.mean(d * d, axis=-1, keepdims=True)
    return d * lax.rsqrt(var + LN_EPS) * g + b


def _mod_index(nct, off):
    def index(b, i):
        return (2 * b + jnp.where(i + off >= nct, 1, 0), 0, 0)
    return index


def _mod_kernel(c_ref, w_ref, b_ref, o_ref):
    x = c_ref[...]
    s = x * jax.nn.sigmoid(x)
    o_ref[...] = jnp.dot(s, w_ref[...], precision=HI, preferred_element_type=F32) + b_ref[...]


def _mod_vectors(cc, w, b):
    rows, d = cc.shape
    n = w.shape[1] // d
    return pl.pallas_call(
        _mod_kernel,
        grid=(n,),
        in_specs=[pl.BlockSpec((rows, d), lambda j: (0, 0)),
                  pl.BlockSpec((d, d), lambda j: (0, j)),
                  pl.BlockSpec((1, d), lambda j: (0, j))],
        out_specs=pl.BlockSpec((rows, d), lambda j: (0, j)),
        out_shape=jax.ShapeDtypeStruct((rows, w.shape[1]), F32),
        compiler_params=_cp("arbitrary"),
        name="mod_vectors",
    )(cc, w, b)


def _rope_block(x, c, s1, s2):
    return x * c + pltpu.roll(x, LANES - MLA_ROPE // 2, axis=1) * s1 + pltpu.roll(x, MLA_ROPE // 2, axis=1) * s2


def _ev_proj_kernel(h_ref, mod_ref, win_ref, qn_ref, kvn_ref, wuq_ref, wukv_ref, wa2_ref, ba_ref,
                    rc_ref, rs1_ref, rs2_ref,
                    q_ref, k_ref, v_ref, gq_ref, gk_ref, gv_ref, gg_ref, laf_ref, lab_ref):
    x = h_ref[0]
    mod = mod_ref[0]
    xin = (x * (1.0 + mod[1:2]) + mod[0:1]).astype(BF16)
    p = jnp.dot(xin, win_ref[...], preferred_element_type=F32)
    o = 0
    cq = p[:, o:o + MLA_Q_RANK]; o += MLA_Q_RANK
    ckv = p[:, o:o + MLA_KV_RANK]; o += MLA_KV_RANK
    kpe = p[:, o:o + HEAD_PAD]; o += HEAD_PAD
    gdk = GLA_HEADS * GLA_DK
    gdv = GLA_HEADS * GLA_DV
    gq_ref[0] = p[:, o:o + gdk] * (GLA_DK ** -0.5); o += gdk
    gk_ref[0] = p[:, o:o + gdk]; o += gdk
    gv_ref[0] = p[:, o:o + gdv]; o += gdv
    gg_ref[0] = p[:, o:o + gdv]; o += gdv
    ga = p[:, o:o + LANES]

    c, s1, s2 = rc_ref[...], rs1_ref[...], rs2_ref[...]
    nq = cq * lax.rsqrt(jnp.mean(cq * cq, axis=-1, keepdims=True) + LN_EPS) * qn_ref[...]
    q = jnp.dot(nq.astype(BF16), wuq_ref[...], preferred_element_type=F32)
    nkv = ckv * lax.rsqrt(jnp.mean(ckv * ckv, axis=-1, keepdims=True) + LN_EPS) * kvn_ref[...]
    kv = jnp.dot(nkv.astype(BF16), wukv_ref[...], preferred_element_type=F32)
    kpe_r = _rope_block(kpe, c, s1, s2)
    for h in range(MLA_HEADS):
        sl = slice(h * HEAD_PAD, (h + 1) * HEAD_PAD)
        q_ref[0, :, sl] = _rope_block(q[:, sl], c, s1, s2).astype(BF16)
        k_ref[0, :, sl] = (kv[:, sl] + kpe_r).astype(BF16)
    v_ref[0] = kv[:, MLA_HEADS * HEAD_PAD:].astype(BF16)

    z = jnp.dot(ga.astype(BF16), wa2_ref[...], preferred_element_type=F32) + ba_ref[...]
    la = jax.nn.log_sigmoid(z) * (1.0 / GLA_TAU)
    laf_ref[0] = la[:, :gdk]
    lab_ref[0] = la[:, gdk:]


def _ev_proj(h, modt, nct, w):
    b, l, d = h.shape
    nt = l // TOK_TILE
    tok = lambda width: pl.BlockSpec((1, TOK_TILE, width), lambda bi, i: (bi, i, 0))
    full = lambda a: pl.BlockSpec(a.shape, lambda bi, i: (0,) * a.ndim)
    rope = pl.BlockSpec((TOK_TILE, LANES), lambda bi, i: (i, 0))
    gdk = GLA_HEADS * GLA_DK
    gdv = GLA_HEADS * GLA_DV
    widths = [(MLA_HEADS * HEAD_PAD, BF16), (MLA_HEADS * HEAD_PAD, BF16), (MLA_HEADS * MLA_V, BF16),
              (gdk, F32), (gdk, F32), (gdv, F32), (gdv, F32), (gdk, F32), (gdk, F32)]
    return pl.pallas_call(
        _ev_proj_kernel,
        grid=(b, nt),
        in_specs=[tok(d), pl.BlockSpec((1, N_MOD, d), _mod_index(nct, 0)),
                  full(w["win"]), full(w["qn"]), full(w["kvn"]), full(w["wuq"]), full(w["wukv"]),
                  full(w["wa2"]), full(w["ba"]), rope, rope, rope],
        out_specs=[tok(wd) for wd, _ in widths],
        out_shape=[jax.ShapeDtypeStruct((b, l, wd), dt) for wd, dt in widths],
        compiler_params=_cp("parallel", "parallel"),
        name="ev_proj",
    )(h, modt, w["win"], w["qn"], w["kvn"], w["wuq"], w["wukv"], w["wa2"], w["ba"],
      w["rope_c"], w["rope_s1"], w["rope_s2"])


def _attn_kernel(q_ref, k_ref, v_ref, o_ref, *, nct, ctx_len):
    def attend(nk):
        lane = lax.broadcasted_iota(jnp.int32, (TOK_TILE, LANES), 1)
        for hp in range(MLA_HEADS // 2):
            vp = v_ref[0, 0:nk, hp * LANES:(hp + 1) * LANES]
            outs = []
            for hh in range(2):
                h = 2 * hp + hh
                qh = q_ref[0, :, h * HEAD_PAD:(h + 1) * HEAD_PAD]
                kh = k_ref[0, 0:nk, h * HEAD_PAD:(h + 1) * HEAD_PAD]
                s = lax.dot_general(qh, kh, (((1,), (1,)), ((), ())), preferred_element_type=F32)
                m = jnp.max(s, axis=-1, keepdims=True)
                p = jnp.exp((s - m) * MLA_SCALE)
                den = jnp.sum(p, axis=-1, keepdims=True)
                o = jnp.dot(p.astype(BF16), vp, preferred_element_type=F32)
                outs.append(o / den)
            o_ref[0, :, hp * LANES:(hp + 1) * LANES] = jnp.where(lane < MLA_V, outs[0], outs[1]).astype(BF16)

    is_ctx = pl.program_id(1) < nct

    @pl.when(is_ctx)
    def _():
        attend(ctx_len)

    @pl.when(jnp.logical_not(is_ctx))
    def _():
        attend(k_ref.shape[1])


def _attention(q, k, v, nct, ctx_len):
    b, l, _ = q.shape
    nt = l // TOK_TILE
    return pl.pallas_call(
        functools.partial(_attn_kernel, nct=nct, ctx_len=ctx_len),
        grid=(b, nt),
        in_specs=[pl.BlockSpec((1, TOK_TILE, q.shape[2]), lambda bi, i: (bi, i, 0)),
                  pl.BlockSpec((1, l, k.shape[2]), lambda bi, i: (bi, 0, 0)),
                  pl.BlockSpec((1, l, v.shape[2]), lambda bi, i: (bi, 0, 0))],
        out_specs=pl.BlockSpec((1, TOK_TILE, v.shape[2]), lambda bi, i: (bi, i, 0)),
        out_shape=jax.ShapeDtypeStruct((b, l, v.shape[2]), BF16),
        compiler_params=_cp("parallel", "arbitrary"),
        name="mla_attention",
    )(q, k, v)


def _bwd_tile(s, nct, nt):
    return jnp.where(s < nct, nct - 1 - s, nt - 1 - (s - nct))


def _gla_chunk(q, k, v, la, st, rev):
    c = GLA_CHUNK
    nsub = c // GLA_SUB
    row = lax.broadcasted_iota(jnp.int32, (c, c), 0)
    col = lax.broadcasted_iota(jnp.int32, (c, c), 1)
    tri = (col >= row) if rev else (col <= row)
    cum = jnp.dot(tri.astype(F32), la, precision=HI, preferred_element_type=F32)
    rid = lax.broadcasted_iota(jnp.int32, (c, LANES), 0)
    lane = lax.broadcasted_iota(jnp.int32, (c, LANES), 1)
    head0 = lane < GLA_DK

    qparts, kparts, eref = [], [], jnp.zeros((c, LANES), F32)
    for jb in range(nsub):
        r = jb * GLA_SUB if rev else jb * GLA_SUB + GLA_SUB - 1
        ej = cum[r:r + 1, :]
        valid = (rid <= jb * GLA_SUB + GLA_SUB - 1) if rev else (rid >= jb * GLA_SUB)
        qparts.append(jnp.where(valid, q * jnp.exp(jnp.where(valid, cum - ej, 0.0)), 0.0))
        inblk = (rid >= jb * GLA_SUB) & (rid < (jb + 1) * GLA_SUB)
        eref = jnp.where(inblk, ej, eref)
    kt = k * jnp.exp(eref - cum)
    for jb in range(nsub):
        inblk = (rid >= jb * GLA_SUB) & (rid < (jb + 1) * GLA_SUB)
        kparts.append(jnp.where(inblk, kt, 0.0))
    qbig = jnp.concatenate(qparts, axis=1)
    kbig = jnp.concatenate(kparts, axis=1).astype(BF16)
    h0big = lax.broadcasted_iota(jnp.int32, (c, nsub * LANES), 1) % LANES < GLA_DK
    qst = jnp.concatenate([jnp.where(h0big, qbig, 0.0), jnp.where(h0big, 0.0, qbig)], axis=0).astype(BF16)
    sc = lax.dot_general(qst, kbig, (((1,), (1,)), ((), ())), preferred_element_type=F32)
    ri = lax.broadcasted_iota(jnp.int32, (2 * c, c), 0) % c
    ci = lax.broadcasted_iota(jnp.int32, (2 * c, c), 1)
    sc = jnp.where((ci >= ri) if rev else (ci <= ri), sc, 0.0)
    vb = v.astype(BF16)
    intra = jnp.dot(sc.astype(BF16), vb, preferred_element_type=F32)
    qs = (q * jnp.exp(cum)).astype(BF16)
    inter = lax.dot_general(qs, st.astype(BF16), (((1,), (1,)), ((), ())), preferred_element_type=F32)
    out = jnp.concatenate([intra[0:c, 0:GLA_DV] + inter[:, 0:GLA_DV],
                           intra[c:2 * c, GLA_DV:2 * GLA_DV] + inter[:, GLA_DV:2 * GLA_DV]], axis=1)
    last = cum[0:1, :] if rev else cum[c - 1:c, :]
    kd = (k * jnp.exp(last - cum)).astype(BF16)
    upd = lax.dot_general(vb, kd, (((0,), (0,)), ((), ())), preferred_element_type=F32)
    srow = lax.broadcasted_iota(jnp.int32, (2 * GLA_DV, LANES), 0)
    slane = lax.broadcasted_iota(jnp.int32, (2 * GLA_DV, LANES), 1)
    diag = (srow < GLA_DV) == (slane < GLA_DK)
    st_new = jnp.where(diag, st * jnp.exp(last) + upd, 0.0)
    return out, st_new


def _gla_kernel(qf_ref, kf_ref, vf_ref, laf_ref, qb_ref, kb_ref, vb_ref, lab_ref, of_ref, ob_ref, stf_ref, stb_ref):
    @pl.when(pl.program_id(1) == 0)
    def _():
        stf_ref[...] = jnp.zeros_like(stf_ref)
        stb_ref[...] = jnp.zeros_like(stb_ref)

    nchunk = TOK_TILE // GLA_CHUNK
    npair = GLA_HEADS // 2

    def body(ci, carry):
        rf = pl.multiple_of(ci * GLA_CHUNK, GLA_CHUNK)
        rb = pl.multiple_of((nchunk - 1 - ci) * GLA_CHUNK, GLA_CHUNK)
        for pr in range(npair):
            ks = slice(pr * LANES, (pr + 1) * LANES)
            vs = slice(pr * 2 * GLA_DV, (pr + 1) * 2 * GLA_DV)
            o, st = _gla_chunk(qf_ref[0, pl.ds(rf, GLA_CHUNK), ks], kf_ref[0, pl.ds(rf, GLA_CHUNK), ks],
                               vf_ref[0, pl.ds(rf, GLA_CHUNK), vs], laf_ref[0, pl.ds(rf, GLA_CHUNK), ks],
                               stf_ref[pr], False)
            of_ref[0, pl.ds(rf, GLA_CHUNK), vs] = o
            stf_ref[pr] = st
            o, st = _gla_chunk(qb_ref[0, pl.ds(rb, GLA_CHUNK), ks], kb_ref[0, pl.ds(rb, GLA_CHUNK), ks],
                               vb_ref[0, pl.ds(rb, GLA_CHUNK), vs], lab_ref[0, pl.ds(rb, GLA_CHUNK), ks],
                               stb_ref[pr], True)
            ob_ref[0, pl.ds(rb, GLA_CHUNK), vs] = o
            stb_ref[pr] = st
        return carry

    lax.fori_loop(0, nchunk, body, 0)


def _gla(gq, gk, gv, laf, lab, nct):
    b, l, dk = gq.shape
    dv = gv.shape[2]
    nt = l // TOK_TILE
    fwd = lambda width: pl.BlockSpec((1, TOK_TILE, width), lambda bi, s: (bi, s, 0))
    bwd = lambda width: pl.BlockSpec((1, TOK_TILE, width), lambda bi, s: (bi, _bwd_tile(s, nct, nt), 0))
    npair = GLA_HEADS // 2
    return pl.pallas_call(
        _gla_kernel,
        grid=(b, nt),
        in_specs=[fwd(dk), fwd(dk), fwd(dv), fwd(dk), bwd(dk), bwd(dk), bwd(dv), bwd(dk)],
        out_specs=[fwd(dv), bwd(dv)],
        out_shape=[jax.ShapeDtypeStruct((b, l, dv), F32)] * 2,
        scratch_shapes=[pltpu.VMEM((npair, 2 * GLA_DV, LANES), F32)] * 2,
        compiler_params=_cp("parallel", "arbitrary"),
        name="gla_scan",
    )(gq, gk, gv, laf, gq, gk, gv, lab)


def _ev_out_kernel(att_ref, of_ref, ob_ref, gg_ref, gn_ref, wout_ref, h_ref, mod_ref, lng_ref, lnb_ref, o_ref, *, alpha):
    o = of_ref[0] + ob_ref[0]
    g = gg_ref[0]
    parts = []
    for hd in range(GLA_HEADS):
        sl = slice(hd * GLA_DV, (hd + 1) * GLA_DV)
        oh = o[:, sl]
        y = oh * lax.rsqrt(jnp.mean(oh * oh, axis=-1, keepdims=True) + LN_EPS) * gn_ref[:, sl]
        gh = g[:, sl]
        parts.append((y * (gh * jax.nn.sigmoid(gh))).astype(BF16))
    gla = jnp.concatenate(parts, axis=1)
    na = att_ref.shape[2]
    mix = (jnp.dot(att_ref[0], wout_ref[0:na, :], preferred_element_type=F32)
           + jnp.dot(gla, wout_ref[na:, :], preferred_element_type=F32))
    mod = mod_ref[0]
    o_ref[0] = _ln(alpha * h_ref[0] + mod[2:3] * mix, lng_ref[...], lnb_ref[...])


def _ev_out(att, o_f, o_b, gg, gn, wout, h, modt, lng, lnb, nct, alpha):
    b, l, d = h.shape
    nt = l // TOK_TILE
    tok = lambda a: pl.BlockSpec((1, TOK_TILE, a.shape[2]), lambda bi, i: (bi, i, 0))
    full = lambda a: pl.BlockSpec(a.shape, lambda bi, i: (0,) * a.ndim)
    return pl.pallas_call(
        functools.partial(_ev_out_kernel, alpha=alpha),
        grid=(b, nt),
        in_specs=[tok(att), tok(o_f), tok(o_b), tok(gg), full(gn), full(wout), tok(h),
                  pl.BlockSpec((1, N_MOD, d), _mod_index(nct, 0)), full(lng), full(lnb)],
        out_specs=tok(h),
        out_shape=jax.ShapeDtypeStruct((b, l, d), F32),
        compiler_params=_cp("parallel", "parallel"),
        name="ev_out",
    )(att, o_f, o_b, gg, gn, wout, h, modt, lng, lnb)


def _od_proj_kernel(h_ref, mod_ref, win_ref, br_ref, rnn_ref):
    mod = mod_ref[0]
    xin = (h_ref[0] * (1.0 + mod[1:2]) + mod[0:1]).astype(BF16)
    p = jnp.dot(xin, win_ref[...], preferred_element_type=F32)
    br_ref[0] = p[:, :RG_WIDTH]
    rnn_ref[0] = p[:, RG_WIDTH:]


def _od_proj(h, modt, nct, win):
    b, l, d = h.shape
    nt = l // TOK_TILE
    tok = lambda width: pl.BlockSpec((1, TOK_TILE, width), lambda bi, i: (bi, i, 0))
    return pl.pallas_call(
        _od_proj_kernel,
        grid=(b, nt),
        in_specs=[tok(d), pl.BlockSpec((1, N_MOD, d), _mod_index(nct, 0)),
                  pl.BlockSpec(win.shape, lambda bi, i: (0, 0))],
        out_specs=[tok(RG_WIDTH), tok(RG_WIDTH)],
        out_shape=[jax.ShapeDtypeStruct((b, l, RG_WIDTH), F32)] * 2,
        compiler_params=_cp("parallel", "parallel"),
        name="od_proj",
    )(h, modt, win)


def _rg_prepare(z_ref, pv_ref, nx_ref, prev_zero, next_zero, cw_ref, cb_ref, gw_ref, gb_ref, lam_ref, a_ref, x_ref):
    t = TOK_TILE
    z = z_ref[0]
    pv = jnp.where(prev_zero, 0.0, pv_ref[0])
    nx = jnp.where(next_zero, 0.0, nx_ref[0])
    ext = jnp.concatenate([pv, z, nx], axis=0)
    base = SUBLANES - RG_CONV_LEFT
    y = cb_ref[...] + ext[base:base + t] * cw_ref[0:1, :]
    for tap in range(1, RG_CONV):
        y = y + ext[base + tap:base + tap + t] * cw_ref[tap:tap + 1, :]
    for n in range(RG_BLOCKS):
        sl = slice(n * RG_BW, (n + 1) * RG_BW)
        yb = y[:, sl]
        ri = jnp.dot(yb.astype(BF16), gw_ref[n], preferred_element_type=F32)
        r = jax.nn.sigmoid(ri[:, :RG_BW] + gb_ref[0:1, sl])
        i = jax.nn.sigmoid(ri[:, RG_BW:] + gb_ref[1:2, sl])
        log_a = (-RG_C) * jax.nn.softplus(-lam_ref[:, sl]) * r
        th = jnp.tanh(log_a)
        a_ref[:, sl] = jnp.exp(log_a)
        x_ref[:, sl] = jnp.sqrt(-2.0 * th / (1.0 - th)) * (i * yb)


def _rglru_kernel(zf_ref, pf_ref, nf_ref, zb_ref, pb_ref, nb_ref, cw_ref, cb_ref,
                  gwf_ref, gbf_ref, lamf_ref, gwb_ref, gbb_ref, lamb_ref,
                  yf_ref, yb_ref, af_ref, xf_ref, ab_ref, xb_ref, hf_ref, hb_ref, *, nct, nt):
    s = pl.program_id(1)

    @pl.when(s == 0)
    def _():
        hf_ref[...] = jnp.zeros_like(hf_ref)
        hb_ref[...] = jnp.zeros_like(hb_ref)

    def edges(j):
        return (j == 0) | (j == nct), (j == nct - 1) | (j == nt - 1)

    pz, nz = edges(s)
    _rg_prepare(zf_ref, pf_ref, nf_ref, pz, nz, cw_ref, cb_ref, gwf_ref, gbf_ref, lamf_ref, af_ref, xf_ref)
    pz, nz = edges(_bwd_tile(s, nct, nt))
    _rg_prepare(zb_ref, pb_ref, nb_ref, pz, nz, cw_ref, cb_ref, gwb_ref, gbb_ref, lamb_ref, ab_ref, xb_ref)

    def body(kk, carry):
        hf, hb = carry
        tf = kk
        tb = TOK_TILE - 1 - kk
        hf = af_ref[pl.ds(tf, 1), :] * hf + xf_ref[pl.ds(tf, 1), :]
        yf_ref[0, pl.ds(tf, 1), :] = hf
        hb = ab_ref[pl.ds(tb, 1), :] * hb + xb_ref[pl.ds(tb, 1), :]
        yb_ref[0, pl.ds(tb, 1), :] = hb
        return hf, hb

    hf, hb = lax.fori_loop(0, TOK_TILE, body, (hf_ref[...], hb_ref[...]), unroll=8)
    hf_ref[...] = hf
    hb_ref[...] = hb


def _rglru(rnn, nct, w):
    b, l, wd = rnn.shape
    nt = l // TOK_TILE
    per = TOK_TILE // SUBLANES
    nh = l // SUBLANES
    bt = lambda s: _bwd_tile(s, nct, nt)
    tile = lambda f: pl.BlockSpec((1, TOK_TILE, wd), lambda bi, s: (bi, f(s), 0))
    prev = lambda f: pl.BlockSpec((1, SUBLANES, wd), lambda bi, s: (bi, jnp.maximum(f(s) * per - 1, 0), 0))
    nxt = lambda f: pl.BlockSpec((1, SUBLANES, wd), lambda bi, s: (bi, jnp.minimum((f(s) + 1) * per, nh - 1), 0))
    full = lambda a: pl.BlockSpec(a.shape, lambda bi, s: (0,) * a.ndim)
    ident = lambda s: s
    vm = lambda rows: pltpu.VMEM((rows, wd), F32)
    return pl.pallas_call(
        functools.partial(_rglru_kernel, nct=nct, nt=nt),
        grid=(b, nt),
        in_specs=[tile(ident), prev(ident), nxt(ident), tile(bt), prev(bt), nxt(bt),
                  full(w["cw"]), full(w["cb"]), full(w["gwf"]), full(w["gbf"]), full(w["lamf"]),
                  full(w["gwb"]), full(w["gbb"]), full(w["lamb"])],
        out_specs=[tile(ident), tile(bt)],
        out_shape=[jax.ShapeDtypeStruct((b, l, wd), F32)] * 2,
        scratch_shapes=[vm(TOK_TILE), vm(TOK_TILE), vm(TOK_TILE), vm(TOK_TILE), vm(1), vm(1)],
        compiler_params=_cp("parallel", "arbitrary"),
        name="rglru_scan",
    )(rnn, rnn, rnn, rnn, rnn, rnn, w["cw"], w["cb"], w["gwf"], w["gbf"], w["lamf"], w["gwb"], w["gbb"], w["lamb"])


def _od_out_kernel(br_ref, yf_ref, yb_ref, wout_ref, h_ref, mod_ref, lng_ref, lnb_ref, o_ref, *, alpha):
    m = (_gelu(br_ref[0]) * (yf_ref[0] + yb_ref[0])).astype(BF16)
    mix = jnp.dot(m, wout_ref[...], preferred_element_type=F32)
    mod = mod_ref[0]
    o_ref[0] = _ln(alpha * h_ref[0] + mod[2:3] * mix, lng_ref[...], lnb_ref[...])


def _od_out(br, yf, yb, wout, h, modt, lng, lnb, nct, off, alpha):
    b, l, d = h.shape
    nt = l // TOK_TILE - off
    tok = lambda a: pl.BlockSpec((1, TOK_TILE, a.shape[2]), lambda bi, i: (bi, i + off, 0))
    full = lambda a: pl.BlockSpec(a.shape, lambda bi, i: (0,) * a.ndim)
    return pl.pallas_call(
        functools.partial(_od_out_kernel, alpha=alpha),
        grid=(b, nt),
        in_specs=[tok(br), tok(yf), tok(yb), full(wout), tok(h),
                  pl.BlockSpec((1, N_MOD, d), _mod_index(nct, off)), full(lng), full(lnb)],
        out_specs=pl.BlockSpec((1, TOK_TILE, d), lambda bi, i: (bi, i, 0)),
        out_shape=jax.ShapeDtypeStruct((b, nt * TOK_TILE, d), F32),
        compiler_params=_cp("parallel", "parallel"),
        name="od_out",
    )(br, yf, yb, wout, h, modt, lng, lnb)


def _top16_rows(s, vals_ref):
    nk = s.shape[0]
    rid = lax.broadcasted_iota(jnp.int32, s.shape, 0).astype(F32)
    rank = jnp.full(s.shape, float(PEER_TOPK), F32)
    left = s
    for r in range(PEER_TOPK):
        m = jnp.max(left, axis=0, keepdims=True)
        pos = jnp.min(jnp.where(left == m, rid, float(nk)), axis=0, keepdims=True)
        hit = rid == pos
        vals_ref[r:r + 1, :] = m
        rank = jnp.where(hit, float(r), rank)
        left = jnp.where(hit, -jnp.inf, left)
    ex = jnp.where(rank < float(PEER_TOPK), jnp.exp(s - vals_ref[0:1, :]), 0.0)
    return rank, ex


def _cand_sums(t1, t2):
    blocks = [t1[0:1] + t2[0:16]]
    for a in range(1, 8):
        blocks.append(t1[a:a + 1] + t2[0:8])
    blocks.append(t1[8:16] + t2[0:1])
    return jnp.concatenate(blocks, axis=0)


def _selected_counts(t1, t2, flat):
    cs0 = _cand_sums(t1, t2)
    cs = cs0
    sel = jnp.zeros(cs.shape, F32)
    for _ in range(PEER_TOPK):
        m = jnp.max(cs, axis=0, keepdims=True)
        pos = jnp.min(jnp.where(cs == m, flat, float(PEER_TOPK * PEER_TOPK)), axis=0, keepdims=True)
        hit = flat == pos
        sel = jnp.where(hit, 1.0, sel)
        cs = jnp.where(hit, -jnp.inf, cs)
    z = jnp.sum(sel * jnp.exp(cs0 - cs0[0:1]), axis=0, keepdims=True)
    counts = [jnp.sum(sel[0:16], axis=0, keepdims=True)]
    for a in range(1, 8):
        counts.append(jnp.sum(sel[8 + 8 * a:16 + 8 * a], axis=0, keepdims=True))
    for a in range(8, 16):
        counts.append(sel[64 + a:65 + a])
    return counts, z


def _cand_flat_index(ntok):
    a16 = lax.broadcasted_iota(jnp.int32, (16, ntok), 0)
    a8 = lax.broadcasted_iota(jnp.int32, (8, ntok), 0)
    blocks = [a16]
    for a in range(1, 8):
        blocks.append(a * PEER_TOPK + a8)
    blocks.append((a8 + 8) * PEER_TOPK)
    return jnp.concatenate(blocks, axis=0).astype(F32)


def _split_bf16(x):
    hi = x.astype(BF16)
    return hi, (x - hi.astype(F32)).astype(BF16)


def _dot3(a, b, dims):
    dg = lambda x, y: lax.dot_general(x, y, dims, preferred_element_type=F32)
    return dg(a[0], b[0]) + (dg(a[0], b[1]) + dg(a[1], b[0]))


def _peer_route_kernel(h_ref, mod_ref, wq_ref, k_ref, hmt_ref, n1_ref, e1_ref, r2_ref, e2_ref, t1_ref, t2_ref):
    mod = mod_ref[0]
    hm = h_ref[0] * (1.0 + mod[4:5]) + mod[3:4]
    hmt_ref[...] = hm.T.astype(BF16)
    q = _dot3(_split_bf16(hm), (wq_ref[0], wq_ref[1]), (((1,), (0,)), ((), ())))
    flat = _cand_flat_index(LANES)
    nn = (((1,), (1,)), ((), ()))
    qs = _split_bf16(q)
    for h in range(PEER_HEADS):
        c1 = slice(h * PEER_QDIM, h * PEER_QDIM + PEER_HALF)
        c2 = slice(h * PEER_QDIM + PEER_HALF, (h + 1) * PEER_QDIM)
        s1 = _dot3((k_ref[0], k_ref[1]), (qs[0][:, c1], qs[1][:, c1]), nn)
        s2 = _dot3((k_ref[2], k_ref[3]), (qs[0][:, c2], qs[1][:, c2]), nn)
        for tc in range(hm.shape[0] // LANES):
            cols = slice(tc * LANES, (tc + 1) * LANES)
            r1, e1 = _top16_rows(s1[:, cols], t1_ref)
            r2, e2 = _top16_rows(s2[:, cols], t2_ref)
            counts, z = _selected_counts(t1_ref[...], t2_ref[...], flat)
            n1 = jnp.zeros(r1.shape, F32)
            for a in range(PEER_TOPK):
                n1 = jnp.where(r1 == float(a), counts[a], n1)
            n1_ref[h, :, cols] = n1
            e1_ref[h, :, cols] = e1 / z
            r2_ref[h, :, cols] = r2
            e2_ref[h, :, cols] = e2


def _peer_route(h, modt, nct, off, wq, k1, k2):
    b, l, d = h.shape
    nt = l // TOK_TILE
    ntok = b * l
    full = lambda a: pl.BlockSpec(a.shape, lambda bi, i: (0,) * a.ndim)
    top_vals = pltpu.VMEM((PEER_TOPK, LANES), F32)
    wq = jnp.stack(_split_bf16(wq))
    keys = jnp.stack(_split_bf16(k1) + _split_bf16(k2))
    per_key = pl.BlockSpec((PEER_HEADS, PEER_NK, TOK_TILE), lambda bi, i: (0, 0, bi * nt + i))
    per_key_shape = jax.ShapeDtypeStruct((PEER_HEADS, PEER_NK, ntok), F32)
    return pl.pallas_call(
        _peer_route_kernel,
        grid=(b, nt),
        in_specs=[pl.BlockSpec((1, TOK_TILE, d), lambda bi, i: (bi, i, 0)),
                  pl.BlockSpec((1, N_MOD, d), _mod_index(nct, off)), full(wq), full(keys)],
        out_specs=[pl.BlockSpec((d, TOK_TILE), lambda bi, i: (0, bi * nt + i))] + [per_key] * 4,
        out_shape=[jax.ShapeDtypeStruct((d, ntok), BF16)] + [per_key_shape] * 4,
        scratch_shapes=[top_vals, top_vals],
        compiler_params=_cp("parallel", "parallel"),
        name="peer_route",
    )(h, modt, wq, keys)


def _peer_dense_kernel(hmt_ref, u_ref, vt_ref, n1_ref, e1_ref, r2_ref, e2_ref, o_ref, acc_ref, w_ref):
    j = pl.program_id(1)

    @pl.when(j == 0)
    def _():
        acc_ref[...] = jnp.zeros_like(acc_ref)

    hmt = hmt_ref[...]
    per_step = u_ref.shape[0] // PEER_NK
    for al in range(per_step):
        a = j * per_step + al
        rows = slice(al * PEER_NK, (al + 1) * PEER_NK)
        s = jnp.dot(u_ref[rows, :], hmt, preferred_element_type=F32)
        n1 = [n1_ref[h, pl.ds(a, 1), :] for h in range(PEER_HEADS)]
        e1 = [e1_ref[h, pl.ds(a, 1), :] for h in range(PEER_HEADS)]
        for tc in range(s.shape[1] // LANES):
            cols = slice(tc * LANES, (tc + 1) * LANES)
            g = jnp.zeros((PEER_NK, LANES), F32)
            for h in range(PEER_HEADS):
                g = g + jnp.where(r2_ref[h, :, cols] < n1[h][:, cols], e2_ref[h, :, cols] * e1[h][:, cols], 0.0)
            w_ref[rows, cols] = (_gelu(s[:, cols]) * g).astype(BF16)
    acc_ref[...] += jnp.dot(vt_ref[...], w_ref[...], preferred_element_type=F32)

    @pl.when(j == pl.num_programs(1) - 1)
    def _():
        o_ref[...] = acc_ref[...].T


def _peer_dense(hmt, n1, e1, r2, e2, u, v):
    d, ntok = hmt.shape
    nexp = u.shape[0]
    tt = PEER_DENSE_TOK if ntok % PEER_DENSE_TOK == 0 else TOK_TILE
    ub = u.astype(BF16)
    vt = v.T.astype(BF16)
    per_key = pl.BlockSpec((PEER_HEADS, PEER_NK, tt), lambda i, j: (0, 0, i))
    return pl.pallas_call(
        _peer_dense_kernel,
        grid=(ntok // tt, nexp // PEER_EXPERT_BLK),
        in_specs=[pl.BlockSpec((d, tt), lambda i, j: (0, i)),
                  pl.BlockSpec((PEER_EXPERT_BLK, d), lambda i, j: (j, 0)),
                  pl.BlockSpec((d, PEER_EXPERT_BLK), lambda i, j: (0, j)),
                  per_key, per_key, per_key, per_key],
        out_specs=pl.BlockSpec((tt, d), lambda i, j: (i, 0)),
        out_shape=jax.ShapeDtypeStruct((ntok, d), F32),
        scratch_shapes=[pltpu.VMEM((d, tt), F32), pltpu.VMEM((PEER_EXPERT_BLK, tt), BF16)],
        compiler_params=_cp("parallel", "arbitrary"),
        name="peer_dense",
    )(hmt, ub, vt, n1, e1, r2, e2)


def _peer_res_kernel(h_ref, f_ref, mod_ref, lng_ref, lnb_ref, o_ref, *, alpha):
    mod = mod_ref[0]
    o_ref[0] = _ln(alpha * h_ref[0] + mod[5:6] * f_ref[0], lng_ref[...], lnb_ref[...])


def _peer_res(h, f, modt, nct, off, lng, lnb, alpha):
    b, l, d = h.shape
    tok = pl.BlockSpec((1, TOK_TILE, d), lambda bi, i: (bi, i, 0))
    full = lambda a: pl.BlockSpec(a.shape, lambda bi, i: (0,) * a.ndim)
    return pl.pallas_call(
        functools.partial(_peer_res_kernel, alpha=alpha),
        grid=(b, l // TOK_TILE),
        in_specs=[tok, tok, pl.BlockSpec((1, N_MOD, d), _mod_index(nct, off)), full(lng), full(lnb)],
        out_specs=tok,
        out_shape=jax.ShapeDtypeStruct((b, l, d), F32),
        compiler_params=_cp("parallel", "parallel"),
        name="peer_res",
    )(h, f.reshape(b, l, d), modt, lng, lnb)


def _rope_tables(seq_len, ctx_len):
    rows = seq_len // GRID_W
    row = np.repeat(np.arange(rows, dtype=np.float32), GRID_W)
    col = np.tile(np.arange(GRID_W, dtype=np.float32), rows)
    n_freq = MLA_ROPE // 4
    inv = jnp.power(ROPE_BASE, -jnp.arange(n_freq, dtype=F32) / n_freq)
    ang = jnp.concatenate([row[:, None] * inv, col[:, None] * inv], axis=-1)
    ang = jnp.concatenate([jnp.zeros((ctx_len, MLA_ROPE // 2), F32), ang], axis=0)
    cos, sin = jnp.cos(ang), jnp.sin(ang)
    l = seq_len + ctx_len
    half = MLA_ROPE // 2
    c = jnp.ones((l, LANES), F32).at[:, MLA_NOPE:MLA_NOPE + half].set(cos).at[:, MLA_NOPE + half:MLA_NOPE + 2 * half].set(cos)
    s1 = jnp.zeros((l, LANES), F32).at[:, MLA_NOPE:MLA_NOPE + half].set(-sin)
    s2 = jnp.zeros((l, LANES), F32).at[:, MLA_NOPE + half:MLA_NOPE + 2 * half].set(sin)
    return c, s1, s2


def _ev_weights(w_in, q_norm, kv_norm, w_uq, w_ukv, wa2, ba, seq_len, ctx_len):
    d = w_in.shape[0]
    widths = (MLA_Q_RANK, MLA_KV_RANK, MLA_ROPE, GLA_HEADS * GLA_DK, GLA_HEADS * GLA_DK,
              GLA_HEADS * GLA_DV, GLA_HEADS * GLA_DV, GLA_GATE_RANK, GLA_GATE_RANK)
    offs = np.cumsum((0,) + widths)
    part = lambda i: w_in[:, offs[i]:offs[i + 1]]
    kpe = jnp.zeros((d, HEAD_PAD), F32).at[:, MLA_NOPE:MLA_NOPE + MLA_ROPE].set(part(2))
    gate = jnp.zeros((d, LANES), F32).at[:, :GLA_GATE_RANK].set(part(7)).at[:, GLA_GATE_RANK:2 * GLA_GATE_RANK].set(part(8))
    win = jnp.concatenate([part(0), part(1), kpe, part(3), part(4), part(5), part(6), gate], axis=1).astype(BF16)
    hq = MLA_NOPE + MLA_ROPE
    wuq = jnp.pad(w_uq.reshape(MLA_Q_RANK, MLA_HEADS, hq), ((0, 0), (0, 0), (0, HEAD_PAD - hq)))
    wuq = wuq.reshape(MLA_Q_RANK, MLA_HEADS * HEAD_PAD).astype(BF16)
    ukv = w_ukv.reshape(MLA_KV_RANK, MLA_HEADS, MLA_NOPE + MLA_V)
    wk = jnp.pad(ukv[:, :, :MLA_NOPE], ((0, 0), (0, 0), (0, HEAD_PAD - MLA_NOPE))).reshape(MLA_KV_RANK, MLA_HEADS * HEAD_PAD)
    wv = ukv[:, :, MLA_NOPE:].reshape(MLA_KV_RANK, MLA_HEADS * MLA_V)
    wukv = jnp.concatenate([wk, wv], axis=1).astype(BF16)
    gdk = GLA_HEADS * GLA_DK
    wa2p = jnp.zeros((LANES, 2 * gdk), F32).at[:GLA_GATE_RANK, :gdk].set(wa2[0])
    wa2p = wa2p.at[GLA_GATE_RANK:2 * GLA_GATE_RANK, gdk:].set(wa2[1]).astype(BF16)
    c, s1, s2 = _rope_tables(seq_len, ctx_len)
    return dict(win=win, qn=q_norm[None, :], kvn=kv_norm[None, :], wuq=wuq, wukv=wukv, wa2=wa2p,
                ba=ba.reshape(1, 2 * gdk), rope_c=c, rope_s1=s1, rope_s2=s2)


def _od_weights(conv_w, conv_b, gate_w, gate_b, lam):
    gw = lambda dd: jnp.concatenate([gate_w[dd, 0], gate_w[dd, 1]], axis=-1).astype(BF16)
    return dict(cw=conv_w, cb=conv_b[None, :], gwf=gw(0), gbf=gate_b[0], lamf=lam[0][None, :],
                gwb=gw(1), gbb=gate_b[1], lamb=lam[1][None, :])


def kernel(x, c, ctx, c_ctx, mod_w, mod_b, ln_g, ln_b,
           ev_w_in, ev_q_norm, ev_kv_norm, ev_w_uq, ev_w_ukv, ev_gla_wa2, ev_gla_ba, ev_gla_norm, ev_w_out,
           od_w_in, od_conv_w, od_conv_b, od_gate_w, od_gate_b, od_lambda, od_w_out,
           peer_wq, peer_k1, peer_k2, peer_u, peer_v):
    b, seq_len, d = x.shape
    ctx_len = ctx.shape[1]
    depth = mod_w.shape[0]
    assert ctx_len % TOK_TILE == 0 and seq_len % TOK_TILE == 0 and seq_len % GRID_W == 0
    alpha = (2 * depth) ** 0.25

    rows = -(-(b + 1) // SUBLANES) * SUBLANES
    cc = jnp.zeros((rows, d), F32).at[:b].set(c).at[b].set(c_ctx)

    h = jnp.concatenate([ctx, x], axis=1)
    nct = ctx_len // TOK_TILE
    off = 0
    for layer in range(depth):
        last = layer == depth - 1
        j = layer // 2
        mods = _mod_vectors(cc, mod_w[layer], mod_b[layer][None, :]).reshape(rows, N_MOD, d)
        modt = jnp.stack([jnp.broadcast_to(mods[b], (b, N_MOD, d)), mods[:b]], axis=1).reshape(2 * b, N_MOD, d)
        lng = ln_g[layer][:, None, :]
        lnb = ln_b[layer][:, None, :]
        assert off == 0
        if layer % 2 == 0:
            w = _ev_weights(ev_w_in[j], ev_q_norm[j], ev_kv_norm[j], ev_w_uq[j], ev_w_ukv[j],
                            ev_gla_wa2[j], ev_gla_ba[j], seq_len, ctx_len)
            q, k, v, gq, gk, gv, gg, laf, lab = _ev_proj(h, modt, nct, w)
            att = _attention(q, k, v, nct, ctx_len)
            o_f, o_b = _gla(gq, gk, gv, laf, lab, nct)
            assert not last, "the final layer drops the context rows only after an odd layer"
            h = _ev_out(att, o_f, o_b, gg, ev_gla_norm[j][None, :], ev_w_out[j].astype(BF16), h, modt,
                        lng[0], lnb[0], nct, alpha)
        else:
            w = _od_weights(od_conv_w[j], od_conv_b[j], od_gate_w[j], od_gate_b[j], od_lambda[j])
            br, rnn = _od_proj(h, modt, nct, od_w_in[j].astype(BF16))
            yf, yb = _rglru(rnn, nct, w)
            off = nct if last else 0
            h = _od_out(br, yf, yb, od_w_out[j].astype(BF16), h, modt, lng[0], lnb[0], nct, off, alpha)
        hmt, n1, e1, r2, e2 = _peer_route(h, modt, nct, off, peer_wq[layer], peer_k1[layer], peer_k2[layer])
        f = _peer_dense(hmt, n1, e1, r2, e2, peer_u[layer], peer_v[layer])
        h = _peer_res(h, f, modt, nct, off, lng[1], lnb[1], alpha)
    return h if off else h[:, ctx_len:]
```

```python
import functools

import numpy as np
import jax
import jax.numpy as jnp
from jax import lax
from jax.experimental import pallas as pl
from jax.experimental.pallas import tpu as pltpu

F32 = jnp.float32
BF16 = jnp.bfloat16
HI = lax.Precision.HIGHEST

N_MOD = 6
LN_EPS = 1e-6
GRID_W = 64
MLA_HEADS = 8
MLA_Q_RANK = 384
MLA_KV_RANK = 256
MLA_NOPE = 64
MLA_ROPE = 32
MLA_V = 64
MLA_SCALE = (MLA_NOPE + MLA_ROPE) ** -0.5
ROPE_BASE = 10000.0
GLA_HEADS = 4
GLA_DK = 64
GLA_DV = 128
GLA_GATE_RANK = 16
GLA_TAU = 16.0
RG_WIDTH = 1280
RG_BLOCKS = 10
RG_BW = RG_WIDTH // RG_BLOCKS
RG_CONV = 4
RG_CONV_LEFT = 2
RG_C = 8.0
PEER_HEADS = 8
PEER_NK = 128
PEER_TOPK = 16
PEER_QDIM = 256
PEER_HALF = PEER_QDIM // 2

LANES = 128
SUBLANES = 8
TOK_TILE = 256
GLA_CHUNK = 64
GLA_SUB = 16
PEER_DENSE_TOK = 512
PEER_EXPERT_BLK = 1024
VMEM_LIMIT = 56 * 1024 * 1024

HEAD_PAD = LANES


def _cp(*sem):
    return pltpu.CompilerParams(dimension_semantics=sem, vmem_limit_bytes=VMEM_LIMIT)


def _gelu(x):
    return 0.5 * x * (1.0 + lax.erf(x * (2.0 ** -0.5)))


def _ln(y, g, b):
    mu = jnp.mean(y, axis=-1, keepdims=True)
    d = y - mu
    var = jnp.mean(d * d, axis=-1, keepdims=True)
    return d * lax.rsqrt(var + LN_EPS) * g + b


def _mod_index(nct, off):
    def index(b, i):
        return (2 * b + jnp.where(i + off >= nct, 1, 0), 0, 0)
    return index


def _mod_kernel(c_ref, w_ref, b_ref, o_ref):
    x = c_ref[...]
    s = x * jax.nn.sigmoid(x)
    o_ref[...] = jnp.dot(s, w_ref[...], precision=HI, preferred_element_type=F32) + b_ref[...]


def _mod_vectors(cc, w, b):
    rows, d = cc.shape
    n = w.shape[1] // d
    return pl.pallas_call(
        _mod_kernel,
        grid=(n,),
        in_specs=[pl.BlockSpec((rows, d), lambda j: (0, 0)),
                  pl.BlockSpec((d, d), lambda j: (0, j)),
                  pl.BlockSpec((1, d), lambda j: (0, j))],
        out_specs=pl.BlockSpec((rows, d), lambda j: (0, j)),
        out_shape=jax.ShapeDtypeStruct((rows, w.shape[1]), F32),
        compiler_params=_cp("arbitrary"),
        name="mod_vectors",
    )(cc, w, b)


def _rope_block(x, c, s1, s2):
    return x * c + pltpu.roll(x, LANES - MLA_ROPE // 2, axis=1) * s1 + pltpu.roll(x, MLA_ROPE // 2, axis=1) * s2


def _ev_proj_kernel(h_ref, mod_ref, win_ref, qn_ref, kvn_ref, wuq_ref, wukv_ref, wa2_ref, ba_ref,
                    rc_ref, rs1_ref, rs2_ref,
                    q_ref, k_ref, v_ref, gq_ref, gk_ref, gv_ref, gg_ref, laf_ref, lab_ref):
    x = h_ref[0]
    mod = mod_ref[0]
    xin = (x * (1.0 + mod[1:2]) + mod[0:1]).astype(BF16)
    p = jnp.dot(xin, win_ref[...], preferred_element_type=F32)
    o = 0
    cq = p[:, o:o + MLA_Q_RANK]; o += MLA_Q_RANK
    ckv = p[:, o:o + MLA_KV_RANK]; o += MLA_KV_RANK
    kpe = p[:, o:o + HEAD_PAD]; o += HEAD_PAD
    gdk = GLA_HEADS * GLA_DK
    gdv = GLA_HEADS * GLA_DV
    gq_ref[0] = p[:, o:o + gdk] * (GLA_DK ** -0.5); o += gdk
    gk_ref[0] = p[:, o:o + gdk]; o += gdk
    gv_ref[0] = p[:, o:o + gdv]; o += gdv
    gg_ref[0] = p[:, o:o + gdv]; o += gdv
    ga = p[:, o:o + LANES]

    c, s1, s2 = rc_ref[...], rs1_ref[...], rs2_ref[...]
    nq = cq * lax.rsqrt(jnp.mean(cq * cq, axis=-1, keepdims=True) + LN_EPS) * qn_ref[...]
    q = jnp.dot(nq.astype(BF16), wuq_ref[...], preferred_element_type=F32)
    nkv = ckv * lax.rsqrt(jnp.mean(ckv * ckv, axis=-1, keepdims=True) + LN_EPS) * kvn_ref[...]
    kv = jnp.dot(nkv.astype(BF16), wukv_ref[...], preferred_element_type=F32)
    kpe_r = _rope_block(kpe, c, s1, s2)
    for h in range(MLA_HEADS):
        sl = slice(h * HEAD_PAD, (h + 1) * HEAD_PAD)
        q_ref[0, :, sl] = _rope_block(q[:, sl], c, s1, s2).astype(BF16)
        k_ref[0, :, sl] = (kv[:, sl] + kpe_r).astype(BF16)
    v_ref[0] = kv[:, MLA_HEADS * HEAD_PAD:].astype(BF16)

    z = jnp.dot(ga.astype(BF16), wa2_ref[...], preferred_element_type=F32) + ba_ref[...]
    la = jax.nn.log_sigmoid(z) * (1.0 / GLA_TAU)
    laf_ref[0] = la[:, :gdk]
    lab_ref[0] = la[:, gdk:]


def _ev_proj(h, modt, nct, w):
    b, l, d = h.shape
    nt = l // TOK_TILE
    tok = lambda width: pl.BlockSpec((1, TOK_TILE, width), lambda bi, i: (bi, i, 0))
    full = lambda a: pl.BlockSpec(a.shape, lambda bi, i: (0,) * a.ndim)
    rope = pl.BlockSpec((TOK_TILE, LANES), lambda bi, i: (i, 0))
    gdk = GLA_HEADS * GLA_DK
    gdv = GLA_HEADS * GLA_DV
    widths = [(MLA_HEADS * HEAD_PAD, BF16), (MLA_HEADS * HEAD_PAD, BF16), (MLA_HEADS * MLA_V, BF16),
              (gdk, F32), (gdk, F32), (gdv, F32), (gdv, F32), (gdk, F32), (gdk, F32)]
    return pl.pallas_call(
        _ev_proj_kernel,
        grid=(b, nt),
        in_specs=[tok(d), pl.BlockSpec((1, N_MOD, d), _mod_index(nct, 0)),
                  full(w["win"]), full(w["qn"]), full(w["kvn"]), full(w["wuq"]), full(w["wukv"]),
                  full(w["wa2"]), full(w["ba"]), rope, rope, rope],
        out_specs=[tok(wd) for wd, _ in widths],
        out_shape=[jax.ShapeDtypeStruct((b, l, wd), dt) for wd, dt in widths],
        compiler_params=_cp("parallel", "parallel"),
        name="ev_proj",
    )(h, modt, w["win"], w["qn"], w["kvn"], w["wuq"], w["wukv"], w["wa2"], w["ba"],
      w["rope_c"], w["rope_s1"], w["rope_s2"])


def _attn_kernel(q_ref, k_ref, v_ref, o_ref, *, nct, ctx_len):
    def attend(nk):
        lane = lax.broadcasted_iota(jnp.int32, (TOK_TILE, LANES), 1)
        for hp in range(MLA_HEADS // 2):
            vp = v_ref[0, 0:nk, hp * LANES:(hp + 1) * LANES]
            outs = []
            for hh in range(2):
                h = 2 * hp + hh
                qh = q_ref[0, :, h * HEAD_PAD:(h + 1) * HEAD_PAD]
                kh = k_ref[0, 0:nk, h * HEAD_PAD:(h + 1) * HEAD_PAD]
                s = lax.dot_general(qh, kh, (((1,), (1,)), ((), ())), preferred_element_type=F32)
                m = jnp.max(s, axis=-1, keepdims=True)
                p = jnp.exp((s - m) * MLA_SCALE)
                den = jnp.sum(p, axis=-1, keepdims=True)
                o = jnp.dot(p.astype(BF16), vp, preferred_element_type=F32)
                outs.append(o / den)
            o_ref[0, :, hp * LANES:(hp + 1) * LANES] = jnp.where(lane < MLA_V, outs[0], outs[1]).astype(BF16)

    is_ctx = pl.program_id(1) < nct

    @pl.when(is_ctx)
    def _():
        attend(ctx_len)

    @pl.when(jnp.logical_not(is_ctx))
    def _():
        attend(k_ref.shape[1])


def _attention(q, k, v, nct, ctx_len):
    b, l, _ = q.shape
    nt = l // TOK_TILE
    return pl.pallas_call(
        functools.partial(_attn_kernel, nct=nct, ctx_len=ctx_len),
        grid=(b, nt),
        in_specs=[pl.BlockSpec((1, TOK_TILE, q.shape[2]), lambda bi, i: (bi, i, 0)),
                  pl.BlockSpec((1, l, k.shape[2]), lambda bi, i: (bi, 0, 0)),
                  pl.BlockSpec((1, l, v.shape[2]), lambda bi, i: (bi, 0, 0))],
        out_specs=pl.BlockSpec((1, TOK_TILE, v.shape[2]), lambda bi, i: (bi, i, 0)),
        out_shape=jax.ShapeDtypeStruct((b, l, v.shape[2]), BF16),
        compiler_params=_cp("parallel", "arbitrary"),
        name="mla_attention",
    )(q, k, v)


def _bwd_tile(s, nct, nt):
    return jnp.where(s < nct, nct - 1 - s, nt - 1 - (s - nct))


def _gla_chunk(q, k, v, la, st, rev):
    c = GLA_CHUNK
    nsub = c // GLA_SUB
    row = lax.broadcasted_iota(jnp.int32, (c, c), 0)
    col = lax.broadcasted_iota(jnp.int32, (c, c), 1)
    tri = (col >= row) if rev else (col <= row)
    cum = jnp.dot(tri.astype(F32), la, precision=HI, preferred_element_type=F32)
    rid = lax.broadcasted_iota(jnp.int32, (c, LANES), 0)
    lane = lax.broadcasted_iota(jnp.int32, (c, LANES), 1)
    head0 = lane < GLA_DK

    qparts, kparts, eref = [], [], jnp.zeros((c, LANES), F32)
    for jb in range(nsub):
        r = jb * GLA_SUB if rev else jb * GLA_SUB + GLA_SUB - 1
        ej = cum[r:r + 1, :]
        valid = (rid <= jb * GLA_SUB + GLA_SUB - 1) if rev else (rid >= jb * GLA_SUB)
        qparts.append(jnp.where(valid, q * jnp.exp(jnp.where(valid, cum - ej, 0.0)), 0.0))
        inblk = (rid >= jb * GLA_SUB) & (rid < (jb + 1) * GLA_SUB)
        eref = jnp.where(inblk, ej, eref)
    kt = k * jnp.exp(eref - cum)
    for jb in range(nsub):
        inblk = (rid >= jb * GLA_SUB) & (rid < (jb + 1) * GLA_SUB)
        kparts.append(jnp.where(inblk, kt, 0.0))
    qbig = jnp.concatenate(qparts, axis=1)
    kbig = jnp.concatenate(kparts, axis=1).astype(BF16)
    h0big = lax.broadcasted_iota(jnp.int32, (c, nsub * LANES), 1) % LANES < GLA_DK
    qst = jnp.concatenate([jnp.where(h0big, qbig, 0.0), jnp.where(h0big, 0.0, qbig)], axis=0).astype(BF16)
    sc = lax.dot_general(qst, kbig, (((1,), (1,)), ((), ())), preferred_element_type=F32)
    ri = lax.broadcasted_iota(jnp.int32, (2 * c, c), 0) % c
    ci = lax.broadcasted_iota(jnp.int32, (2 * c, c), 1)
    sc = jnp.where((ci >= ri) if rev else (ci <= ri), sc, 0.0)
    vb = v.astype(BF16)
    intra = jnp.dot(sc.astype(BF16), vb, preferred_element_type=F32)
    qs = (q * jnp.exp(cum)).astype(BF16)
    inter = lax.dot_general(qs, st.astype(BF16), (((1,), (1,)), ((), ())), preferred_element_type=F32)
    out = jnp.concatenate([intra[0:c, 0:GLA_DV] + inter[:, 0:GLA_DV],
                           intra[c:2 * c, GLA_DV:2 * GLA_DV] + inter[:, GLA_DV:2 * GLA_DV]], axis=1)
    last = cum[0:1, :] if rev else cum[c - 1:c, :]
    kd = (k * jnp.exp(last - cum)).astype(BF16)
    upd = lax.dot_general(vb, kd, (((0,), (0,)), ((), ())), preferred_element_type=F32)
    srow = lax.broadcasted_iota(jnp.int32, (2 * GLA_DV, LANES), 0)
    slane = lax.broadcasted_iota(jnp.int32, (2 * GLA_DV, LANES), 1)
    diag = (srow < GLA_DV) == (slane < GLA_DK)
    st_new = jnp.where(diag, st * jnp.exp(last) + upd, 0.0)
    return out, st_new


def _gla_kernel(qf_ref, kf_ref, vf_ref, laf_ref, qb_ref, kb_ref, vb_ref, lab_ref, of_ref, ob_ref, stf_ref, stb_ref):
    @pl.when(pl.program_id(1) == 0)
    def _():
        stf_ref[...] = jnp.zeros_like(stf_ref)
        stb_ref[...] = jnp.zeros_like(stb_ref)

    nchunk = TOK_TILE // GLA_CHUNK
    npair = GLA_HEADS // 2

    def body(ci, carry):
        rf = pl.multiple_of(ci * GLA_CHUNK, GLA_CHUNK)
        rb = pl.multiple_of((nchunk - 1 - ci) * GLA_CHUNK, GLA_CHUNK)
        for pr in range(npair):
            ks = slice(pr * LANES, (pr + 1) * LANES)
            vs = slice(pr * 2 * GLA_DV, (pr + 1) * 2 * GLA_DV)
            o, st = _gla_chunk(qf_ref[0, pl.ds(rf, GLA_CHUNK), ks], kf_ref[0, pl.ds(rf, GLA_CHUNK), ks],
                               vf_ref[0, pl.ds(rf, GLA_CHUNK), vs], laf_ref[0, pl.ds(rf, GLA_CHUNK), ks],
                               stf_ref[pr], False)
            of_ref[0, pl.ds(rf, GLA_CHUNK), vs] = o
            stf_ref[pr] = st
            o, st = _gla_chunk(qb_ref[0, pl.ds(rb, GLA_CHUNK), ks], kb_ref[0, pl.ds(rb, GLA_CHUNK), ks],
                               vb_ref[0, pl.ds(rb, GLA_CHUNK), vs], lab_ref[0, pl.ds(rb, GLA_CHUNK), ks],
                               stb_ref[pr], True)
            ob_ref[0, pl.ds(rb, GLA_CHUNK), vs] = o
            stb_ref[pr] = st
        return carry

    lax.fori_loop(0, nchunk, body, 0)


def _gla(gq, gk, gv, laf, lab, nct):
    b, l, dk = gq.shape
    dv = gv.shape[2]
    nt = l // TOK_TILE
    fwd = lambda width: pl.BlockSpec((1, TOK_TILE, width), lambda bi, s: (bi, s, 0))
    bwd = lambda width: pl.BlockSpec((1, TOK_TILE, width), lambda bi, s: (bi, _bwd_tile(s, nct, nt), 0))
    npair = GLA_HEADS // 2
    return pl.pallas_call(
        _gla_kernel,
        grid=(b, nt),
        in_specs=[fwd(dk), fwd(dk), fwd(dv), fwd(dk), bwd(dk), bwd(dk), bwd(dv), bwd(dk)],
        out_specs=[fwd(dv), bwd(dv)],
        out_shape=[jax.ShapeDtypeStruct((b, l, dv), F32)] * 2,
        scratch_shapes=[pltpu.VMEM((npair, 2 * GLA_DV, LANES), F32)] * 2,
        compiler_params=_cp("parallel", "arbitrary"),
        name="gla_scan",
    )(gq, gk, gv, laf, gq, gk, gv, lab)


def _ev_out_kernel(att_ref, of_ref, ob_ref, gg_ref, gn_ref, wout_ref, h_ref, mod_ref, lng_ref, lnb_ref, o_ref, *, alpha):
    o = of_ref[0] + ob_ref[0]
    g = gg_ref[0]
    parts = []
    for hd in range(GLA_HEADS):
        sl = slice(hd * GLA_DV, (hd + 1) * GLA_DV)
        oh = o[:, sl]
        y = oh * lax.rsqrt(jnp.mean(oh * oh, axis=-1, keepdims=True) + LN_EPS) * gn_ref[:, sl]
        gh = g[:, sl]
        parts.append((y * (gh * jax.nn.sigmoid(gh))).astype(BF16))
    gla = jnp.concatenate(parts, axis=1)
    na = att_ref.shape[2]
    mix = (jnp.dot(att_ref[0], wout_ref[0:na, :], preferred_element_type=F32)
           + jnp.dot(gla, wout_ref[na:, :], preferred_element_type=F32))
    mod = mod_ref[0]
    o_ref[0] = _ln(alpha * h_ref[0] + mod[2:3] * mix, lng_ref[...], lnb_ref[...])


def _ev_out(att, o_f, o_b, gg, gn, wout, h, modt, lng, lnb, nct, alpha):
    b, l, d = h.shape
    nt = l // TOK_TILE
    tok = lambda a: pl.BlockSpec((1, TOK_TILE, a.shape[2]), lambda bi, i: (bi, i, 0))
    full = lambda a: pl.BlockSpec(a.shape, lambda bi, i: (0,) * a.ndim)
    return pl.pallas_call(
        functools.partial(_ev_out_kernel, alpha=alpha),
        grid=(b, nt),
        in_specs=[tok(att), tok(o_f), tok(o_b), tok(gg), full(gn), full(wout), tok(h),
                  pl.BlockSpec((1, N_MOD, d), _mod_index(nct, 0)), full(lng), full(lnb)],
        out_specs=tok(h),
        out_shape=jax.ShapeDtypeStruct((b, l, d), F32),
        compiler_params=_cp("parallel", "parallel"),
        name="ev_out",
    )(att, o_f, o_b, gg, gn, wout, h, modt, lng, lnb)


def _od_proj_kernel(h_ref, mod_ref, win_ref, br_ref, rnn_ref):
    mod = mod_ref[0]
    xin = (h_ref[0] * (1.0 + mod[1:2]) + mod[0:1]).astype(BF16)
    p = jnp.dot(xin, win_ref[...], preferred_element_type=F32)
    br_ref[0] = p[:, :RG_WIDTH]
    rnn_ref[0] = p[:, RG_WIDTH:]


def _od_proj(h, modt, nct, win):
    b, l, d = h.shape
    nt = l // TOK_TILE
    tok = lambda width: pl.BlockSpec((1, TOK_TILE, width), lambda bi, i: (bi, i, 0))
    return pl.pallas_call(
        _od_proj_kernel,
        grid=(b, nt),
        in_specs=[tok(d), pl.BlockSpec((1, N_MOD, d), _mod_index(nct, 0)),
                  pl.BlockSpec(win.shape, lambda bi, i: (0, 0))],
        out_specs=[tok(RG_WIDTH), tok(RG_WIDTH)],
        out_shape=[jax.ShapeDtypeStruct((b, l, RG_WIDTH), F32)] * 2,
        compiler_params=_cp("parallel", "parallel"),
        name="od_proj",
    )(h, modt, win)


def _rg_prepare(z_ref, pv_ref, nx_ref, prev_zero, next_zero, cw_ref, cb_ref, gw_ref, gb_ref, lam_ref, a_ref, x_ref):
    t = TOK_TILE
    z = z_ref[0]
    pv = jnp.where(prev_zero, 0.0, pv_ref[0])
    nx = jnp.where(next_zero, 0.0, nx_ref[0])
    ext = jnp.concatenate([pv, z, nx], axis=0)
    base = SUBLANES - RG_CONV_LEFT
    y = cb_ref[...] + ext[base:base + t] * cw_ref[0:1, :]
    for tap in range(1, RG_CONV):
        y = y + ext[base + tap:base + tap + t] * cw_ref[tap:tap + 1, :]
    for n in range(RG_BLOCKS):
        sl = slice(n * RG_BW, (n + 1) * RG_BW)
        yb = y[:, sl]
        ri = jnp.dot(yb.astype(BF16), gw_ref[n], preferred_element_type=F32)
        r = jax.nn.sigmoid(ri[:, :RG_BW] + gb_ref[0:1, sl])
        i = jax.nn.sigmoid(ri[:, RG_BW:] + gb_ref[1:2, sl])
        log_a = (-RG_C) * jax.nn.softplus(-lam_ref[:, sl]) * r
        th = jnp.tanh(log_a)
        a_ref[:, sl] = jnp.exp(log_a)
        x_ref[:, sl] = jnp.sqrt(-2.0 * th / (1.0 - th)) * (i * yb)


def _rglru_kernel(zf_ref, pf_ref, nf_ref, zb_ref, pb_ref, nb_ref, cw_ref, cb_ref,
                  gwf_ref, gbf_ref, lamf_ref, gwb_ref, gbb_ref, lamb_ref,
                  yf_ref, yb_ref, af_ref, xf_ref, ab_ref, xb_ref, hf_ref, hb_ref, *, nct, nt):
    s = pl.program_id(1)

    @pl.when(s == 0)
    def _():
        hf_ref[...] = jnp.zeros_like(hf_ref)
        hb_ref[...] = jnp.zeros_like(hb_ref)

    def edges(j):
        return (j == 0) | (j == nct), (j == nct - 1) | (j == nt - 1)

    pz, nz = edges(s)
    _rg_prepare(zf_ref, pf_ref, nf_ref, pz, nz, cw_ref, cb_ref, gwf_ref, gbf_ref, lamf_ref, af_ref, xf_ref)
    pz, nz = edges(_bwd_tile(s, nct, nt))
    _rg_prepare(zb_ref, pb_ref, nb_ref, pz, nz, cw_ref, cb_ref, gwb_ref, gbb_ref, lamb_ref, ab_ref, xb_ref)

    def body(kk, carry):
        hf, hb = carry
        tf = kk
        tb = TOK_TILE - 1 - kk
        hf = af_ref[pl.ds(tf, 1), :] * hf + xf_ref[pl.ds(tf, 1), :]
        yf_ref[0, pl.ds(tf, 1), :] = hf
        hb = ab_ref[pl.ds(tb, 1), :] * hb + xb_ref[pl.ds(tb, 1), :]
        yb_ref[0, pl.ds(tb, 1), :] = hb
        return hf, hb

    hf, hb = lax.fori_loop(0, TOK_TILE, body, (hf_ref[...], hb_ref[...]), unroll=8)
    hf_ref[...] = hf
    hb_ref[...] = hb


def _rglru(rnn, nct, w):
    b, l, wd = rnn.shape
    nt = l // TOK_TILE
    per = TOK_TILE // SUBLANES
    nh = l // SUBLANES
    bt = lambda s: _bwd_tile(s, nct, nt)
    tile = lambda f: pl.BlockSpec((1, TOK_TILE, wd), lambda bi, s: (bi, f(s), 0))
    prev = lambda f: pl.BlockSpec((1, SUBLANES, wd), lambda bi, s: (bi, jnp.maximum(f(s) * per - 1, 0), 0))
    nxt = lambda f: pl.BlockSpec((1, SUBLANES, wd), lambda bi, s: (bi, jnp.minimum((f(s) + 1) * per, nh - 1), 0))
    full = lambda a: pl.BlockSpec(a.shape, lambda bi, s: (0,) * a.ndim)
    ident = lambda s: s
    vm = lambda rows: pltpu.VMEM((rows, wd), F32)
    return pl.pallas_call(
        functools.partial(_rglru_kernel, nct=nct, nt=nt),
        grid=(b, nt),
        in_specs=[tile(ident), prev(ident), nxt(ident), tile(bt), prev(bt), nxt(bt),
                  full(w["cw"]), full(w["cb"]), full(w["gwf"]), full(w["gbf"]), full(w["lamf"]),
                  full(w["gwb"]), full(w["gbb"]), full(w["lamb"])],
        out_specs=[tile(ident), tile(bt)],
        out_shape=[jax.ShapeDtypeStruct((b, l, wd), F32)] * 2,
        scratch_shapes=[vm(TOK_TILE), vm(TOK_TILE), vm(TOK_TILE), vm(TOK_TILE), vm(1), vm(1)],
        compiler_params=_cp("parallel", "arbitrary"),
        name="rglru_scan",
    )(rnn, rnn, rnn, rnn, rnn, rnn, w["cw"], w["cb"], w["gwf"], w["gbf"], w["lamf"], w["gwb"], w["gbb"], w["lamb"])


def _od_out_kernel(br_ref, yf_ref, yb_ref, wout_ref, h_ref, mod_ref, lng_ref, lnb_ref, o_ref, *, alpha):
    m = (_gelu(br_ref[0]) * (yf_ref[0] + yb_ref[0])).astype(BF16)
    mix = jnp.dot(m, wout_ref[...], preferred_element_type=F32)
    mod = mod_ref[0]
    o_ref[0] = _ln(alpha * h_ref[0] + mod[2:3] * mix, lng_ref[...], lnb_ref[...])


def _od_out(br, yf, yb, wout, h, modt, lng, lnb, nct, off, alpha):
    b, l, d = h.shape
    nt = l // TOK_TILE - off
    tok = lambda a: pl.BlockSpec((1, TOK_TILE, a.shape[2]), lambda bi, i: (bi, i + off, 0))
    full = lambda a: pl.BlockSpec(a.shape, lambda bi, i: (0,) * a.ndim)
    return pl.pallas_call(
        functools.partial(_od_out_kernel, alpha=alpha),
        grid=(b, nt),
        in_specs=[tok(br), tok(yf), tok(yb), full(wout), tok(h),
                  pl.BlockSpec((1, N_MOD, d), _mod_index(nct, off)), full(lng), full(lnb)],
        out_specs=pl.BlockSpec((1, TOK_TILE, d), lambda bi, i: (bi, i, 0)),
        out_shape=jax.ShapeDtypeStruct((b, nt * TOK_TILE, d), F32),
        compiler_params=_cp("parallel", "parallel"),
        name="od_out",
    )(br, yf, yb, wout, h, modt, lng, lnb)


def _top16_rows(s, vals_ref):
    nk = s.shape[0]
    rid = lax.broadcasted_iota(jnp.int32, s.shape, 0).astype(F32)
    rank = jnp.full(s.shape, float(PEER_TOPK), F32)
    left = s
    for r in range(PEER_TOPK):
        m = jnp.max(left, axis=0, keepdims=True)
        pos = jnp.min(jnp.where(left == m, rid, float(nk)), axis=0, keepdims=True)
        hit = rid == pos
        vals_ref[r:r + 1, :] = m
        rank = jnp.where(hit, float(r), rank)
        left = jnp.where(hit, -jnp.inf, left)
    ex = jnp.where(rank < float(PEER_TOPK), jnp.exp(s - vals_ref[0:1, :]), 0.0)
    return rank, ex


def _cand_sums(t1, t2):
    blocks = [t1[0:1] + t2[0:16]]
    for a in range(1, 8):
        blocks.append(t1[a:a + 1] + t2[0:8])
    blocks.append(t1[8:16] + t2[0:1])
    return jnp.concatenate(blocks, axis=0)


def _selected_counts(t1, t2, flat):
    cs0 = _cand_sums(t1, t2)
    cs = cs0
    sel = jnp.zeros(cs.shape, F32)
    for _ in range(PEER_TOPK):
        m = jnp.max(cs, axis=0, keepdims=True)
        pos = jnp.min(jnp.where(cs == m, flat, float(PEER_TOPK * PEER_TOPK)), axis=0, keepdims=True)
        hit = flat == pos
        sel = jnp.where(hit, 1.0, sel)
        cs = jnp.where(hit, -jnp.inf, cs)
    z = jnp.sum(sel * jnp.exp(cs0 - cs0[0:1]), axis=0, keepdims=True)
    counts = [jnp.sum(sel[0:16], axis=0, keepdims=True)]
    for a in range(1, 8):
        counts.append(jnp.sum(sel[8 + 8 * a:16 + 8 * a], axis=0, keepdims=True))
    for a in range(8, 16):
        counts.append(sel[64 + a:65 + a])
    return counts, z


def _cand_flat_index(ntok):
    a16 = lax.broadcasted_iota(jnp.int32, (16, ntok), 0)
    a8 = lax.broadcasted_iota(jnp.int32, (8, ntok), 0)
    blocks = [a16]
    for a in range(1, 8):
        blocks.append(a * PEER_TOPK + a8)
    blocks.append((a8 + 8) * PEER_TOPK)
    return jnp.concatenate(blocks, axis=0).astype(F32)


def _split_bf16(x):
    hi = x.astype(BF16)
    return hi, (x - hi.astype(F32)).astype(BF16)


def _dot3(a, b, dims):
    dg = lambda x, y: lax.dot_general(x, y, dims, preferred_element_type=F32)
    return dg(a[0], b[0]) + (dg(a[0], b[1]) + dg(a[1], b[0]))


def _peer_route_kernel(h_ref, mod_ref, wq_ref, k_ref, hmt_ref, n1_ref, e1_ref, r2_ref, e2_ref, t1_ref, t2_ref):
    mod = mod_ref[0]
    hm = h_ref[0] * (1.0 + mod[4:5]) + mod[3:4]
    hmt_ref[...] = hm.T.astype(BF16)
    q = _dot3(_split_bf16(hm), (wq_ref[0], wq_ref[1]), (((1,), (0,)), ((), ())))
    flat = _cand_flat_index(LANES)
    nn = (((1,), (1,)), ((), ()))
    qs = _split_bf16(q)
    for h in range(PEER_HEADS):
        c1 = slice(h * PEER_QDIM, h * PEER_QDIM + PEER_HALF)
        c2 = slice(h * PEER_QDIM + PEER_HALF, (h + 1) * PEER_QDIM)
        s1 = _dot3((k_ref[0], k_ref[1]), (qs[0][:, c1], qs[1][:, c1]), nn)
        s2 = _dot3((k_ref[2], k_ref[3]), (qs[0][:, c2], qs[1][:, c2]), nn)
        for tc in range(hm.shape[0] // LANES):
            cols = slice(tc * LANES, (tc + 1) * LANES)
            r1, e1 = _top16_rows(s1[:, cols], t1_ref)
            r2, e2 = _top16_rows(s2[:, cols], t2_ref)
            counts, z = _selected_counts(t1_ref[...], t2_ref[...], flat)
            n1 = jnp.zeros(r1.shape, F32)
            for a in range(PEER_TOPK):
                n1 = jnp.where(r1 == float(a), counts[a], n1)
            n1_ref[h, :, cols] = n1
            e1_ref[h, :, cols] = e1 / z
            r2_ref[h, :, cols] = r2
            e2_ref[h, :, cols] = e2


def _peer_route(h, modt, nct, off, wq, k1, k2):
    b, l, d = h.shape
    nt = l // TOK_TILE
    ntok = b * l
    full = lambda a: pl.BlockSpec(a.shape, lambda bi, i: (0,) * a.ndim)
    top_vals = pltpu.VMEM((PEER_TOPK, LANES), F32)
    wq = jnp.stack(_split_bf16(wq))
    keys = jnp.stack(_split_bf16(k1) + _split_bf16(k2))
    per_key = pl.BlockSpec((PEER_HEADS, PEER_NK, TOK_TILE), lambda bi, i: (0, 0, bi * nt + i))
    per_key_shape = jax.ShapeDtypeStruct((PEER_HEADS, PEER_NK, ntok), F32)
    return pl.pallas_call(
        _peer_route_kernel,
        grid=(b, nt),
        in_specs=[pl.BlockSpec((1, TOK_TILE, d), lambda bi, i: (bi, i, 0)),
                  pl.BlockSpec((1, N_MOD, d), _mod_index(nct, off)), full(wq), full(keys)],
        out_specs=[pl.BlockSpec((d, TOK_TILE), lambda bi, i: (0, bi * nt + i))] + [per_key] * 4,
        out_shape=[jax.ShapeDtypeStruct((d, ntok), BF16)] + [per_key_shape] * 4,
        scratch_shapes=[top_vals, top_vals],
        compiler_params=_cp("parallel", "parallel"),
        name="peer_route",
    )(h, modt, wq, keys)


def _peer_dense_kernel(hmt_ref, u_ref, vt_ref, n1_ref, e1_ref, r2_ref, e2_ref, o_ref, acc_ref, s_ref, w_ref):
    j = pl.program_id(1)

    @pl.when(j == 0)
    def _():
        acc_ref[...] = jnp.zeros_like(acc_ref)

    s_ref[...] = jnp.dot(u_ref[...], hmt_ref[...], preferred_element_type=F32)
    per_step = u_ref.shape[0] // PEER_NK
    for al in range(per_step):
        a = j * per_step + al
        rows = slice(al * PEER_NK, (al + 1) * PEER_NK)
        n1 = [n1_ref[h, pl.ds(a, 1), :] for h in range(PEER_HEADS)]
        e1 = [e1_ref[h, pl.ds(a, 1), :] for h in range(PEER_HEADS)]
        for tc in range(s_ref.shape[1] // LANES):
            cols = slice(tc * LANES, (tc + 1) * LANES)
            g = jnp.zeros((PEER_NK, LANES), F32)
            for h in range(PEER_HEADS):
                g = g + jnp.where(r2_ref[h, :, cols] < n1[h][:, cols], e2_ref[h, :, cols] * e1[h][:, cols], 0.0)
            w_ref[rows, cols] = (_gelu(s_ref[rows, cols]) * g).astype(BF16)
    acc_ref[...] += jnp.dot(vt_ref[...], w_ref[...], preferred_element_type=F32)

    @pl.when(j == pl.num_programs(1) - 1)
    def _():
        o_ref[...] = acc_ref[...].T


def _peer_dense(hmt, n1, e1, r2, e2, u, v):
    d, ntok = hmt.shape
    nexp = u.shape[0]
    tt = PEER_DENSE_TOK if ntok % PEER_DENSE_TOK == 0 else TOK_TILE
    ub = u.astype(BF16)
    vt = v.T.astype(BF16)
    once = pl.Buffered(1)
    per_key = pl.BlockSpec((PEER_HEADS, PEER_NK, tt), lambda i, j: (0, 0, i), pipeline_mode=once)
    return pl.pallas_call(
        _peer_dense_kernel,
        grid=(ntok // tt, nexp // PEER_EXPERT_BLK),
        in_specs=[pl.BlockSpec((d, tt), lambda i, j: (0, i), pipeline_mode=once),
                  pl.BlockSpec((PEER_EXPERT_BLK, d), lambda i, j: (j, 0)),
                  pl.BlockSpec((d, PEER_EXPERT_BLK), lambda i, j: (0, j)),
                  per_key, per_key, per_key, per_key],
        out_specs=pl.BlockSpec((tt, d), lambda i, j: (i, 0)),
        out_shape=jax.ShapeDtypeStruct((ntok, d), F32),
        scratch_shapes=[pltpu.VMEM((d, tt), F32), pltpu.VMEM((PEER_EXPERT_BLK, tt), F32),
                        pltpu.VMEM((PEER_EXPERT_BLK, tt), BF16)],
        compiler_params=_cp("parallel", "arbitrary"),
        name="peer_dense",
    )(hmt, ub, vt, n1, e1, r2, e2)


def _peer_res_kernel(h_ref, f_ref, mod_ref, lng_ref, lnb_ref, o_ref, *, alpha):
    mod = mod_ref[0]
    o_ref[0] = _ln(alpha * h_ref[0] + mod[5:6] * f_ref[0], lng_ref[...], lnb_ref[...])


def _peer_res(h, f, modt, nct, off, lng, lnb, alpha):
    b, l, d = h.shape
    tok = pl.BlockSpec((1, TOK_TILE, d), lambda bi, i: (bi, i, 0))
    full = lambda a: pl.BlockSpec(a.shape, lambda bi, i: (0,) * a.ndim)
    return pl.pallas_call(
        functools.partial(_peer_res_kernel, alpha=alpha),
        grid=(b, l // TOK_TILE),
        in_specs=[tok, tok, pl.BlockSpec((1, N_MOD, d), _mod_index(nct, off)), full(lng), full(lnb)],
        out_specs=tok,
        out_shape=jax.ShapeDtypeStruct((b, l, d), F32),
        compiler_params=_cp("parallel", "parallel"),
        name="peer_res",
    )(h, f.reshape(b, l, d), modt, lng, lnb)


def _rope_tables(seq_len, ctx_len):
    rows = seq_len // GRID_W
    row = np.repeat(np.arange(rows, dtype=np.float32), GRID_W)
    col = np.tile(np.arange(GRID_W, dtype=np.float32), rows)
    n_freq = MLA_ROPE // 4
    inv = jnp.power(ROPE_BASE, -jnp.arange(n_freq, dtype=F32) / n_freq)
    ang = jnp.concatenate([row[:, None] * inv, col[:, None] * inv], axis=-1)
    ang = jnp.concatenate([jnp.zeros((ctx_len, MLA_ROPE // 2), F32), ang], axis=0)
    cos, sin = jnp.cos(ang), jnp.sin(ang)
    l = seq_len + ctx_len
    half = MLA_ROPE // 2
    c = jnp.ones((l, LANES), F32).at[:, MLA_NOPE:MLA_NOPE + half].set(cos).at[:, MLA_NOPE + half:MLA_NOPE + 2 * half].set(cos)
    s1 = jnp.zeros((l, LANES), F32).at[:, MLA_NOPE:MLA_NOPE + half].set(-sin)
    s2 = jnp.zeros((l, LANES), F32).at[:, MLA_NOPE + half:MLA_NOPE + 2 * half].set(sin)
    return c, s1, s2


def _ev_weights(w_in, q_norm, kv_norm, w_uq, w_ukv, wa2, ba, seq_len, ctx_len):
    d = w_in.shape[0]
    widths = (MLA_Q_RANK, MLA_KV_RANK, MLA_ROPE, GLA_HEADS * GLA_DK, GLA_HEADS * GLA_DK,
              GLA_HEADS * GLA_DV, GLA_HEADS * GLA_DV, GLA_GATE_RANK, GLA_GATE_RANK)
    offs = np.cumsum((0,) + widths)
    part = lambda i: w_in[:, offs[i]:offs[i + 1]]
    kpe = jnp.zeros((d, HEAD_PAD), F32).at[:, MLA_NOPE:MLA_NOPE + MLA_ROPE].set(part(2))
    gate = jnp.zeros((d, LANES), F32).at[:, :GLA_GATE_RANK].set(part(7)).at[:, GLA_GATE_RANK:2 * GLA_GATE_RANK].set(part(8))
    win = jnp.concatenate([part(0), part(1), kpe, part(3), part(4), part(5), part(6), gate], axis=1).astype(BF16)
    hq = MLA_NOPE + MLA_ROPE
    wuq = jnp.pad(w_uq.reshape(MLA_Q_RANK, MLA_HEADS, hq), ((0, 0), (0, 0), (0, HEAD_PAD - hq)))
    wuq = wuq.reshape(MLA_Q_RANK, MLA_HEADS * HEAD_PAD).astype(BF16)
    ukv = w_ukv.reshape(MLA_KV_RANK, MLA_HEADS, MLA_NOPE + MLA_V)
    wk = jnp.pad(ukv[:, :, :MLA_NOPE], ((0, 0), (0, 0), (0, HEAD_PAD - MLA_NOPE))).reshape(MLA_KV_RANK, MLA_HEADS * HEAD_PAD)
    wv = ukv[:, :, MLA_NOPE:].reshape(MLA_KV_RANK, MLA_HEADS * MLA_V)
    wukv = jnp.concatenate([wk, wv], axis=1).astype(BF16)
    gdk = GLA_HEADS * GLA_DK
    wa2p = jnp.zeros((LANES, 2 * gdk), F32).at[:GLA_GATE_RANK, :gdk].set(wa2[0])
    wa2p = wa2p.at[GLA_GATE_RANK:2 * GLA_GATE_RANK, gdk:].set(wa2[1]).astype(BF16)
    c, s1, s2 = _rope_tables(seq_len, ctx_len)
    return dict(win=win, qn=q_norm[None, :], kvn=kv_norm[None, :], wuq=wuq, wukv=wukv, wa2=wa2p,
                ba=ba.reshape(1, 2 * gdk), rope_c=c, rope_s1=s1, rope_s2=s2)


def _od_weights(conv_w, conv_b, gate_w, gate_b, lam):
    gw = lambda dd: jnp.concatenate([gate_w[dd, 0], gate_w[dd, 1]], axis=-1).astype(BF16)
    return dict(cw=conv_w, cb=conv_b[None, :], gwf=gw(0), gbf=gate_b[0], lamf=lam[0][None, :],
                gwb=gw(1), gbb=gate_b[1], lamb=lam[1][None, :])


def kernel(x, c, ctx, c_ctx, mod_w, mod_b, ln_g, ln_b,
           ev_w_in, ev_q_norm, ev_kv_norm, ev_w_uq, ev_w_ukv, ev_gla_wa2, ev_gla_ba, ev_gla_norm, ev_w_out,
           od_w_in, od_conv_w, od_conv_b, od_gate_w, od_gate_b, od_lambda, od_w_out,
           peer_wq, peer_k1, peer_k2, peer_u, peer_v):
    b, seq_len, d = x.shape
    ctx_len = ctx.shape[1]
    depth = mod_w.shape[0]
    assert ctx_len % TOK_TILE == 0 and seq_len % TOK_TILE == 0 and seq_len % GRID_W == 0
    alpha = (2 * depth) ** 0.25

    rows = -(-(b + 1) // SUBLANES) * SUBLANES
    cc = jnp.zeros((rows, d), F32).at[:b].set(c).at[b].set(c_ctx)

    h = jnp.concatenate([ctx, x], axis=1)
    nct = ctx_len // TOK_TILE
    off = 0
    for layer in range(depth):
        last = layer == depth - 1
        j = layer // 2
        mods = _mod_vectors(cc, mod_w[layer], mod_b[layer][None, :]).reshape(rows, N_MOD, d)
        modt = jnp.stack([jnp.broadcast_to(mods[b], (b, N_MOD, d)), mods[:b]], axis=1).reshape(2 * b, N_MOD, d)
        lng = ln_g[layer][:, None, :]
        lnb = ln_b[layer][:, None, :]
        assert off == 0
        if layer % 2 == 0:
            w = _ev_weights(ev_w_in[j], ev_q_norm[j], ev_kv_norm[j], ev_w_uq[j], ev_w_ukv[j],
                            ev_gla_wa2[j], ev_gla_ba[j], seq_len, ctx_len)
            q, k, v, gq, gk, gv, gg, laf, lab = _ev_proj(h, modt, nct, w)
            att = _attention(q, k, v, nct, ctx_len)
            o_f, o_b = _gla(gq, gk, gv, laf, lab, nct)
            assert not last, "the final layer drops the context rows only after an odd layer"
            h = _ev_out(att, o_f, o_b, gg, ev_gla_norm[j][None, :], ev_w_out[j].astype(BF16), h, modt,
                        lng[0], lnb[0], nct, alpha)
        else:
            w = _od_weights(od_conv_w[j], od_conv_b[j], od_gate_w[j], od_gate_b[j], od_lambda[j])
            br, rnn = _od_proj(h, modt, nct, od_w_in[j].astype(BF16))
            yf, yb = _rglru(rnn, nct, w)
            off = nct if last else 0
            h = _od_out(br, yf, yb, od_w_out[j].astype(BF16), h, modt, lng[0], lnb[0], nct, off, alpha)
        hmt, n1, e1, r2, e2 = _peer_route(h, modt, nct, off, peer_wq[layer], peer_k1[layer], peer_k2[layer])
        f = _peer_dense(hmt, n1, e1, r2, e2, peer_u[layer], peer_v[layer])
        h = _peer_res(h, f, modt, nct, off, lng[1], lnb[1], alpha)
    return h if off else h[:, ctx_len:]
```

```python
import functools

import numpy as np
import jax
import jax.numpy as jnp
from jax import lax
from jax.experimental import pallas as pl
from jax.experimental.pallas import tpu as pltpu

F32 = jnp.float32
BF16 = jnp.bfloat16
HI = lax.Precision.HIGHEST

N_MOD = 6
LN_EPS = 1e-6
GRID_W = 64
MLA_HEADS = 8
MLA_Q_RANK = 384
MLA_KV_RANK = 256
MLA_NOPE = 64
MLA_ROPE = 32
MLA_V = 64
MLA_SCALE = (MLA_NOPE + MLA_ROPE) ** -0.5
ROPE_BASE = 10000.0
GLA_HEADS = 4
GLA_DK = 64
GLA_DV = 128
GLA_GATE_RANK = 16
GLA_TAU = 16.0
RG_WIDTH = 1280
RG_BLOCKS = 10
RG_BW = RG_WIDTH // RG_BLOCKS
RG_CONV = 4
RG_CONV_LEFT = 2
RG_C = 8.0
PEER_HEADS = 8
PEER_NK = 128
PEER_TOPK = 16
PEER_QDIM = 256
PEER_HALF = PEER_QDIM // 2

LANES = 128
SUBLANES = 8
TOK_TILE = 256
GLA_CHUNK = 64
GLA_SUB = 16
PEER_DENSE_TOK = 512
PEER_EXPERT_BLK = 1024
VMEM_LIMIT = 56 * 1024 * 1024

HEAD_PAD = LANES


def _cp(*sem):
    return pltpu.CompilerParams(dimension_semantics=sem, vmem_limit_bytes=VMEM_LIMIT)


def _gelu(x):
    return 0.5 * x * (1.0 + lax.erf(x * (2.0 ** -0.5)))


def _ln(y, g, b):
    mu = jnp.mean(y, axis=-1, keepdims=True)
    d = y - mu
    var = jnp.mean(d * d, axis=-1, keepdims=True)
    return d * lax.rsqrt(var + LN_EPS) * g + b


def _mod_index(nct, off):
    def index(b, i):
        return (2 * b + jnp.where(i + off >= nct, 1, 0), 0, 0)
    return index


def _mod_kernel(c_ref, w_ref, b_ref, o_ref):
    x = c_ref[...]
    s = x * jax.nn.sigmoid(x)
    o_ref[...] = jnp.dot(s, w_ref[...], precision=HI, preferred_element_type=F32) + b_ref[...]


def _mod_vectors(cc, w, b):
    rows, d = cc.shape
    n = w.shape[1] // d
    return pl.pallas_call(
        _mod_kernel,
        grid=(n,),
        in_specs=[pl.BlockSpec((rows, d), lambda j: (0, 0)),
                  pl.BlockSpec((d, d), lambda j: (0, j)),
                  pl.BlockSpec((1, d), lambda j: (0, j))],
        out_specs=pl.BlockSpec((rows, d), lambda j: (0, j)),
        out_shape=jax.ShapeDtypeStruct((rows, w.shape[1]), F32),
        compiler_params=_cp("arbitrary"),
        name="mod_vectors",
    )(cc, w, b)


def _rope_block(x, c, s1, s2):
    return x * c + pltpu.roll(x, LANES - MLA_ROPE // 2, axis=1) * s1 + pltpu.roll(x, MLA_ROPE // 2, axis=1) * s2


def _ev_proj_kernel(h_ref, mod_ref, win_ref, qn_ref, kvn_ref, wuq_ref, wukv_ref, wa2_ref, ba_ref,
                    rc_ref, rs1_ref, rs2_ref,
                    q_ref, k_ref, v_ref, gq_ref, gk_ref, gv_ref, gg_ref, laf_ref, lab_ref):
    x = h_ref[0]
    mod = mod_ref[0]
    xin = (x * (1.0 + mod[1:2]) + mod[0:1]).astype(BF16)
    p = jnp.dot(xin, win_ref[...], preferred_element_type=F32)
    o = 0
    cq = p[:, o:o + MLA_Q_RANK]; o += MLA_Q_RANK
    ckv = p[:, o:o + MLA_KV_RANK]; o += MLA_KV_RANK
    kpe = p[:, o:o + HEAD_PAD]; o += HEAD_PAD
    gdk = GLA_HEADS * GLA_DK
    gdv = GLA_HEADS * GLA_DV
    gq_ref[0] = p[:, o:o + gdk] * (GLA_DK ** -0.5); o += gdk
    gk_ref[0] = p[:, o:o + gdk]; o += gdk
    gv_ref[0] = p[:, o:o + gdv]; o += gdv
    gg_ref[0] = p[:, o:o + gdv]; o += gdv
    ga = p[:, o:o + LANES]

    c, s1, s2 = rc_ref[...], rs1_ref[...], rs2_ref[...]
    nq = cq * lax.rsqrt(jnp.mean(cq * cq, axis=-1, keepdims=True) + LN_EPS) * qn_ref[...]
    q = jnp.dot(nq.astype(BF16), wuq_ref[...], preferred_element_type=F32)
    nkv = ckv * lax.rsqrt(jnp.mean(ckv * ckv, axis=-1, keepdims=True) + LN_EPS) * kvn_ref[...]
    kv = jnp.dot(nkv.astype(BF16), wukv_ref[...], preferred_element_type=F32)
    kpe_r = _rope_block(kpe, c, s1, s2)
    for h in range(MLA_HEADS):
        sl = slice(h * HEAD_PAD, (h + 1) * HEAD_PAD)
        q_ref[0, :, sl] = _rope_block(q[:, sl], c, s1, s2).astype(BF16)
        k_ref[0, :, sl] = (kv[:, sl] + kpe_r).astype(BF16)
    v_ref[0] = kv[:, MLA_HEADS * HEAD_PAD:].astype(BF16)

    z = jnp.dot(ga.astype(BF16), wa2_ref[...], preferred_element_type=F32) + ba_ref[...]
    la = jax.nn.log_sigmoid(z) * (1.0 / GLA_TAU)
    laf_ref[0] = la[:, :gdk]
    lab_ref[0] = la[:, gdk:]


def _ev_proj(h, modt, nct, w):
    b, l, d = h.shape
    nt = l // TOK_TILE
    tok = lambda width: pl.BlockSpec((1, TOK_TILE, width), lambda bi, i: (bi, i, 0))
    full = lambda a: pl.BlockSpec(a.shape, lambda bi, i: (0,) * a.ndim)
    rope = pl.BlockSpec((TOK_TILE, LANES), lambda bi, i: (i, 0))
    gdk = GLA_HEADS * GLA_DK
    gdv = GLA_HEADS * GLA_DV
    widths = [(MLA_HEADS * HEAD_PAD, BF16), (MLA_HEADS * HEAD_PAD, BF16), (MLA_HEADS * MLA_V, BF16),
              (gdk, F32), (gdk, F32), (gdv, F32), (gdv, F32), (gdk, F32), (gdk, F32)]
    return pl.pallas_call(
        _ev_proj_kernel,
        grid=(b, nt),
        in_specs=[tok(d), pl.BlockSpec((1, N_MOD, d), _mod_index(nct, 0)),
                  full(w["win"]), full(w["qn"]), full(w["kvn"]), full(w["wuq"]), full(w["wukv"]),
                  full(w["wa2"]), full(w["ba"]), rope, rope, rope],
        out_specs=[tok(wd) for wd, _ in widths],
        out_shape=[jax.ShapeDtypeStruct((b, l, wd), dt) for wd, dt in widths],
        compiler_params=_cp("parallel", "parallel"),
        name="ev_proj",
    )(h, modt, w["win"], w["qn"], w["kvn"], w["wuq"], w["wukv"], w["wa2"], w["ba"],
      w["rope_c"], w["rope_s1"], w["rope_s2"])


def _attn_kernel(q_ref, k_ref, v_ref, o_ref, *, nct, ctx_len):
    def attend(nk):
        lane = lax.broadcasted_iota(jnp.int32, (TOK_TILE, LANES), 1)
        for hp in range(MLA_HEADS // 2):
            vp = v_ref[0, 0:nk, hp * LANES:(hp + 1) * LANES]
            outs = []
            for hh in range(2):
                h = 2 * hp + hh
                qh = q_ref[0, :, h * HEAD_PAD:(h + 1) * HEAD_PAD]
                kh = k_ref[0, 0:nk, h * HEAD_PAD:(h + 1) * HEAD_PAD]
                s = lax.dot_general(qh, kh, (((1,), (1,)), ((), ())), preferred_element_type=F32)
                m = jnp.max(s, axis=-1, keepdims=True)
                p = jnp.exp((s - m) * MLA_SCALE)
                den = jnp.sum(p, axis=-1, keepdims=True)
                o = jnp.dot(p.astype(BF16), vp, preferred_element_type=F32)
                outs.append(o / den)
            o_ref[0, :, hp * LANES:(hp + 1) * LANES] = jnp.where(lane < MLA_V, outs[0], outs[1]).astype(BF16)

    is_ctx = pl.program_id(1) < nct

    @pl.when(is_ctx)
    def _():
        attend(ctx_len)

    @pl.when(jnp.logical_not(is_ctx))
    def _():
        attend(k_ref.shape[1])


def _attention(q, k, v, nct, ctx_len):
    b, l, _ = q.shape
    nt = l // TOK_TILE
    return pl.pallas_call(
        functools.partial(_attn_kernel, nct=nct, ctx_len=ctx_len),
        grid=(b, nt),
        in_specs=[pl.BlockSpec((1, TOK_TILE, q.shape[2]), lambda bi, i: (bi, i, 0)),
                  pl.BlockSpec((1, l, k.shape[2]), lambda bi, i: (bi, 0, 0)),
                  pl.BlockSpec((1, l, v.shape[2]), lambda bi, i: (bi, 0, 0))],
        out_specs=pl.BlockSpec((1, TOK_TILE, v.shape[2]), lambda bi, i: (bi, i, 0)),
        out_shape=jax.ShapeDtypeStruct((b, l, v.shape[2]), BF16),
        compiler_params=_cp("parallel", "arbitrary"),
        name="mla_attention",
    )(q, k, v)


def _bwd_tile(s, nct, nt):
    return jnp.where(s < nct, nct - 1 - s, nt - 1 - (s - nct))


def _gla_chunk(q, k, v, la, st, rev):
    c = GLA_CHUNK
    nsub = c // GLA_SUB
    row = lax.broadcasted_iota(jnp.int32, (c, c), 0)
    col = lax.broadcasted_iota(jnp.int32, (c, c), 1)
    tri = (col >= row) if rev else (col <= row)
    cum = jnp.dot(tri.astype(F32), la, precision=HI, preferred_element_type=F32)
    rid = lax.broadcasted_iota(jnp.int32, (c, LANES), 0)
    lane = lax.broadcasted_iota(jnp.int32, (c, LANES), 1)
    head0 = lane < GLA_DK

    qparts, kparts, eref = [], [], jnp.zeros((c, LANES), F32)
    for jb in range(nsub):
        r = jb * GLA_SUB if rev else jb * GLA_SUB + GLA_SUB - 1
        ej = cum[r:r + 1, :]
        valid = (rid <= jb * GLA_SUB + GLA_SUB - 1) if rev else (rid >= jb * GLA_SUB)
        qparts.append(jnp.where(valid, q * jnp.exp(jnp.where(valid, cum - ej, 0.0)), 0.0))
        inblk = (rid >= jb * GLA_SUB) & (rid < (jb + 1) * GLA_SUB)
        eref = jnp.where(inblk, ej, eref)
    kt = k * jnp.exp(eref - cum)
    for jb in range(nsub):
        inblk = (rid >= jb * GLA_SUB) & (rid < (jb + 1) * GLA_SUB)
        kparts.append(jnp.where(inblk, kt, 0.0))
    qbig = jnp.concatenate(qparts, axis=1)
    kbig = jnp.concatenate(kparts, axis=1).astype(BF16)
    h0big = lax.broadcasted_iota(jnp.int32, (c, nsub * LANES), 1) % LANES < GLA_DK
    qst = jnp.concatenate([jnp.where(h0big, qbig, 0.0), jnp.where(h0big, 0.0, qbig)], axis=0).astype(BF16)
    sc = lax.dot_general(qst, kbig, (((1,), (1,)), ((), ())), preferred_element_type=F32)
    ri = lax.broadcasted_iota(jnp.int32, (2 * c, c), 0) % c
    ci = lax.broadcasted_iota(jnp.int32, (2 * c, c), 1)
    sc = jnp.where((ci >= ri) if rev else (ci <= ri), sc, 0.0)
    vb = v.astype(BF16)
    intra = jnp.dot(sc.astype(BF16), vb, preferred_element_type=F32)
    qs = (q * jnp.exp(cum)).astype(BF16)
    inter = lax.dot_general(qs, st.astype(BF16), (((1,), (1,)), ((), ())), preferred_element_type=F32)
    out = jnp.concatenate([intra[0:c, 0:GLA_DV] + inter[:, 0:GLA_DV],
                           intra[c:2 * c, GLA_DV:2 * GLA_DV] + inter[:, GLA_DV:2 * GLA_DV]], axis=1)
    last = cum[0:1, :] if rev else cum[c - 1:c, :]
    kd = (k * jnp.exp(last - cum)).astype(BF16)
    upd = lax.dot_general(vb, kd, (((0,), (0,)), ((), ())), preferred_element_type=F32)
    srow = lax.broadcasted_iota(jnp.int32, (2 * GLA_DV, LANES), 0)
    slane = lax.broadcasted_iota(jnp.int32, (2 * GLA_DV, LANES), 1)
    diag = (srow < GLA_DV) == (slane < GLA_DK)
    st_new = jnp.where(diag, st * jnp.exp(last) + upd, 0.0)
    return out, st_new


def _gla_kernel(qf_ref, kf_ref, vf_ref, laf_ref, qb_ref, kb_ref, vb_ref, lab_ref, of_ref, ob_ref, stf_ref, stb_ref):
    @pl.when(pl.program_id(1) == 0)
    def _():
        stf_ref[...] = jnp.zeros_like(stf_ref)
        stb_ref[...] = jnp.zeros_like(stb_ref)

    nchunk = TOK_TILE // GLA_CHUNK
    npair = GLA_HEADS // 2

    def body(ci, carry):
        rf = pl.multiple_of(ci * GLA_CHUNK, GLA_CHUNK)
        rb = pl.multiple_of((nchunk - 1 - ci) * GLA_CHUNK, GLA_CHUNK)
        for pr in range(npair):
            ks = slice(pr * LANES, (pr + 1) * LANES)
            vs = slice(pr * 2 * GLA_DV, (pr + 1) * 2 * GLA_DV)
            o, st = _gla_chunk(qf_ref[0, pl.ds(rf, GLA_CHUNK), ks], kf_ref[0, pl.ds(rf, GLA_CHUNK), ks],
                               vf_ref[0, pl.ds(rf, GLA_CHUNK), vs], laf_ref[0, pl.ds(rf, GLA_CHUNK), ks],
                               stf_ref[pr], False)
            of_ref[0, pl.ds(rf, GLA_CHUNK), vs] = o
            stf_ref[pr] = st
            o, st = _gla_chunk(qb_ref[0, pl.ds(rb, GLA_CHUNK), ks], kb_ref[0, pl.ds(rb, GLA_CHUNK), ks],
                               vb_ref[0, pl.ds(rb, GLA_CHUNK), vs], lab_ref[0, pl.ds(rb, GLA_CHUNK), ks],
                               stb_ref[pr], True)
            ob_ref[0, pl.ds(rb, GLA_CHUNK), vs] = o
            stb_ref[pr] = st
        return carry

    lax.fori_loop(0, nchunk, body, 0)


def _gla(gq, gk, gv, laf, lab, nct):
    b, l, dk = gq.shape
    dv = gv.shape[2]
    nt = l // TOK_TILE
    fwd = lambda width: pl.BlockSpec((1, TOK_TILE, width), lambda bi, s: (bi, s, 0))
    bwd = lambda width: pl.BlockSpec((1, TOK_TILE, width), lambda bi, s: (bi, _bwd_tile(s, nct, nt), 0))
    npair = GLA_HEADS // 2
    return pl.pallas_call(
        _gla_kernel,
        grid=(b, nt),
        in_specs=[fwd(dk), fwd(dk), fwd(dv), fwd(dk), bwd(dk), bwd(dk), bwd(dv), bwd(dk)],
        out_specs=[fwd(dv), bwd(dv)],
        out_shape=[jax.ShapeDtypeStruct((b, l, dv), F32)] * 2,
        scratch_shapes=[pltpu.VMEM((npair, 2 * GLA_DV, LANES), F32)] * 2,
        compiler_params=_cp("parallel", "arbitrary"),
        name="gla_scan",
    )(gq, gk, gv, laf, gq, gk, gv, lab)


def _ev_out_kernel(att_ref, of_ref, ob_ref, gg_ref, gn_ref, wout_ref, h_ref, mod_ref, lng_ref, lnb_ref, o_ref, *, alpha):
    o = of_ref[0] + ob_ref[0]
    g = gg_ref[0]
    parts = []
    for hd in range(GLA_HEADS):
        sl = slice(hd * GLA_DV, (hd + 1) * GLA_DV)
        oh = o[:, sl]
        y = oh * lax.rsqrt(jnp.mean(oh * oh, axis=-1, keepdims=True) + LN_EPS) * gn_ref[:, sl]
        gh = g[:, sl]
        parts.append((y * (gh * jax.nn.sigmoid(gh))).astype(BF16))
    gla = jnp.concatenate(parts, axis=1)
    na = att_ref.shape[2]
    mix = (jnp.dot(att_ref[0], wout_ref[0:na, :], preferred_element_type=F32)
           + jnp.dot(gla, wout_ref[na:, :], preferred_element_type=F32))
    mod = mod_ref[0]
    o_ref[0] = _ln(alpha * h_ref[0] + mod[2:3] * mix, lng_ref[...], lnb_ref[...])


def _ev_out(att, o_f, o_b, gg, gn, wout, h, modt, lng, lnb, nct, alpha):
    b, l, d = h.shape
    nt = l // TOK_TILE
    tok = lambda a: pl.BlockSpec((1, TOK_TILE, a.shape[2]), lambda bi, i: (bi, i, 0))
    full = lambda a: pl.BlockSpec(a.shape, lambda bi, i: (0,) * a.ndim)
    return pl.pallas_call(
        functools.partial(_ev_out_kernel, alpha=alpha),
        grid=(b, nt),
        in_specs=[tok(att), tok(o_f), tok(o_b), tok(gg), full(gn), full(wout), tok(h),
                  pl.BlockSpec((1, N_MOD, d), _mod_index(nct, 0)), full(lng), full(lnb)],
        out_specs=tok(h),
        out_shape=jax.ShapeDtypeStruct((b, l, d), F32),
        compiler_params=_cp("parallel", "parallel"),
        name="ev_out",
    )(att, o_f, o_b, gg, gn, wout, h, modt, lng, lnb)


def _od_proj_kernel(h_ref, mod_ref, win_ref, br_ref, rnn_ref):
    mod = mod_ref[0]
    xin = (h_ref[0] * (1.0 + mod[1:2]) + mod[0:1]).astype(BF16)
    p = jnp.dot(xin, win_ref[...], preferred_element_type=F32)
    br_ref[0] = p[:, :RG_WIDTH]
    rnn_ref[0] = p[:, RG_WIDTH:]


def _od_proj(h, modt, nct, win):
    b, l, d = h.shape
    nt = l // TOK_TILE
    tok = lambda width: pl.BlockSpec((1, TOK_TILE, width), lambda bi, i: (bi, i, 0))
    return pl.pallas_call(
        _od_proj_kernel,
        grid=(b, nt),
        in_specs=[tok(d), pl.BlockSpec((1, N_MOD, d), _mod_index(nct, 0)),
                  pl.BlockSpec(win.shape, lambda bi, i: (0, 0))],
        out_specs=[tok(RG_WIDTH), tok(RG_WIDTH)],
        out_shape=[jax.ShapeDtypeStruct((b, l, RG_WIDTH), F32)] * 2,
        compiler_params=_cp("parallel", "parallel"),
        name="od_proj",
    )(h, modt, win)


def _rg_prepare(z_ref, pv_ref, nx_ref, prev_zero, next_zero, cw_ref, cb_ref, gw_ref, gb_ref, lam_ref, a_ref, x_ref):
    t = TOK_TILE
    z = z_ref[0]
    pv = jnp.where(prev_zero, 0.0, pv_ref[0])
    nx = jnp.where(next_zero, 0.0, nx_ref[0])
    ext = jnp.concatenate([pv, z, nx], axis=0)
    base = SUBLANES - RG_CONV_LEFT
    y = cb_ref[...] + ext[base:base + t] * cw_ref[0:1, :]
    for tap in range(1, RG_CONV):
        y = y + ext[base + tap:base + tap + t] * cw_ref[tap:tap + 1, :]
    for n in range(RG_BLOCKS):
        sl = slice(n * RG_BW, (n + 1) * RG_BW)
        yb = y[:, sl]
        ri = jnp.dot(yb.astype(BF16), gw_ref[n], preferred_element_type=F32)
        r = jax.nn.sigmoid(ri[:, :RG_BW] + gb_ref[0:1, sl])
        i = jax.nn.sigmoid(ri[:, RG_BW:] + gb_ref[1:2, sl])
        log_a = (-RG_C) * jax.nn.softplus(-lam_ref[:, sl]) * r
        th = jnp.tanh(log_a)
        a_ref[:, sl] = jnp.exp(log_a)
        x_ref[:, sl] = jnp.sqrt(-2.0 * th / (1.0 - th)) * (i * yb)


def _rglru_kernel(zf_ref, pf_ref, nf_ref, zb_ref, pb_ref, nb_ref, cw_ref, cb_ref,
                  gwf_ref, gbf_ref, lamf_ref, gwb_ref, gbb_ref, lamb_ref,
                  yf_ref, yb_ref, af_ref, xf_ref, ab_ref, xb_ref, hf_ref, hb_ref, *, nct, nt):
    s = pl.program_id(1)

    @pl.when(s == 0)
    def _():
        hf_ref[...] = jnp.zeros_like(hf_ref)
        hb_ref[...] = jnp.zeros_like(hb_ref)

    def edges(j):
        return (j == 0) | (j == nct), (j == nct - 1) | (j == nt - 1)

    pz, nz = edges(s)
    _rg_prepare(zf_ref, pf_ref, nf_ref, pz, nz, cw_ref, cb_ref, gwf_ref, gbf_ref, lamf_ref, af_ref, xf_ref)
    pz, nz = edges(_bwd_tile(s, nct, nt))
    _rg_prepare(zb_ref, pb_ref, nb_ref, pz, nz, cw_ref, cb_ref, gwb_ref, gbb_ref, lamb_ref, ab_ref, xb_ref)

    def body(kk, carry):
        hf, hb = carry
        tf = kk
        tb = TOK_TILE - 1 - kk
        hf = af_ref[pl.ds(tf, 1), :] * hf + xf_ref[pl.ds(tf, 1), :]
        yf_ref[0, pl.ds(tf, 1), :] = hf
        hb = ab_ref[pl.ds(tb, 1), :] * hb + xb_ref[pl.ds(tb, 1), :]
        yb_ref[0, pl.ds(tb, 1), :] = hb
        return hf, hb

    hf, hb = lax.fori_loop(0, TOK_TILE, body, (hf_ref[...], hb_ref[...]), unroll=8)
    hf_ref[...] = hf
    hb_ref[...] = hb


def _rglru(rnn, nct, w):
    b, l, wd = rnn.shape
    nt = l // TOK_TILE
    per = TOK_TILE // SUBLANES
    nh = l // SUBLANES
    bt = lambda s: _bwd_tile(s, nct, nt)
    tile = lambda f: pl.BlockSpec((1, TOK_TILE, wd), lambda bi, s: (bi, f(s), 0))
    prev = lambda f: pl.BlockSpec((1, SUBLANES, wd), lambda bi, s: (bi, jnp.maximum(f(s) * per - 1, 0), 0))
    nxt = lambda f: pl.BlockSpec((1, SUBLANES, wd), lambda bi, s: (bi, jnp.minimum((f(s) + 1) * per, nh - 1), 0))
    full = lambda a: pl.BlockSpec(a.shape, lambda bi, s: (0,) * a.ndim)
    ident = lambda s: s
    vm = lambda rows: pltpu.VMEM((rows, wd), F32)
    return pl.pallas_call(
        functools.partial(_rglru_kernel, nct=nct, nt=nt),
        grid=(b, nt),
        in_specs=[tile(ident), prev(ident), nxt(ident), tile(bt), prev(bt), nxt(bt),
                  full(w["cw"]), full(w["cb"]), full(w["gwf"]), full(w["gbf"]), full(w["lamf"]),
                  full(w["gwb"]), full(w["gbb"]), full(w["lamb"])],
        out_specs=[tile(ident), tile(bt)],
        out_shape=[jax.ShapeDtypeStruct((b, l, wd), F32)] * 2,
        scratch_shapes=[vm(TOK_TILE), vm(TOK_TILE), vm(TOK_TILE), vm(TOK_TILE), vm(1), vm(1)],
        compiler_params=_cp("parallel", "arbitrary"),
        name="rglru_scan",
    )(rnn, rnn, rnn, rnn, rnn, rnn, w["cw"], w["cb"], w["gwf"], w["gbf"], w["lamf"], w["gwb"], w["gbb"], w["lamb"])


def _od_out_kernel(br_ref, yf_ref, yb_ref, wout_ref, h_ref, mod_ref, lng_ref, lnb_ref, o_ref, *, alpha):
    m = (_gelu(br_ref[0]) * (yf_ref[0] + yb_ref[0])).astype(BF16)
    mix = jnp.dot(m, wout_ref[...], preferred_element_type=F32)
    mod = mod_ref[0]
    o_ref[0] = _ln(alpha * h_ref[0] + mod[2:3] * mix, lng_ref[...], lnb_ref[...])


def _od_out(br, yf, yb, wout, h, modt, lng, lnb, nct, off, alpha):
    b, l, d = h.shape
    nt = l // TOK_TILE - off
    tok = lambda a: pl.BlockSpec((1, TOK_TILE, a.shape[2]), lambda bi, i: (bi, i + off, 0))
    full = lambda a: pl.BlockSpec(a.shape, lambda bi, i: (0,) * a.ndim)
    return pl.pallas_call(
        functools.partial(_od_out_kernel, alpha=alpha),
        grid=(b, nt),
        in_specs=[tok(br), tok(yf), tok(yb), full(wout), tok(h),
                  pl.BlockSpec((1, N_MOD, d), _mod_index(nct, off)), full(lng), full(lnb)],
        out_specs=pl.BlockSpec((1, TOK_TILE, d), lambda bi, i: (bi, i, 0)),
        out_shape=jax.ShapeDtypeStruct((b, nt * TOK_TILE, d), F32),
        compiler_params=_cp("parallel", "parallel"),
        name="od_out",
    )(br, yf, yb, wout, h, modt, lng, lnb)


def _top16_rows(s, vals_ref, exact):
    nk = s.shape[0]
    top = float(PEER_TOPK)
    rid = lax.broadcasted_iota(jnp.int32, s.shape, 0).astype(F32)
    rank = jnp.full(s.shape, top, F32)
    left = s
    for r in range(PEER_TOPK):
        m = jnp.max(left, axis=0, keepdims=True)
        hit = left == m
        if exact:
            hit = rid == jnp.min(jnp.where(hit, rid, float(nk)), axis=0, keepdims=True)
        vals_ref[r:r + 1, :] = m
        rank = jnp.where(hit, float(r), rank)
        left = jnp.where(hit, -jnp.inf, left)
    picked = jnp.sum(jnp.where(rank < top, 1.0, 0.0), axis=0, keepdims=True)
    ex = jnp.where(rank < top, jnp.exp(s - vals_ref[0:1, :]), 0.0)
    return rank, ex, jnp.abs(picked - top)


def _cand_sums(t1, t2):
    blocks = [t1[0:1] + t2[0:16]]
    for a in range(1, 8):
        blocks.append(t1[a:a + 1] + t2[0:8])
    blocks.append(t1[8:16] + t2[0:1])
    return jnp.concatenate(blocks, axis=0)


def _selected_counts(t1, t2, flat, exact):
    cs0 = _cand_sums(t1, t2)
    cs = cs0
    sel = jnp.zeros(cs.shape, F32)
    for _ in range(PEER_TOPK):
        m = jnp.max(cs, axis=0, keepdims=True)
        hit = cs == m
        if exact:
            hit = flat == jnp.min(jnp.where(hit, flat, float(PEER_TOPK * PEER_TOPK)), axis=0, keepdims=True)
        sel = jnp.where(hit, 1.0, sel)
        cs = jnp.where(hit, -jnp.inf, cs)
    picked = jnp.sum(sel, axis=0, keepdims=True)
    z = jnp.sum(sel * jnp.exp(cs0 - cs0[0:1]), axis=0, keepdims=True)
    counts = [jnp.sum(sel[0:16], axis=0, keepdims=True)]
    for a in range(1, 8):
        counts.append(jnp.sum(sel[8 + 8 * a:16 + 8 * a], axis=0, keepdims=True))
    for a in range(8, 16):
        counts.append(sel[64 + a:65 + a])
    return counts, z, jnp.abs(picked - float(PEER_TOPK))


def _cand_flat_index(ntok):
    a16 = lax.broadcasted_iota(jnp.int32, (16, ntok), 0)
    a8 = lax.broadcasted_iota(jnp.int32, (8, ntok), 0)
    blocks = [a16]
    for a in range(1, 8):
        blocks.append(a * PEER_TOPK + a8)
    blocks.append((a8 + 8) * PEER_TOPK)
    return jnp.concatenate(blocks, axis=0).astype(F32)


def _split_bf16(x):
    hi = x.astype(BF16)
    return hi, (x - hi.astype(F32)).astype(BF16)


def _dot3(a, b, dims):
    dg = lambda x, y: lax.dot_general(x, y, dims, preferred_element_type=F32)
    return dg(a[0], b[0]) + (dg(a[0], b[1]) + dg(a[1], b[0]))


def _peer_route_kernel(h_ref, mod_ref, wq_ref, k_ref, hmt_ref, n1_ref, e1_ref, r2_ref, e2_ref, vals_ref):
    mod = mod_ref[0]
    hm = h_ref[0] * (1.0 + mod[4:5]) + mod[3:4]
    hmt_ref[...] = hm.T.astype(BF16)
    q = _dot3(_split_bf16(hm), (wq_ref[0], wq_ref[1]), (((1,), (0,)), ((), ())))
    flat = _cand_flat_index(LANES)
    nn = (((1,), (1,)), ((), ()))
    qs = _split_bf16(q)
    ntile = hm.shape[0] // LANES

    def route(exact):
        off_count = jnp.zeros((1, LANES), F32)
        for h in range(PEER_HEADS):
            c1 = slice(h * PEER_QDIM, h * PEER_QDIM + PEER_HALF)
            c2 = slice(h * PEER_QDIM + PEER_HALF, (h + 1) * PEER_QDIM)
            s1 = _dot3((k_ref[0], k_ref[1]), (qs[0][:, c1], qs[1][:, c1]), nn)
            s2 = _dot3((k_ref[2], k_ref[3]), (qs[0][:, c2], qs[1][:, c2]), nn)
            for tc in range(ntile):
                cols = slice(tc * LANES, (tc + 1) * LANES)
                t1_ref = vals_ref.at[(h * ntile + tc) * 2]
                t2_ref = vals_ref.at[(h * ntile + tc) * 2 + 1]
                r1, e1, d1 = _top16_rows(s1[:, cols], t1_ref, exact)
                r2, e2, d2 = _top16_rows(s2[:, cols], t2_ref, exact)
                counts, z, d3 = _selected_counts(t1_ref[...], t2_ref[...], flat, exact)
                off_count = jnp.maximum(off_count, jnp.maximum(jnp.maximum(d1, d2), d3))
                n1 = jnp.zeros(r1.shape, F32)
                for a in range(PEER_TOPK):
                    n1 = jnp.where(r1 == float(a), counts[a], n1)
                n1_ref[h, :, cols] = n1
                e1_ref[h, :, cols] = e1 / z
                r2_ref[h, :, cols] = r2
                e2_ref[h, :, cols] = e2
        return off_count

    off_count = route(False)

    @pl.when(jnp.max(off_count) > 0.0)
    def _():
        route(True)


def _peer_route(h, modt, nct, off, wq, k1, k2):
    b, l, d = h.shape
    nt = l // TOK_TILE
    ntok = b * l
    full = lambda a: pl.BlockSpec(a.shape, lambda bi, i: (0,) * a.ndim)
    top_vals = pltpu.VMEM((PEER_HEADS * (TOK_TILE // LANES) * 2, PEER_TOPK, LANES), F32)
    wq = jnp.stack(_split_bf16(wq))
    keys = jnp.stack(_split_bf16(k1) + _split_bf16(k2))
    per_key = pl.BlockSpec((PEER_HEADS, PEER_NK, TOK_TILE), lambda bi, i: (0, 0, bi * nt + i))
    per_key_shape = jax.ShapeDtypeStruct((PEER_HEADS, PEER_NK, ntok), F32)
    return pl.pallas_call(
        _peer_route_kernel,
        grid=(b, nt),
        in_specs=[pl.BlockSpec((1, TOK_TILE, d), lambda bi, i: (bi, i, 0)),
                  pl.BlockSpec((1, N_MOD, d), _mod_index(nct, off)), full(wq), full(keys)],
        out_specs=[pl.BlockSpec((d, TOK_TILE), lambda bi, i: (0, bi * nt + i))] + [per_key] * 4,
        out_shape=[jax.ShapeDtypeStruct((d, ntok), BF16)] + [per_key_shape] * 4,
        scratch_shapes=[top_vals],
        compiler_params=_cp("parallel", "parallel"),
        name="peer_route",
    )(h, modt, wq, keys)


def _peer_dense_kernel(hmt_ref, u_ref, vt_ref, n1_ref, e1_ref, r2_ref, e2_ref, o_ref, acc_ref, s_ref, w_ref):
    j = pl.program_id(1)

    @pl.when(j == 0)
    def _():
        acc_ref[...] = jnp.zeros_like(acc_ref)

    s_ref[...] = jnp.dot(u_ref[...], hmt_ref[...], preferred_element_type=F32)
    per_step = u_ref.shape[0] // PEER_NK
    for al in range(per_step):
        a = j * per_step + al
        rows = slice(al * PEER_NK, (al + 1) * PEER_NK)
        n1 = [n1_ref[h, pl.ds(a, 1), :] for h in range(PEER_HEADS)]
        e1 = [e1_ref[h, pl.ds(a, 1), :] for h in range(PEER_HEADS)]
        for tc in range(s_ref.shape[1] // LANES):
            cols = slice(tc * LANES, (tc + 1) * LANES)
            g = jnp.zeros((PEER_NK, LANES), F32)
            for h in range(PEER_HEADS):
                g = g + jnp.where(r2_ref[h, :, cols] < n1[h][:, cols], e2_ref[h, :, cols] * e1[h][:, cols], 0.0)
            w_ref[rows, cols] = (_gelu(s_ref[rows, cols]) * g).astype(BF16)
    acc_ref[...] += jnp.dot(vt_ref[...], w_ref[...], preferred_element_type=F32)

    @pl.when(j == pl.num_programs(1) - 1)
    def _():
        o_ref[...] = acc_ref[...].T


def _peer_dense(hmt, n1, e1, r2, e2, u, v):
    d, ntok = hmt.shape
    nexp = u.shape[0]
    tt = PEER_DENSE_TOK if ntok % PEER_DENSE_TOK == 0 else TOK_TILE
    ub = u.astype(BF16)
    vt = v.T.astype(BF16)
    once = pl.Buffered(1)
    per_key = pl.BlockSpec((PEER_HEADS, PEER_NK, tt), lambda i, j: (0, 0, i), pipeline_mode=once)
    return pl.pallas_call(
        _peer_dense_kernel,
        grid=(ntok // tt, nexp // PEER_EXPERT_BLK),
        in_specs=[pl.BlockSpec((d, tt), lambda i, j: (0, i), pipeline_mode=once),
                  pl.BlockSpec((PEER_EXPERT_BLK, d), lambda i, j: (j, 0)),
                  pl.BlockSpec((d, PEER_EXPERT_BLK), lambda i, j: (0, j)),
                  per_key, per_key, per_key, per_key],
        out_specs=pl.BlockSpec((tt, d), lambda i, j: (i, 0)),
        out_shape=jax.ShapeDtypeStruct((ntok, d), F32),
        scratch_shapes=[pltpu.VMEM((d, tt), F32), pltpu.VMEM((PEER_EXPERT_BLK, tt), F32),
                        pltpu.VMEM((PEER_EXPERT_BLK, tt), BF16)],
        compiler_params=_cp("parallel", "arbitrary"),
        name="peer_dense",
    )(hmt, ub, vt, n1, e1, r2, e2)


def _peer_res_kernel(h_ref, f_ref, mod_ref, lng_ref, lnb_ref, o_ref, *, alpha):
    mod = mod_ref[0]
    o_ref[0] = _ln(alpha * h_ref[0] + mod[5:6] * f_ref[0], lng_ref[...], lnb_ref[...])


def _peer_res(h, f, modt, nct, off, lng, lnb, alpha):
    b, l, d = h.shape
    tok = pl.BlockSpec((1, TOK_TILE, d), lambda bi, i: (bi, i, 0))
    full = lambda a: pl.BlockSpec(a.shape, lambda bi, i: (0,) * a.ndim)
    return pl.pallas_call(
        functools.partial(_peer_res_kernel, alpha=alpha),
        grid=(b, l // TOK_TILE),
        in_specs=[tok, tok, pl.BlockSpec((1, N_MOD, d), _mod_index(nct, off)), full(lng), full(lnb)],
        out_specs=tok,
        out_shape=jax.ShapeDtypeStruct((b, l, d), F32),
        compiler_params=_cp("parallel", "parallel"),
        name="peer_res",
    )(h, f.reshape(b, l, d), modt, lng, lnb)


def _rope_tables(seq_len, ctx_len):
    rows = seq_len // GRID_W
    row = np.repeat(np.arange(rows, dtype=np.float32), GRID_W)
    col = np.tile(np.arange(GRID_W, dtype=np.float32), rows)
    n_freq = MLA_ROPE // 4
    inv = jnp.power(ROPE_BASE, -jnp.arange(n_freq, dtype=F32) / n_freq)
    ang = jnp.concatenate([row[:, None] * inv, col[:, None] * inv], axis=-1)
    ang = jnp.concatenate([jnp.zeros((ctx_len, MLA_ROPE // 2), F32), ang], axis=0)
    cos, sin = jnp.cos(ang), jnp.sin(ang)
    l = seq_len + ctx_len
    half = MLA_ROPE // 2
    c = jnp.ones((l, LANES), F32).at[:, MLA_NOPE:MLA_NOPE + half].set(cos).at[:, MLA_NOPE + half:MLA_NOPE + 2 * half].set(cos)
    s1 = jnp.zeros((l, LANES), F32).at[:, MLA_NOPE:MLA_NOPE + half].set(-sin)
    s2 = jnp.zeros((l, LANES), F32).at[:, MLA_NOPE + half:MLA_NOPE + 2 * half].set(sin)
    return c, s1, s2


def _ev_weights(w_in, q_norm, kv_norm, w_uq, w_ukv, wa2, ba, seq_len, ctx_len):
    d = w_in.shape[0]
    widths = (MLA_Q_RANK, MLA_KV_RANK, MLA_ROPE, GLA_HEADS * GLA_DK, GLA_HEADS * GLA_DK,
              GLA_HEADS * GLA_DV, GLA_HEADS * GLA_DV, GLA_GATE_RANK, GLA_GATE_RANK)
    offs = np.cumsum((0,) + widths)
    part = lambda i: w_in[:, offs[i]:offs[i + 1]]
    kpe = jnp.zeros((d, HEAD_PAD), F32).at[:, MLA_NOPE:MLA_NOPE + MLA_ROPE].set(part(2))
    gate = jnp.zeros((d, LANES), F32).at[:, :GLA_GATE_RANK].set(part(7)).at[:, GLA_GATE_RANK:2 * GLA_GATE_RANK].set(part(8))
    win = jnp.concatenate([part(0), part(1), kpe, part(3), part(4), part(5), part(6), gate], axis=1).astype(BF16)
    hq = MLA_NOPE + MLA_ROPE
    wuq = jnp.pad(w_uq.reshape(MLA_Q_RANK, MLA_HEADS, hq), ((0, 0), (0, 0), (0, HEAD_PAD - hq)))
    wuq = wuq.reshape(MLA_Q_RANK, MLA_HEADS * HEAD_PAD).astype(BF16)
    ukv = w_ukv.reshape(MLA_KV_RANK, MLA_HEADS, MLA_NOPE + MLA_V)
    wk = jnp.pad(ukv[:, :, :MLA_NOPE], ((0, 0), (0, 0), (0, HEAD_PAD - MLA_NOPE))).reshape(MLA_KV_RANK, MLA_HEADS * HEAD_PAD)
    wv = ukv[:, :, MLA_NOPE:].reshape(MLA_KV_RANK, MLA_HEADS * MLA_V)
    wukv = jnp.concatenate([wk, wv], axis=1).astype(BF16)
    gdk = GLA_HEADS * GLA_DK
    wa2p = jnp.zeros((LANES, 2 * gdk), F32).at[:GLA_GATE_RANK, :gdk].set(wa2[0])
    wa2p = wa2p.at[GLA_GATE_RANK:2 * GLA_GATE_RANK, gdk:].set(wa2[1]).astype(BF16)
    c, s1, s2 = _rope_tables(seq_len, ctx_len)
    return dict(win=win, qn=q_norm[None, :], kvn=kv_norm[None, :], wuq=wuq, wukv=wukv, wa2=wa2p,
                ba=ba.reshape(1, 2 * gdk), rope_c=c, rope_s1=s1, rope_s2=s2)


def _od_weights(conv_w, conv_b, gate_w, gate_b, lam):
    gw = lambda dd: jnp.concatenate([gate_w[dd, 0], gate_w[dd, 1]], axis=-1).astype(BF16)
    return dict(cw=conv_w, cb=conv_b[None, :], gwf=gw(0), gbf=gate_b[0], lamf=lam[0][None, :],
                gwb=gw(1), gbb=gate_b[1], lamb=lam[1][None, :])


def kernel(x, c, ctx, c_ctx, mod_w, mod_b, ln_g, ln_b,
           ev_w_in, ev_q_norm, ev_kv_norm, ev_w_uq, ev_w_ukv, ev_gla_wa2, ev_gla_ba, ev_gla_norm, ev_w_out,
           od_w_in, od_conv_w, od_conv_b, od_gate_w, od_gate_b, od_lambda, od_w_out,
           peer_wq, peer_k1, peer_k2, peer_u, peer_v):
    b, seq_len, d = x.shape
    ctx_len = ctx.shape[1]
    depth = mod_w.shape[0]
    assert ctx_len % TOK_TILE == 0 and seq_len % TOK_TILE == 0 and seq_len % GRID_W == 0
    alpha = (2 * depth) ** 0.25

    rows = -(-(b + 1) // SUBLANES) * SUBLANES
    cc = jnp.zeros((rows, d), F32).at[:b].set(c).at[b].set(c_ctx)

    h = jnp.concatenate([ctx, x], axis=1)
    nct = ctx_len // TOK_TILE
    off = 0
    for layer in range(depth):
        last = layer == depth - 1
        j = layer // 2
        mods = _mod_vectors(cc, mod_w[layer], mod_b[layer][None, :]).reshape(rows, N_MOD, d)
        modt = jnp.stack([jnp.broadcast_to(mods[b], (b, N_MOD, d)), mods[:b]], axis=1).reshape(2 * b, N_MOD, d)
        lng = ln_g[layer][:, None, :]
        lnb = ln_b[layer][:, None, :]
        assert off == 0
        if layer % 2 == 0:
            w = _ev_weights(ev_w_in[j], ev_q_norm[j], ev_kv_norm[j], ev_w_uq[j], ev_w_ukv[j],
                            ev_gla_wa2[j], ev_gla_ba[j], seq_len, ctx_len)
            q, k, v, gq, gk, gv, gg, laf, lab = _ev_proj(h, modt, nct, w)
            att = _attention(q, k, v, nct, ctx_len)
            o_f, o_b = _gla(gq, gk, gv, laf, lab, nct)
            assert not last, "the final layer drops the context rows only after an odd layer"
            h = _ev_out(att, o_f, o_b, gg, ev_gla_norm[j][None, :], ev_w_out[j].astype(BF16), h, modt,
                        lng[0], lnb[0], nct, alpha)
        else:
            w = _od_weights(od_conv_w[j], od_conv_b[j], od_gate_w[j], od_gate_b[j], od_lambda[j])
            br, rnn = _od_proj(h, modt, nct, od_w_in[j].astype(BF16))
            yf, yb = _rglru(rnn, nct, w)
            off = nct if last else 0
            h = _od_out(br, yf, yb, od_w_out[j].astype(BF16), h, modt, lng[0], lnb[0], nct, off, alpha)
        hmt, n1, e1, r2, e2 = _peer_route(h, modt, nct, off, peer_wq[layer], peer_k1[layer], peer_k2[layer])
        f = _peer_dense(hmt, n1, e1, r2, e2, peer_u[layer], peer_v[layer])
        h = _peer_res(h, f, modt, nct, off, lng[1], lnb[1], alpha)
    return h if off else h[:, ctx_len:]
```

```python
import functools

import numpy as np
import jax
import jax.numpy as jnp
from jax import lax
from jax.experimental import pallas as pl
from jax.experimental.pallas import tpu as pltpu

F32 = jnp.float32
BF16 = jnp.bfloat16
HI = lax.Precision.HIGHEST

N_MOD = 6
LN_EPS = 1e-6
GRID_W = 64
MLA_HEADS = 8
MLA_Q_RANK = 384
MLA_KV_RANK = 256
MLA_NOPE = 64
MLA_ROPE = 32
MLA_V = 64
MLA_SCALE = (MLA_NOPE + MLA_ROPE) ** -0.5
LOG2_E = 1.4426950408889634
ROPE_BASE = 10000.0
GLA_HEADS = 4
GLA_DK = 64
GLA_DV = 128
GLA_GATE_RANK = 16
GLA_TAU = 16.0
RG_WIDTH = 1280
RG_BLOCKS = 10
RG_BW = RG_WIDTH // RG_BLOCKS
RG_CONV = 4
RG_CONV_LEFT = 2
RG_C = 8.0
PEER_HEADS = 8
PEER_NK = 128
PEER_TOPK = 16
PEER_QDIM = 256
PEER_HALF = PEER_QDIM // 2

LANES = 128
SUBLANES = 8
TOK_TILE = 256
GLA_CHUNK = 64
GLA_SUB = 16
PEER_DENSE_TOK = 512
PEER_EXPERT_BLK = 1024
VMEM_LIMIT = 56 * 1024 * 1024

HEAD_PAD = LANES


def _cp(*sem):
    return pltpu.CompilerParams(dimension_semantics=sem, vmem_limit_bytes=VMEM_LIMIT)


def _gelu(x):
    return 0.5 * x * (1.0 + lax.erf(x * (2.0 ** -0.5)))


def _ln(y, g, b):
    mu = jnp.mean(y, axis=-1, keepdims=True)
    d = y - mu
    var = jnp.mean(d * d, axis=-1, keepdims=True)
    return d * lax.rsqrt(var + LN_EPS) * g + b


def _mod_index(nct, off):
    def index(b, i):
        return (2 * b + jnp.where(i + off >= nct, 1, 0), 0, 0)
    return index


def _mod_kernel(c_ref, w_ref, b_ref, o_ref):
    x = c_ref[...]
    s = x * jax.nn.sigmoid(x)
    o_ref[...] = jnp.dot(s, w_ref[...], precision=HI, preferred_element_type=F32) + b_ref[...]


def _mod_vectors(cc, w, b):
    rows, d = cc.shape
    n = w.shape[1] // d
    return pl.pallas_call(
        _mod_kernel,
        grid=(n,),
        in_specs=[pl.BlockSpec((rows, d), lambda j: (0, 0)),
                  pl.BlockSpec((d, d), lambda j: (0, j)),
                  pl.BlockSpec((1, d), lambda j: (0, j))],
        out_specs=pl.BlockSpec((rows, d), lambda j: (0, j)),
        out_shape=jax.ShapeDtypeStruct((rows, w.shape[1]), F32),
        compiler_params=_cp("arbitrary"),
        name="mod_vectors",
    )(cc, w, b)


def _rope_block(x, c, s1, s2):
    return x * c + pltpu.roll(x, LANES - MLA_ROPE // 2, axis=1) * s1 + pltpu.roll(x, MLA_ROPE // 2, axis=1) * s2


def _ev_proj_kernel(h_ref, mod_ref, win_ref, qn_ref, kvn_ref, wuq_ref, wukv_ref, wa2_ref, ba_ref,
                    rc_ref, rs1_ref, rs2_ref,
                    q_ref, k_ref, v_ref, gq_ref, gk_ref, gv_ref, gg_ref, laf_ref, lab_ref):
    x = h_ref[0]
    mod = mod_ref[0]
    xin = (x * (1.0 + mod[1:2]) + mod[0:1]).astype(BF16)
    p = jnp.dot(xin, win_ref[...], preferred_element_type=F32)
    o = 0
    cq = p[:, o:o + MLA_Q_RANK]; o += MLA_Q_RANK
    ckv = p[:, o:o + MLA_KV_RANK]; o += MLA_KV_RANK
    kpe = p[:, o:o + HEAD_PAD]; o += HEAD_PAD
    gdk = GLA_HEADS * GLA_DK
    gdv = GLA_HEADS * GLA_DV
    gq_ref[0] = p[:, o:o + gdk] * (GLA_DK ** -0.5); o += gdk
    gk_ref[0] = p[:, o:o + gdk]; o += gdk
    gv_ref[0] = p[:, o:o + gdv]; o += gdv
    gg_ref[0] = p[:, o:o + gdv]; o += gdv
    ga = p[:, o:o + LANES]

    c, s1, s2 = rc_ref[...], rs1_ref[...], rs2_ref[...]
    nq = cq * lax.rsqrt(jnp.mean(cq * cq, axis=-1, keepdims=True) + LN_EPS) * qn_ref[...]
    q = jnp.dot(nq.astype(BF16), wuq_ref[...], preferred_element_type=F32)
    nkv = ckv * lax.rsqrt(jnp.mean(ckv * ckv, axis=-1, keepdims=True) + LN_EPS) * kvn_ref[...]
    kv = jnp.dot(nkv.astype(BF16), wukv_ref[...], preferred_element_type=F32)
    kpe_r = _rope_block(kpe, c, s1, s2)
    for h in range(MLA_HEADS):
        sl = slice(h * HEAD_PAD, (h + 1) * HEAD_PAD)
        q_ref[0, :, sl] = _rope_block(q[:, sl], c, s1, s2).astype(BF16)
        k_ref[0, :, sl] = (kv[:, sl] + kpe_r).astype(BF16)
    v_ref[0] = kv[:, MLA_HEADS * HEAD_PAD:].astype(BF16)

    z = jnp.dot(ga.astype(BF16), wa2_ref[...], preferred_element_type=F32) + ba_ref[...]
    la = jax.nn.log_sigmoid(z) * (1.0 / GLA_TAU)
    laf_ref[0] = la[:, :gdk]
    lab_ref[0] = la[:, gdk:]


def _ev_proj(h, modt, nct, w):
    b, l, d = h.shape
    nt = l // TOK_TILE
    tok = lambda width: pl.BlockSpec((1, TOK_TILE, width), lambda bi, i: (bi, i, 0))
    full = lambda a: pl.BlockSpec(a.shape, lambda bi, i: (0,) * a.ndim)
    rope = pl.BlockSpec((TOK_TILE, LANES), lambda bi, i: (i, 0))
    gdk = GLA_HEADS * GLA_DK
    gdv = GLA_HEADS * GLA_DV
    widths = [(MLA_HEADS * HEAD_PAD, BF16), (MLA_HEADS * HEAD_PAD, BF16), (MLA_HEADS * MLA_V, BF16),
              (gdk, F32), (gdk, F32), (gdv, F32), (gdv, F32), (gdk, F32), (gdk, F32)]
    return pl.pallas_call(
        _ev_proj_kernel,
        grid=(b, nt),
        in_specs=[tok(d), pl.BlockSpec((1, N_MOD, d), _mod_index(nct, 0)),
                  full(w["win"]), full(w["qn"]), full(w["kvn"]), full(w["wuq"]), full(w["wukv"]),
                  full(w["wa2"]), full(w["ba"]), rope, rope, rope],
        out_specs=[tok(wd) for wd, _ in widths],
        out_shape=[jax.ShapeDtypeStruct((b, l, wd), dt) for wd, dt in widths],
        compiler_params=_cp("parallel", "parallel"),
        name="ev_proj",
    )(h, modt, w["win"], w["qn"], w["kvn"], w["wuq"], w["wukv"], w["wa2"], w["ba"],
      w["rope_c"], w["rope_s1"], w["rope_s2"])


def _attn_kernel(q_ref, k_ref, v_ref, o_ref, *, nct, ctx_len):
    def attend(nk):
        lane = lax.broadcasted_iota(jnp.int32, (TOK_TILE, LANES), 1)
        for hp in range(MLA_HEADS // 2):
            vp = v_ref[0, 0:nk, hp * LANES:(hp + 1) * LANES]
            outs = []
            for hh in range(2):
                h = 2 * hp + hh
                qh = q_ref[0, :, h * HEAD_PAD:(h + 1) * HEAD_PAD]
                kh = k_ref[0, 0:nk, h * HEAD_PAD:(h + 1) * HEAD_PAD]
                s = lax.dot_general(qh, kh, (((1,), (1,)), ((), ())), preferred_element_type=F32)
                m = jnp.max(s, axis=-1, keepdims=True)
                p = jnp.exp2((s - m) * (MLA_SCALE * LOG2_E))
                den = jnp.sum(p, axis=-1, keepdims=True)
                o = jnp.dot(p.astype(BF16), vp, preferred_element_type=F32)
                outs.append(o / den)
            o_ref[0, :, hp * LANES:(hp + 1) * LANES] = jnp.where(lane < MLA_V, outs[0], outs[1]).astype(BF16)

    is_ctx = pl.program_id(1) < nct

    @pl.when(is_ctx)
    def _():
        attend(ctx_len)

    @pl.when(jnp.logical_not(is_ctx))
    def _():
        attend(k_ref.shape[1])


def _attention(q, k, v, nct, ctx_len):
    b, l, _ = q.shape
    nt = l // TOK_TILE
    return pl.pallas_call(
        functools.partial(_attn_kernel, nct=nct, ctx_len=ctx_len),
        grid=(b, nt),
        in_specs=[pl.BlockSpec((1, TOK_TILE, q.shape[2]), lambda bi, i: (bi, i, 0)),
                  pl.BlockSpec((1, l, k.shape[2]), lambda bi, i: (bi, 0, 0)),
                  pl.BlockSpec((1, l, v.shape[2]), lambda bi, i: (bi, 0, 0))],
        out_specs=pl.BlockSpec((1, TOK_TILE, v.shape[2]), lambda bi, i: (bi, i, 0)),
        out_shape=jax.ShapeDtypeStruct((b, l, v.shape[2]), BF16),
        compiler_params=_cp("parallel", "arbitrary"),
        name="mla_attention",
    )(q, k, v)


def _bwd_tile(s, nct, nt):
    return jnp.where(s < nct, nct - 1 - s, nt - 1 - (s - nct))


def _gla_chunk(q, k, v, la, st, rev):
    c = GLA_CHUNK
    nsub = c // GLA_SUB
    row = lax.broadcasted_iota(jnp.int32, (c, c), 0)
    col = lax.broadcasted_iota(jnp.int32, (c, c), 1)
    tri = (col >= row) if rev else (col <= row)
    cum = jnp.dot(tri.astype(F32), la, precision=HI, preferred_element_type=F32)
    rid = lax.broadcasted_iota(jnp.int32, (c, LANES), 0)
    lane = lax.broadcasted_iota(jnp.int32, (c, LANES), 1)
    head0 = lane < GLA_DK

    qparts, kparts, eref = [], [], jnp.zeros((c, LANES), F32)
    for jb in range(nsub):
        r = jb * GLA_SUB if rev else jb * GLA_SUB + GLA_SUB - 1
        ej = cum[r:r + 1, :]
        valid = (rid <= jb * GLA_SUB + GLA_SUB - 1) if rev else (rid >= jb * GLA_SUB)
        qparts.append(jnp.where(valid, q * jnp.exp(jnp.where(valid, cum - ej, 0.0)), 0.0))
        inblk = (rid >= jb * GLA_SUB) & (rid < (jb + 1) * GLA_SUB)
        eref = jnp.where(inblk, ej, eref)
    kt = k * jnp.exp(eref - cum)
    for jb in range(nsub):
        inblk = (rid >= jb * GLA_SUB) & (rid < (jb + 1) * GLA_SUB)
        kparts.append(jnp.where(inblk, kt, 0.0))
    qbig = jnp.concatenate(qparts, axis=1)
    kbig = jnp.concatenate(kparts, axis=1).astype(BF16)
    h0big = lax.broadcasted_iota(jnp.int32, (c, nsub * LANES), 1) % LANES < GLA_DK
    qst = jnp.concatenate([jnp.where(h0big, qbig, 0.0), jnp.where(h0big, 0.0, qbig)], axis=0).astype(BF16)
    sc = lax.dot_general(qst, kbig, (((1,), (1,)), ((), ())), preferred_element_type=F32)
    ri = lax.broadcasted_iota(jnp.int32, (2 * c, c), 0) % c
    ci = lax.broadcasted_iota(jnp.int32, (2 * c, c), 1)
    sc = jnp.where((ci >= ri) if rev else (ci <= ri), sc, 0.0)
    vb = v.astype(BF16)
    intra = jnp.dot(sc.astype(BF16), vb, preferred_element_type=F32)
    qs = (q * jnp.exp(cum)).astype(BF16)
    inter = lax.dot_general(qs, st.astype(BF16), (((1,), (1,)), ((), ())), preferred_element_type=F32)
    out = jnp.concatenate([intra[0:c, 0:GLA_DV] + inter[:, 0:GLA_DV],
                           intra[c:2 * c, GLA_DV:2 * GLA_DV] + inter[:, GLA_DV:2 * GLA_DV]], axis=1)
    last = cum[0:1, :] if rev else cum[c - 1:c, :]
    kd = (k * jnp.exp(last - cum)).astype(BF16)
    upd = lax.dot_general(vb, kd, (((0,), (0,)), ((), ())), preferred_element_type=F32)
    srow = lax.broadcasted_iota(jnp.int32, (2 * GLA_DV, LANES), 0)
    slane = lax.broadcasted_iota(jnp.int32, (2 * GLA_DV, LANES), 1)
    diag = (srow < GLA_DV) == (slane < GLA_DK)
    st_new = jnp.where(diag, st * jnp.exp(last) + upd, 0.0)
    return out, st_new


def _gla_kernel(qf_ref, kf_ref, vf_ref, laf_ref, qb_ref, kb_ref, vb_ref, lab_ref, of_ref, ob_ref, stf_ref, stb_ref):
    @pl.when(pl.program_id(1) == 0)
    def _():
        stf_ref[...] = jnp.zeros_like(stf_ref)
        stb_ref[...] = jnp.zeros_like(stb_ref)

    nchunk = TOK_TILE // GLA_CHUNK
    npair = GLA_HEADS // 2

    def body(ci, carry):
        rf = pl.multiple_of(ci * GLA_CHUNK, GLA_CHUNK)
        rb = pl.multiple_of((nchunk - 1 - ci) * GLA_CHUNK, GLA_CHUNK)
        for pr in range(npair):
            ks = slice(pr * LANES, (pr + 1) * LANES)
            vs = slice(pr * 2 * GLA_DV, (pr + 1) * 2 * GLA_DV)
            o, st = _gla_chunk(qf_ref[0, pl.ds(rf, GLA_CHUNK), ks], kf_ref[0, pl.ds(rf, GLA_CHUNK), ks],
                               vf_ref[0, pl.ds(rf, GLA_CHUNK), vs], laf_ref[0, pl.ds(rf, GLA_CHUNK), ks],
                               stf_ref[pr], False)
            of_ref[0, pl.ds(rf, GLA_CHUNK), vs] = o
            stf_ref[pr] = st
            o, st = _gla_chunk(qb_ref[0, pl.ds(rb, GLA_CHUNK), ks], kb_ref[0, pl.ds(rb, GLA_CHUNK), ks],
                               vb_ref[0, pl.ds(rb, GLA_CHUNK), vs], lab_ref[0, pl.ds(rb, GLA_CHUNK), ks],
                               stb_ref[pr], True)
            ob_ref[0, pl.ds(rb, GLA_CHUNK), vs] = o
            stb_ref[pr] = st
        return carry

    lax.fori_loop(0, nchunk, body, 0)


def _gla(gq, gk, gv, laf, lab, nct):
    b, l, dk = gq.shape
    dv = gv.shape[2]
    nt = l // TOK_TILE
    fwd = lambda width: pl.BlockSpec((1, TOK_TILE, width), lambda bi, s: (bi, s, 0))
    bwd = lambda width: pl.BlockSpec((1, TOK_TILE, width), lambda bi, s: (bi, _bwd_tile(s, nct, nt), 0))
    npair = GLA_HEADS // 2
    return pl.pallas_call(
        _gla_kernel,
        grid=(b, nt),
        in_specs=[fwd(dk), fwd(dk), fwd(dv), fwd(dk), bwd(dk), bwd(dk), bwd(dv), bwd(dk)],
        out_specs=[fwd(dv), bwd(dv)],
        out_shape=[jax.ShapeDtypeStruct((b, l, dv), F32)] * 2,
        scratch_shapes=[pltpu.VMEM((npair, 2 * GLA_DV, LANES), F32)] * 2,
        compiler_params=_cp("parallel", "arbitrary"),
        name="gla_scan",
    )(gq, gk, gv, laf, gq, gk, gv, lab)


def _ev_out_kernel(att_ref, of_ref, ob_ref, gg_ref, gn_ref, wout_ref, h_ref, mod_ref, lng_ref, lnb_ref, o_ref, *, alpha):
    o = of_ref[0] + ob_ref[0]
    g = gg_ref[0]
    parts = []
    for hd in range(GLA_HEADS):
        sl = slice(hd * GLA_DV, (hd + 1) * GLA_DV)
        oh = o[:, sl]
        y = oh * lax.rsqrt(jnp.mean(oh * oh, axis=-1, keepdims=True) + LN_EPS) * gn_ref[:, sl]
        gh = g[:, sl]
        parts.append((y * (gh * jax.nn.sigmoid(gh))).astype(BF16))
    gla = jnp.concatenate(parts, axis=1)
    na = att_ref.shape[2]
    mix = (jnp.dot(att_ref[0], wout_ref[0:na, :], preferred_element_type=F32)
           + jnp.dot(gla, wout_ref[na:, :], preferred_element_type=F32))
    mod = mod_ref[0]
    o_ref[0] = _ln(alpha * h_ref[0] + mod[2:3] * mix, lng_ref[...], lnb_ref[...])


def _ev_out(att, o_f, o_b, gg, gn, wout, h, modt, lng, lnb, nct, alpha):
    b, l, d = h.shape
    nt = l // TOK_TILE
    tok = lambda a: pl.BlockSpec((1, TOK_TILE, a.shape[2]), lambda bi, i: (bi, i, 0))
    full = lambda a: pl.BlockSpec(a.shape, lambda bi, i: (0,) * a.ndim)
    return pl.pallas_call(
        functools.partial(_ev_out_kernel, alpha=alpha),
        grid=(b, nt),
        in_specs=[tok(att), tok(o_f), tok(o_b), tok(gg), full(gn), full(wout), tok(h),
                  pl.BlockSpec((1, N_MOD, d), _mod_index(nct, 0)), full(lng), full(lnb)],
        out_specs=tok(h),
        out_shape=jax.ShapeDtypeStruct((b, l, d), F32),
        compiler_params=_cp("parallel", "parallel"),
        name="ev_out",
    )(att, o_f, o_b, gg, gn, wout, h, modt, lng, lnb)


def _od_proj_kernel(h_ref, mod_ref, win_ref, br_ref, rnn_ref):
    mod = mod_ref[0]
    xin = (h_ref[0] * (1.0 + mod[1:2]) + mod[0:1]).astype(BF16)
    p = jnp.dot(xin, win_ref[...], preferred_element_type=F32)
    br_ref[0] = p[:, :RG_WIDTH]
    rnn_ref[0] = p[:, RG_WIDTH:]


def _od_proj(h, modt, nct, win):
    b, l, d = h.shape
    nt = l // TOK_TILE
    tok = lambda width: pl.BlockSpec((1, TOK_TILE, width), lambda bi, i: (bi, i, 0))
    return pl.pallas_call(
        _od_proj_kernel,
        grid=(b, nt),
        in_specs=[tok(d), pl.BlockSpec((1, N_MOD, d), _mod_index(nct, 0)),
                  pl.BlockSpec(win.shape, lambda bi, i: (0, 0))],
        out_specs=[tok(RG_WIDTH), tok(RG_WIDTH)],
        out_shape=[jax.ShapeDtypeStruct((b, l, RG_WIDTH), F32)] * 2,
        compiler_params=_cp("parallel", "parallel"),
        name="od_proj",
    )(h, modt, win)


def _rg_prepare(z_ref, pv_ref, nx_ref, prev_zero, next_zero, cw_ref, cb_ref, gw_ref, gb_ref, lam_ref, a_ref, x_ref):
    t = TOK_TILE
    z = z_ref[0]
    pv = jnp.where(prev_zero, 0.0, pv_ref[0])
    nx = jnp.where(next_zero, 0.0, nx_ref[0])
    ext = jnp.concatenate([pv, z, nx], axis=0)
    base = SUBLANES - RG_CONV_LEFT
    y = cb_ref[...] + ext[base:base + t] * cw_ref[0:1, :]
    for tap in range(1, RG_CONV):
        y = y + ext[base + tap:base + tap + t] * cw_ref[tap:tap + 1, :]
    for n in range(RG_BLOCKS):
        sl = slice(n * RG_BW, (n + 1) * RG_BW)
        yb = y[:, sl]
        ri = jnp.dot(yb.astype(BF16), gw_ref[n], preferred_element_type=F32)
        r = jax.nn.sigmoid(ri[:, :RG_BW] + gb_ref[0:1, sl])
        i = jax.nn.sigmoid(ri[:, RG_BW:] + gb_ref[1:2, sl])
        log_a = (-RG_C) * jax.nn.softplus(-lam_ref[:, sl]) * r
        th = jnp.tanh(log_a)
        a_ref[:, sl] = jnp.exp(log_a)
        x_ref[:, sl] = jnp.sqrt(-2.0 * th / (1.0 - th)) * (i * yb)


def _rglru_kernel(zf_ref, pf_ref, nf_ref, zb_ref, pb_ref, nb_ref, cw_ref, cb_ref,
                  gwf_ref, gbf_ref, lamf_ref, gwb_ref, gbb_ref, lamb_ref,
                  yf_ref, yb_ref, af_ref, xf_ref, ab_ref, xb_ref, hf_ref, hb_ref, *, nct, nt):
    s = pl.program_id(1)

    @pl.when(s == 0)
    def _():
        hf_ref[...] = jnp.zeros_like(hf_ref)
        hb_ref[...] = jnp.zeros_like(hb_ref)

    def edges(j):
        return (j == 0) | (j == nct), (j == nct - 1) | (j == nt - 1)

    pz, nz = edges(s)
    _rg_prepare(zf_ref, pf_ref, nf_ref, pz, nz, cw_ref, cb_ref, gwf_ref, gbf_ref, lamf_ref, af_ref, xf_ref)
    pz, nz = edges(_bwd_tile(s, nct, nt))
    _rg_prepare(zb_ref, pb_ref, nb_ref, pz, nz, cw_ref, cb_ref, gwb_ref, gbb_ref, lamb_ref, ab_ref, xb_ref)

    def body(kk, carry):
        hf, hb = carry
        tf = kk
        tb = TOK_TILE - 1 - kk
        hf = af_ref[pl.ds(tf, 1), :] * hf + xf_ref[pl.ds(tf, 1), :]
        yf_ref[0, pl.ds(tf, 1), :] = hf
        hb = ab_ref[pl.ds(tb, 1), :] * hb + xb_ref[pl.ds(tb, 1), :]
        yb_ref[0, pl.ds(tb, 1), :] = hb
        return hf, hb

    hf, hb = lax.fori_loop(0, TOK_TILE, body, (hf_ref[...], hb_ref[...]), unroll=8)
    hf_ref[...] = hf
    hb_ref[...] = hb


def _rglru(rnn, nct, w):
    b, l, wd = rnn.shape
    nt = l // TOK_TILE
    per = TOK_TILE // SUBLANES
    nh = l // SUBLANES
    bt = lambda s: _bwd_tile(s, nct, nt)
    tile = lambda f: pl.BlockSpec((1, TOK_TILE, wd), lambda bi, s: (bi, f(s), 0))
    prev = lambda f: pl.BlockSpec((1, SUBLANES, wd), lambda bi, s: (bi, jnp.maximum(f(s) * per - 1, 0), 0))
    nxt = lambda f: pl.BlockSpec((1, SUBLANES, wd), lambda bi, s: (bi, jnp.minimum((f(s) + 1) * per, nh - 1), 0))
    full = lambda a: pl.BlockSpec(a.shape, lambda bi, s: (0,) * a.ndim)
    ident = lambda s: s
    vm = lambda rows: pltpu.VMEM((rows, wd), F32)
    return pl.pallas_call(
        functools.partial(_rglru_kernel, nct=nct, nt=nt),
        grid=(b, nt),
        in_specs=[tile(ident), prev(ident), nxt(ident), tile(bt), prev(bt), nxt(bt),
                  full(w["cw"]), full(w["cb"]), full(w["gwf"]), full(w["gbf"]), full(w["lamf"]),
                  full(w["gwb"]), full(w["gbb"]), full(w["lamb"])],
        out_specs=[tile(ident), tile(bt)],
        out_shape=[jax.ShapeDtypeStruct((b, l, wd), F32)] * 2,
        scratch_shapes=[vm(TOK_TILE), vm(TOK_TILE), vm(TOK_TILE), vm(TOK_TILE), vm(1), vm(1)],
        compiler_params=_cp("parallel", "arbitrary"),
        name="rglru_scan",
    )(rnn, rnn, rnn, rnn, rnn, rnn, w["cw"], w["cb"], w["gwf"], w["gbf"], w["lamf"], w["gwb"], w["gbb"], w["lamb"])


def _od_out_kernel(br_ref, yf_ref, yb_ref, wout_ref, h_ref, mod_ref, lng_ref, lnb_ref, o_ref, *, alpha):
    m = (_gelu(br_ref[0]) * (yf_ref[0] + yb_ref[0])).astype(BF16)
    mix = jnp.dot(m, wout_ref[...], preferred_element_type=F32)
    mod = mod_ref[0]
    o_ref[0] = _ln(alpha * h_ref[0] + mod[2:3] * mix, lng_ref[...], lnb_ref[...])


def _od_out(br, yf, yb, wout, h, modt, lng, lnb, nct, off, alpha):
    b, l, d = h.shape
    nt = l // TOK_TILE - off
    tok = lambda a: pl.BlockSpec((1, TOK_TILE, a.shape[2]), lambda bi, i: (bi, i + off, 0))
    full = lambda a: pl.BlockSpec(a.shape, lambda bi, i: (0,) * a.ndim)
    return pl.pallas_call(
        functools.partial(_od_out_kernel, alpha=alpha),
        grid=(b, nt),
        in_specs=[tok(br), tok(yf), tok(yb), full(wout), tok(h),
                  pl.BlockSpec((1, N_MOD, d), _mod_index(nct, off)), full(lng), full(lnb)],
        out_specs=pl.BlockSpec((1, TOK_TILE, d), lambda bi, i: (bi, i, 0)),
        out_shape=jax.ShapeDtypeStruct((b, nt * TOK_TILE, d), F32),
        compiler_params=_cp("parallel", "parallel"),
        name="od_out",
    )(br, yf, yb, wout, h, modt, lng, lnb)


def _top16_rows(s, vals_ref, exact):
    nk = s.shape[0]
    top = float(PEER_TOPK)
    rid = lax.broadcasted_iota(jnp.int32, s.shape, 0).astype(F32)
    rank = jnp.full(s.shape, top, F32)
    left = s
    for r in range(PEER_TOPK):
        m = jnp.max(left, axis=0, keepdims=True)
        hit = left == m
        if exact:
            hit = rid == jnp.min(jnp.where(hit, rid, float(nk)), axis=0, keepdims=True)
        vals_ref[r:r + 1, :] = m
        rank = jnp.where(hit, float(r), rank)
        left = jnp.where(hit, -jnp.inf, left)
    picked = jnp.sum(jnp.where(rank < top, 1.0, 0.0), axis=0, keepdims=True)
    ex = jnp.where(rank < top, jnp.exp(s - vals_ref[0:1, :]), 0.0)
    return rank, ex, jnp.abs(picked - top)


def _cand_sums(t1, t2):
    blocks = [t1[0:1] + t2[0:16]]
    for a in range(1, 8):
        blocks.append(t1[a:a + 1] + t2[0:8])
    blocks.append(t1[8:16] + t2[0:1])
    return jnp.concatenate(blocks, axis=0)


def _selected_counts(t1, t2, flat, exact):
    cs0 = _cand_sums(t1, t2)
    cs = cs0
    sel = jnp.zeros(cs.shape, F32)
    for _ in range(PEER_TOPK):
        m = jnp.max(cs, axis=0, keepdims=True)
        hit = cs == m
        if exact:
            hit = flat == jnp.min(jnp.where(hit, flat, float(PEER_TOPK * PEER_TOPK)), axis=0, keepdims=True)
        sel = jnp.where(hit, 1.0, sel)
        cs = jnp.where(hit, -jnp.inf, cs)
    picked = jnp.sum(sel, axis=0, keepdims=True)
    z = jnp.sum(sel * jnp.exp(cs0 - cs0[0:1]), axis=0, keepdims=True)
    counts = [jnp.sum(sel[0:16], axis=0, keepdims=True)]
    for a in range(1, 8):
        counts.append(jnp.sum(sel[8 + 8 * a:16 + 8 * a], axis=0, keepdims=True))
    for a in range(8, 16):
        counts.append(sel[64 + a:65 + a])
    return counts, z, jnp.abs(picked - float(PEER_TOPK))


def _cand_flat_index(ntok):
    a16 = lax.broadcasted_iota(jnp.int32, (16, ntok), 0)
    a8 = lax.broadcasted_iota(jnp.int32, (8, ntok), 0)
    blocks = [a16]
    for a in range(1, 8):
        blocks.append(a * PEER_TOPK + a8)
    blocks.append((a8 + 8) * PEER_TOPK)
    return jnp.concatenate(blocks, axis=0).astype(F32)


def _split_bf16(x):
    hi = x.astype(BF16)
    return hi, (x - hi.astype(F32)).astype(BF16)


def _dot3(a, b, dims):
    dg = lambda x, y: lax.dot_general(x, y, dims, preferred_element_type=F32)
    return dg(a[0], b[0]) + (dg(a[0], b[1]) + dg(a[1], b[0]))


def _peer_route_kernel(h_ref, mod_ref, wq_ref, k_ref, hmt_ref, n1_ref, e1_ref, r2_ref, e2_ref, vals_ref):
    mod = mod_ref[0]
    hm = h_ref[0] * (1.0 + mod[4:5]) + mod[3:4]
    hmt_ref[...] = hm.T.astype(BF16)
    q = _dot3(_split_bf16(hm), (wq_ref[0], wq_ref[1]), (((1,), (0,)), ((), ())))
    flat = _cand_flat_index(LANES)
    nn = (((1,), (1,)), ((), ()))
    qs = _split_bf16(q)
    ntile = hm.shape[0] // LANES

    def route(exact):
        off_count = jnp.zeros((1, LANES), F32)
        for h in range(PEER_HEADS):
            c1 = slice(h * PEER_QDIM, h * PEER_QDIM + PEER_HALF)
            c2 = slice(h * PEER_QDIM + PEER_HALF, (h + 1) * PEER_QDIM)
            s1 = _dot3((k_ref[0], k_ref[1]), (qs[0][:, c1], qs[1][:, c1]), nn)
            s2 = _dot3((k_ref[2], k_ref[3]), (qs[0][:, c2], qs[1][:, c2]), nn)
            for tc in range(ntile):
                cols = slice(tc * LANES, (tc + 1) * LANES)
                t1_ref = vals_ref.at[(h * ntile + tc) * 2]
                t2_ref = vals_ref.at[(h * ntile + tc) * 2 + 1]
                r1, e1, d1 = _top16_rows(s1[:, cols], t1_ref, exact)
                r2, e2, d2 = _top16_rows(s2[:, cols], t2_ref, exact)
                counts, z, d3 = _selected_counts(t1_ref[...], t2_ref[...], flat, exact)
                off_count = jnp.maximum(off_count, jnp.maximum(jnp.maximum(d1, d2), d3))
                n1 = jnp.zeros(r1.shape, F32)
                for a in range(PEER_TOPK):
                    n1 = jnp.where(r1 == float(a), counts[a], n1)
                n1_ref[h, :, cols] = n1
                e1_ref[h, :, cols] = e1 / z
                r2_ref[h, :, cols] = r2
                e2_ref[h, :, cols] = e2
        return off_count

    off_count = route(False)

    @pl.when(jnp.max(off_count) > 0.0)
    def _():
        route(True)


def _peer_route(h, modt, nct, off, wq, k1, k2):
    b, l, d = h.shape
    nt = l // TOK_TILE
    ntok = b * l
    full = lambda a: pl.BlockSpec(a.shape, lambda bi, i: (0,) * a.ndim)
    top_vals = pltpu.VMEM((PEER_HEADS * (TOK_TILE // LANES) * 2, PEER_TOPK, LANES), F32)
    wq = jnp.stack(_split_bf16(wq))
    keys = jnp.stack(_split_bf16(k1) + _split_bf16(k2))
    per_key = pl.BlockSpec((PEER_HEADS, PEER_NK, TOK_TILE), lambda bi, i: (0, 0, bi * nt + i))
    per_key_shape = jax.ShapeDtypeStruct((PEER_HEADS, PEER_NK, ntok), F32)
    return pl.pallas_call(
        _peer_route_kernel,
        grid=(b, nt),
        in_specs=[pl.BlockSpec((1, TOK_TILE, d), lambda bi, i: (bi, i, 0)),
                  pl.BlockSpec((1, N_MOD, d), _mod_index(nct, off)), full(wq), full(keys)],
        out_specs=[pl.BlockSpec((d, TOK_TILE), lambda bi, i: (0, bi * nt + i))] + [per_key] * 4,
        out_shape=[jax.ShapeDtypeStruct((d, ntok), BF16)] + [per_key_shape] * 4,
        scratch_shapes=[top_vals],
        compiler_params=_cp("parallel", "parallel"),
        name="peer_route",
    )(h, modt, wq, keys)


def _peer_dense_kernel(hmt_ref, u_ref, vt_ref, n1_ref, e1_ref, r2_ref, e2_ref, o_ref, acc_ref, s_ref, w_ref):
    j = pl.program_id(1)

    @pl.when(j == 0)
    def _():
        acc_ref[...] = jnp.zeros_like(acc_ref)

    s_ref[...] = jnp.dot(u_ref[...], hmt_ref[...], preferred_element_type=F32)
    per_step = u_ref.shape[0] // PEER_NK
    for al in range(per_step):
        a = j * per_step + al
        rows = slice(al * PEER_NK, (al + 1) * PEER_NK)
        n1 = [n1_ref[h, pl.ds(a, 1), :] for h in range(PEER_HEADS)]
        e1 = [e1_ref[h, pl.ds(a, 1), :] for h in range(PEER_HEADS)]
        for tc in range(s_ref.shape[1] // LANES):
            cols = slice(tc * LANES, (tc + 1) * LANES)
            g = jnp.zeros((PEER_NK, LANES), F32)
            for h in range(PEER_HEADS):
                g = g + jnp.where(r2_ref[h, :, cols] < n1[h][:, cols], e2_ref[h, :, cols] * e1[h][:, cols], 0.0)
            w_ref[rows, cols] = (_gelu(s_ref[rows, cols]) * g).astype(BF16)
    acc_ref[...] += jnp.dot(vt_ref[...], w_ref[...], preferred_element_type=F32)

    @pl.when(j == pl.num_programs(1) - 1)
    def _():
        o_ref[...] = acc_ref[...].T


def _peer_dense(hmt, n1, e1, r2, e2, u, v):
    d, ntok = hmt.shape
    nexp = u.shape[0]
    tt = PEER_DENSE_TOK if ntok % PEER_DENSE_TOK == 0 else TOK_TILE
    ub = u.astype(BF16)
    vt = v.T.astype(BF16)
    once = pl.Buffered(1)
    per_key = pl.BlockSpec((PEER_HEADS, PEER_NK, tt), lambda i, j: (0, 0, i), pipeline_mode=once)
    return pl.pallas_call(
        _peer_dense_kernel,
        grid=(ntok // tt, nexp // PEER_EXPERT_BLK),
        in_specs=[pl.BlockSpec((d, tt), lambda i, j: (0, i), pipeline_mode=once),
                  pl.BlockSpec((PEER_EXPERT_BLK, d), lambda i, j: (j, 0)),
                  pl.BlockSpec((d, PEER_EXPERT_BLK), lambda i, j: (0, j)),
                  per_key, per_key, per_key, per_key],
        out_specs=pl.BlockSpec((tt, d), lambda i, j: (i, 0)),
        out_shape=jax.ShapeDtypeStruct((ntok, d), F32),
        scratch_shapes=[pltpu.VMEM((d, tt), F32), pltpu.VMEM((PEER_EXPERT_BLK, tt), F32),
                        pltpu.VMEM((PEER_EXPERT_BLK, tt), BF16)],
        compiler_params=_cp("parallel", "arbitrary"),
        name="peer_dense",
    )(hmt, ub, vt, n1, e1, r2, e2)


def _peer_res_kernel(h_ref, f_ref, mod_ref, lng_ref, lnb_ref, o_ref, *, alpha):
    mod = mod_ref[0]
    o_ref[0] = _ln(alpha * h_ref[0] + mod[5:6] * f_ref[0], lng_ref[...], lnb_ref[...])


def _peer_res(h, f, modt, nct, off, lng, lnb, alpha):
    b, l, d = h.shape
    tok = pl.BlockSpec((1, TOK_TILE, d), lambda bi, i: (bi, i, 0))
    full = lambda a: pl.BlockSpec(a.shape, lambda bi, i: (0,) * a.ndim)
    return pl.pallas_call(
        functools.partial(_peer_res_kernel, alpha=alpha),
        grid=(b, l // TOK_TILE),
        in_specs=[tok, tok, pl.BlockSpec((1, N_MOD, d), _mod_index(nct, off)), full(lng), full(lnb)],
        out_specs=tok,
        out_shape=jax.ShapeDtypeStruct((b, l, d), F32),
        compiler_params=_cp("parallel", "parallel"),
        name="peer_res",
    )(h, f.reshape(b, l, d), modt, lng, lnb)


def _rope_tables(seq_len, ctx_len):
    rows = seq_len // GRID_W
    row = np.repeat(np.arange(rows, dtype=np.float32), GRID_W)
    col = np.tile(np.arange(GRID_W, dtype=np.float32), rows)
    n_freq = MLA_ROPE // 4
    inv = jnp.power(ROPE_BASE, -jnp.arange(n_freq, dtype=F32) / n_freq)
    ang = jnp.concatenate([row[:, None] * inv, col[:, None] * inv], axis=-1)
    ang = jnp.concatenate([jnp.zeros((ctx_len, MLA_ROPE // 2), F32), ang], axis=0)
    cos, sin = jnp.cos(ang), jnp.sin(ang)
    l = seq_len + ctx_len
    half = MLA_ROPE // 2
    c = jnp.ones((l, LANES), F32).at[:, MLA_NOPE:MLA_NOPE + half].set(cos).at[:, MLA_NOPE + half:MLA_NOPE + 2 * half].set(cos)
    s1 = jnp.zeros((l, LANES), F32).at[:, MLA_NOPE:MLA_NOPE + half].set(-sin)
    s2 = jnp.zeros((l, LANES), F32).at[:, MLA_NOPE + half:MLA_NOPE + 2 * half].set(sin)
    return c, s1, s2


def _ev_weights(w_in, q_norm, kv_norm, w_uq, w_ukv, wa2, ba, seq_len, ctx_len):
    d = w_in.shape[0]
    widths = (MLA_Q_RANK, MLA_KV_RANK, MLA_ROPE, GLA_HEADS * GLA_DK, GLA_HEADS * GLA_DK,
              GLA_HEADS * GLA_DV, GLA_HEADS * GLA_DV, GLA_GATE_RANK, GLA_GATE_RANK)
    offs = np.cumsum((0,) + widths)
    part = lambda i: w_in[:, offs[i]:offs[i + 1]]
    kpe = jnp.zeros((d, HEAD_PAD), F32).at[:, MLA_NOPE:MLA_NOPE + MLA_ROPE].set(part(2))
    gate = jnp.zeros((d, LANES), F32).at[:, :GLA_GATE_RANK].set(part(7)).at[:, GLA_GATE_RANK:2 * GLA_GATE_RANK].set(part(8))
    win = jnp.concatenate([part(0), part(1), kpe, part(3), part(4), part(5), part(6), gate], axis=1).astype(BF16)
    hq = MLA_NOPE + MLA_ROPE
    wuq = jnp.pad(w_uq.reshape(MLA_Q_RANK, MLA_HEADS, hq), ((0, 0), (0, 0), (0, HEAD_PAD - hq)))
    wuq = wuq.reshape(MLA_Q_RANK, MLA_HEADS * HEAD_PAD).astype(BF16)
    ukv = w_ukv.reshape(MLA_KV_RANK, MLA_HEADS, MLA_NOPE + MLA_V)
    wk = jnp.pad(ukv[:, :, :MLA_NOPE], ((0, 0), (0, 0), (0, HEAD_PAD - MLA_NOPE))).reshape(MLA_KV_RANK, MLA_HEADS * HEAD_PAD)
    wv = ukv[:, :, MLA_NOPE:].reshape(MLA_KV_RANK, MLA_HEADS * MLA_V)
    wukv = jnp.concatenate([wk, wv], axis=1).astype(BF16)
    gdk = GLA_HEADS * GLA_DK
    wa2p = jnp.zeros((LANES, 2 * gdk), F32).at[:GLA_GATE_RANK, :gdk].set(wa2[0])
    wa2p = wa2p.at[GLA_GATE_RANK:2 * GLA_GATE_RANK, gdk:].set(wa2[1]).astype(BF16)
    c, s1, s2 = _rope_tables(seq_len, ctx_len)
    return dict(win=win, qn=q_norm[None, :], kvn=kv_norm[None, :], wuq=wuq, wukv=wukv, wa2=wa2p,
                ba=ba.reshape(1, 2 * gdk), rope_c=c, rope_s1=s1, rope_s2=s2)


def _od_weights(conv_w, conv_b, gate_w, gate_b, lam):
    gw = lambda dd: jnp.concatenate([gate_w[dd, 0], gate_w[dd, 1]], axis=-1).astype(BF16)
    return dict(cw=conv_w, cb=conv_b[None, :], gwf=gw(0), gbf=gate_b[0], lamf=lam[0][None, :],
                gwb=gw(1), gbb=gate_b[1], lamb=lam[1][None, :])


def kernel(x, c, ctx, c_ctx, mod_w, mod_b, ln_g, ln_b,
           ev_w_in, ev_q_norm, ev_kv_norm, ev_w_uq, ev_w_ukv, ev_gla_wa2, ev_gla_ba, ev_gla_norm, ev_w_out,
           od_w_in, od_conv_w, od_conv_b, od_gate_w, od_gate_b, od_lambda, od_w_out,
           peer_wq, peer_k1, peer_k2, peer_u, peer_v):
    b, seq_len, d = x.shape
    ctx_len = ctx.shape[1]
    depth = mod_w.shape[0]
    assert ctx_len % TOK_TILE == 0 and seq_len % TOK_TILE == 0 and seq_len % GRID_W == 0
    alpha = (2 * depth) ** 0.25

    rows = -(-(b + 1) // SUBLANES) * SUBLANES
    cc = jnp.zeros((rows, d), F32).at[:b].set(c).at[b].set(c_ctx)

    h = jnp.concatenate([ctx, x], axis=1)
    nct = ctx_len // TOK_TILE
    off = 0
    for layer in range(depth):
        last = layer == depth - 1
        j = layer // 2
        mods = _mod_vectors(cc, mod_w[layer], mod_b[layer][None, :]).reshape(rows, N_MOD, d)
        modt = jnp.stack([jnp.broadcast_to(mods[b], (b, N_MOD, d)), mods[:b]], axis=1).reshape(2 * b, N_MOD, d)
        lng = ln_g[layer][:, None, :]
        lnb = ln_b[layer][:, None, :]
        assert off == 0
        if layer % 2 == 0:
            w = _ev_weights(ev_w_in[j], ev_q_norm[j], ev_kv_norm[j], ev_w_uq[j], ev_w_ukv[j],
                            ev_gla_wa2[j], ev_gla_ba[j], seq_len, ctx_len)
            q, k, v, gq, gk, gv, gg, laf, lab = _ev_proj(h, modt, nct, w)
            att = _attention(q, k, v, nct, ctx_len)
            o_f, o_b = _gla(gq, gk, gv, laf, lab, nct)
            assert not last, "the final layer drops the context rows only after an odd layer"
            h = _ev_out(att, o_f, o_b, gg, ev_gla_norm[j][None, :], ev_w_out[j].astype(BF16), h, modt,
                        lng[0], lnb[0], nct, alpha)
        else:
            w = _od_weights(od_conv_w[j], od_conv_b[j], od_gate_w[j], od_gate_b[j], od_lambda[j])
            br, rnn = _od_proj(h, modt, nct, od_w_in[j].astype(BF16))
            yf, yb = _rglru(rnn, nct, w)
            off = nct if last else 0
            h = _od_out(br, yf, yb, od_w_out[j].astype(BF16), h, modt, lng[0], lnb[0], nct, off, alpha)
        hmt, n1, e1, r2, e2 = _peer_route(h, modt, nct, off, peer_wq[layer], peer_k1[layer], peer_k2[layer])
        f = _peer_dense(hmt, n1, e1, r2, e2, peer_u[layer], peer_v[layer])
        h = _peer_res(h, f, modt, nct, off, lng[1], lnb[1], alpha)
    return h if off else h[:, ctx_len:]
```
